```python
import math
import jax, jax.numpy as jnp
from jax import lax
import numpy as np

D_MODEL = 1024
BATCH = 8
SEQ = 2048
DEPTH = 2
DEC_BATCH = 32
DEC_SEQ = 1
PAST_LEN = 16384
PAGE_SIZE = 128

HEAD_DIM = 64
N_HEADS_MOBA = 8
N_HEADS_FOX = 8
ROT_DIM = HEAD_DIM // 4
ROPE_THETA = 500000.0
MOBA_BLOCK = 256
MOBA_TOPK = 3
MOBA_Q_CHUNK = 32
FOX_Q_BLOCK = 128
FOX_F_BIAS = 3.0
ATT_W_A = N_HEADS_MOBA * HEAD_DIM
ATT_W_B = N_HEADS_FOX * HEAD_DIM
ATT_IN = 3 * ATT_W_A + 3 * ATT_W_B + N_HEADS_FOX
D_INNER = 2 * D_MODEL
SSM_HEAD_DIM = 64
SSM_HEADS = D_INNER // SSM_HEAD_DIM
SSM_GROUPS = 4
SSM_HPG = SSM_HEADS // SSM_GROUPS
D_STATE = 128
CONV_W = 4
CONV_DIM = D_INNER + 2 * SSM_GROUPS * D_STATE
SSM_IN = D_INNER + CONV_DIM + SSM_HEADS
SSD_CHUNK = 128
D_FF = ((8 * D_MODEL + 3 * 256 - 1) // (3 * 256)) * 256
N_ATT_LAYERS = (DEPTH + 1) // 2
N_SSM_LAYERS = DEPTH // 2
EPS = 1e-6
F32 = jnp.float32

kernel_name = 'moba_fox_ssd_hybrid_step'


def rms_norm(x, g):
    xf = x.astype(F32)
    y = xf * lax.rsqrt(jnp.mean(xf * xf, axis=-1, keepdims=True) + EPS)
    return (y * g.astype(F32)).astype(x.dtype)


def ada_terms(c, w, b):
    m = jax.nn.silu(c) @ w + b
    return [t[:, None, :] for t in jnp.split(m, 6, axis=-1)]


def partial_rotary(x, pos):
    half = ROT_DIM // 2
    inv = ROPE_THETA ** (-2.0 * jnp.arange(half, dtype=F32) / ROT_DIM)
    ang = pos.astype(F32)[:, None] * inv[None, :]
    cos = jnp.cos(ang)[:, None, :]
    sin = jnp.sin(ang)[:, None, :]
    xr = x[..., :ROT_DIM].astype(F32)
    x1, x2 = xr[..., :half], xr[..., half:]
    rot = jnp.concatenate([x1 * cos - x2 * sin, x1 * sin + x2 * cos], axis=-1).astype(x.dtype)
    return jnp.concatenate([rot, x[..., ROT_DIM:]], axis=-1)


def attn_project(h, w_in, b_f, pos):
    b, t, _ = h.shape
    z = h @ w_in
    cuts = [ATT_W_A, 2 * ATT_W_A, 3 * ATT_W_A, 3 * ATT_W_A + ATT_W_B,
            3 * ATT_W_A + 2 * ATT_W_B, 3 * ATT_W_A + 3 * ATT_W_B]
    qa, ka, va, qb, kb, vb, fl = jnp.split(z, cuts, axis=-1)
    qa = partial_rotary(qa.reshape(b, t, N_HEADS_MOBA, HEAD_DIM), pos)
    ka = partial_rotary(ka.reshape(b, t, N_HEADS_MOBA, HEAD_DIM), pos)
    va = va.reshape(b, t, N_HEADS_MOBA, HEAD_DIM)
    qb = qb.reshape(b, t, N_HEADS_FOX, HEAD_DIM)
    kb = kb.reshape(b, t, N_HEADS_FOX, HEAD_DIM)
    vb = vb.reshape(b, t, N_HEADS_FOX, HEAD_DIM)
    logf = jax.nn.log_sigmoid(fl.astype(F32) + b_f.astype(F32))
    return qa, ka, va, qb, kb, vb, logf


def moba_select(gate):
    nb = gate.shape[-1]
    if nb < MOBA_TOPK:
        gate = jnp.pad(gate, [(0, 0)] * (gate.ndim - 1) + [(0, MOBA_TOPK - nb)], constant_values=-jnp.inf)
    vals, idx = lax.top_k(gate, MOBA_TOPK)
    return jnp.clip(idx, 0, max(nb - 1, 0)), vals > -jnp.inf


def moba_attend(q, ks, vs, valid, ko, vo, qpos, kpos):
    b, sq, h, _ = q.shape
    scale = HEAD_DIM ** -0.5
    nsel = MOBA_TOPK * MOBA_BLOCK
    s_sel = jnp.einsum('bqhd,bhqjkd->bhqjk', q, ks).astype(F32) * scale
    s_sel = jnp.where(valid[..., None], s_sel, -jnp.inf).reshape(b, h, sq, nsel)
    s_own = jnp.einsum('bqhd,bkhd->bhqk', q, ko).astype(F32) * scale
    s_own = jnp.where(kpos[None, :] <= qpos[:, None], s_own, -jnp.inf)
    p = jax.nn.softmax(jnp.concatenate([s_sel, s_own], axis=-1), axis=-1).astype(vs.dtype)
    p_sel = p[..., :nsel].reshape(b, h, sq, MOBA_TOPK, MOBA_BLOCK)
    return (jnp.einsum('bhqjk,bhqjkd->bqhd', p_sel, vs)
            + jnp.einsum('bhqk,bkhd->bqhd', p[..., nsel:], vo))


def moba_prompt(q, k, v):
    b, t, h, d = q.shape
    nb = -(-t // MOBA_BLOCK)
    pad = nb * MOBA_BLOCK - t
    k_blk = jnp.pad(k, ((0, 0), (0, pad), (0, 0), (0, 0))).reshape(b, nb, MOBA_BLOCK, h, d)
    v_blk = jnp.pad(v, ((0, 0), (0, pad), (0, 0), (0, 0))).reshape(b, nb, MOBA_BLOCK, h, d)
    k_mean = jnp.mean(k_blk.astype(F32), axis=2)
    gate = jnp.einsum('bthd,bnhd->bhtn', q.astype(F32), k_mean)
    q_blk = jnp.arange(t) // MOBA_BLOCK
    gate = jnp.where(jnp.arange(nb)[None, :] < q_blk[:, None], gate, -jnp.inf)
    idx, valid = moba_select(gate)
    nc = t // MOBA_Q_CHUNK
    qc = q.reshape(b, nc, MOBA_Q_CHUNK, h, d).transpose(1, 0, 2, 3, 4)
    ic = idx.reshape(b, h, nc, MOBA_Q_CHUNK, MOBA_TOPK).transpose(2, 0, 1, 3, 4)
    vc = valid.reshape(b, h, nc, MOBA_Q_CHUNK, MOBA_TOPK).transpose(2, 0, 1, 3, 4)
    starts = jnp.arange(nc) * MOBA_Q_CHUNK
    bI = jnp.arange(b)[:, None, None, None]
    hI = jnp.arange(h)[None, :, None, None]

    def chunk(args):
        qq, ii, vv, t0 = args
        ks = k_blk[bI, ii, :, hI]
        vs = v_blk[bI, ii, :, hI]
        own = t0 // MOBA_BLOCK
        ko = lax.dynamic_index_in_dim(k_blk, own, axis=1, keepdims=False)
        vo = lax.dynamic_index_in_dim(v_blk, own, axis=1, keepdims=False)
        qpos = t0 + jnp.arange(MOBA_Q_CHUNK)
        kpos = own * MOBA_BLOCK + jnp.arange(MOBA_BLOCK)
        return moba_attend(qq, ks, vs, vv, ko, vo, qpos, kpos)

    o = lax.map(chunk, (qc, ic, vc, starts))
    return o.transpose(1, 0, 2, 3, 4).reshape(b, t, h, d)


def moba_decode(q, k_new, v_new, pool_k, pool_v, li, page_table):
    b, sq, h, d = q.shape
    n_pages = page_table.shape[1]
    past = n_pages * PAGE_SIZE
    nbp = past // MOBA_BLOCK
    full = nbp * MOBA_BLOCK
    rem = past - full
    ppb = MOBA_BLOCK // PAGE_SIZE
    k_past = pool_k[li, page_table].reshape(b, past, h, d)
    k_mean = jnp.mean(k_past[:, :full].reshape(b, nbp, MOBA_BLOCK, h, d).astype(F32), axis=2)
    gate = jnp.einsum('bqhd,bnhd->bhqn', q.astype(F32), k_mean)
    idx, valid = moba_select(gate)
    lp = jnp.clip(idx[..., None] * ppb + jnp.arange(ppb), 0, n_pages - 1)
    phys = page_table[jnp.arange(b)[:, None, None, None, None], lp]
    hI = jnp.arange(h)[None, :, None, None, None]
    ks = pool_k[li, phys, :, hI].reshape(b, h, sq, MOBA_TOPK, MOBA_BLOCK, d)
    vs = pool_v[li, phys, :, hI].reshape(b, h, sq, MOBA_TOPK, MOBA_BLOCK, d)
    own_pages = page_table[:, full // PAGE_SIZE:]
    ko = jnp.concatenate([k_past[:, full:], k_new], axis=1)
    vo = jnp.concatenate([pool_v[li, own_pages].reshape(b, rem, h, d), v_new], axis=1)
    qpos = past + jnp.arange(sq)
    kpos = full + jnp.arange(rem + sq)
    return moba_attend(q, ks, vs, valid, ko, vo, qpos, kpos)


def fox_prompt(q, k, v, logf):
    b, t, h, d = q.shape
    scale = d ** -0.5
    F = jnp.cumsum(logf, axis=1).transpose(0, 2, 1)
    nq = t // FOX_Q_BLOCK
    q_blocks = q.reshape(b, nq, FOX_Q_BLOCK, h, d).transpose(1, 0, 2, 3, 4)
    F_blocks = F.reshape(b, h, nq, FOX_Q_BLOCK).transpose(2, 0, 1, 3)
    starts = jnp.arange(nq) * FOX_Q_BLOCK
    kpos = jnp.arange(t)

    def block(args):
        qb, Fq, t0 = args
        s = jnp.einsum('bqhd,bkhd->bhqk', qb, k).astype(F32) * scale
        s = s + Fq[..., :, None] - F[:, :, None, :]
        qpos = t0 + jnp.arange(FOX_Q_BLOCK)
        s = jnp.where(kpos[None, :] <= qpos[:, None], s, -jnp.inf)
        p = jax.nn.softmax(s, axis=-1).astype(v.dtype)
        return jnp.einsum('bhqk,bkhd->bqhd', p, v)

    o = lax.map(block, (q_blocks, F_blocks, starts))
    return o.transpose(1, 0, 2, 3, 4).reshape(b, t, h, d)


def fox_decode(q, k_new, v_new, logf_new, pool_k, pool_v, pool_logf, li, page_table):
    b, sq, h, d = q.shape
    past = page_table.shape[1] * PAGE_SIZE
    scale = d ** -0.5
    k_past = pool_k[li, page_table].reshape(b, past, h, d)
    v_past = pool_v[li, page_table].reshape(b, past, h, d)
    F_past = jnp.cumsum(pool_logf[li, page_table].reshape(b, past, h).astype(F32), axis=1)
    F_new = F_past[:, -1:] + jnp.cumsum(logf_new, axis=1)
    Fq = F_new.transpose(0, 2, 1)[..., None]
    s_past = (jnp.einsum('bqhd,bkhd->bhqk', q, k_past).astype(F32) * scale
              + Fq - F_past.transpose(0, 2, 1)[:, :, None, :])
    s_new = (jnp.einsum('bqhd,bkhd->bhqk', q, k_new).astype(F32) * scale
             + Fq - F_new.transpose(0, 2, 1)[:, :, None, :])
    causal = jnp.arange(sq)[None, :] <= jnp.arange(sq)[:, None]
    s_new = jnp.where(causal, s_new, -jnp.inf)
    p = jax.nn.softmax(jnp.concatenate([s_past, s_new], axis=-1), axis=-1).astype(v_new.dtype)
    return (jnp.einsum('bhqk,bkhd->bqhd', p[..., :past], v_past)
            + jnp.einsum('bhqk,bkhd->bqhd', p[..., past:], v_new))


def causal_conv(u, prev, w, bias):
    up = jnp.concatenate([prev.astype(u.dtype), u], axis=1)
    y = lax.conv_general_dilated(up, w[:, None, :].astype(u.dtype), window_strides=(1,), padding='VALID',
                                 dimension_numbers=('NWC', 'WIO', 'NWC'), feature_group_count=u.shape[-1])
    return jax.nn.silu(y + bias), up[:, up.shape[1] - (CONV_W - 1):]


def ssd_scan(x, dt, A, Bm, Cm, D_skip, h0):
    b, L = x.shape[:2]
    Q = L if L <= SSD_CHUNK else SSD_CHUNK
    nc = L // Q
    xc = x.reshape(b, nc, Q, SSM_GROUPS, SSM_HPG, SSM_HEAD_DIM)
    dtc = dt.reshape(b, nc, Q, SSM_GROUPS, SSM_HPG)
    Bc = Bm.reshape(b, nc, Q, SSM_GROUPS, D_STATE)
    Cc = Cm.reshape(b, nc, Q, SSM_GROUPS, D_STATE)
    acum = jnp.cumsum(dtc * A, axis=2)
    causal = (jnp.arange(Q)[:, None] >= jnp.arange(Q)[None, :])[None, None, :, :, None, None]
    seg = acum[:, :, :, None] - acum[:, :, None, :]
    Lmat = jnp.exp(jnp.where(causal, seg, -jnp.inf))
    CB = jnp.einsum('bctgn,bcsgn->bctsg', Cc, Bc)
    M = CB[..., None] * Lmat * dtc[:, :, None]
    y = jnp.einsum('bctsgh,bcsghp->bctghp', M, xc)
    wdecay = jnp.exp(acum[:, :, -1:] - acum) * dtc
    S = jnp.einsum('bcsgn,bcsghp->bcghpn', Bc, wdecay[..., None] * xc)
    chunk_decay = jnp.exp(acum[:, :, -1])

    def step(hc, inp):
        dec, sc = inp
        return hc * dec[..., None, None] + sc, hc

    hT, h_in = lax.scan(step, h0, (jnp.moveaxis(chunk_decay, 1, 0), jnp.moveaxis(S, 1, 0)))
    h_in = jnp.moveaxis(h_in, 0, 1)
    y = (y + jnp.einsum('bctgn,bcghpn->bctghp', Cc, h_in) * jnp.exp(acum)[..., None]
         + D_skip[..., None] * xc)
    return y.reshape(b, L, SSM_GROUPS, SSM_HPG, SSM_HEAD_DIM), hT


def ssm_mixer(h, conv_prev, h0, w_in, conv_w, conv_b, dt_bias, a_log, d_skip, g_norm, w_out):
    b, t, _ = h.shape
    z, xbc, dt_raw = jnp.split(h @ w_in, [D_INNER, D_INNER + CONV_DIM], axis=-1)
    xbc, conv_state = causal_conv(xbc, conv_prev, conv_w, conv_b)
    xs, Bm, Cm = jnp.split(xbc.astype(F32), [D_INNER, D_INNER + SSM_GROUPS * D_STATE], axis=-1)
    dt = jax.nn.softplus(dt_raw.astype(F32) + dt_bias.astype(F32)).reshape(b, t, SSM_GROUPS, SSM_HPG)
    A = -jnp.exp(a_log.astype(F32)).reshape(SSM_GROUPS, SSM_HPG)
    y, hT = ssd_scan(xs.reshape(b, t, SSM_GROUPS, SSM_HPG, SSM_HEAD_DIM), dt, A,
                     Bm.reshape(b, t, SSM_GROUPS, D_STATE), Cm.reshape(b, t, SSM_GROUPS, D_STATE),
                     d_skip.astype(F32).reshape(SSM_GROUPS, SSM_HPG),
                     h0.astype(F32).reshape(b, SSM_GROUPS, SSM_HPG, SSM_HEAD_DIM, D_STATE))
    gs = D_INNER // SSM_GROUPS
    y = y.reshape(b, t, SSM_GROUPS, gs) * jax.nn.silu(z.astype(F32)).reshape(b, t, SSM_GROUPS, gs)
    y = y * lax.rsqrt(jnp.mean(y * y, axis=-1, keepdims=True) + EPS)
    y = (y.reshape(b, t, D_INNER) * g_norm.astype(F32)).astype(h.dtype)
    return y @ w_out, hT.reshape(b, SSM_HEADS, SSM_HEAD_DIM, D_STATE), conv_state


def swiglu(h, w_in, w_out):
    g, u = jnp.split(h @ w_in, 2, axis=-1)
    return (jax.nn.silu(g) * u) @ w_out


def setup_inputs(seed: int = 0) -> dict:
    key = jax.random.key(seed)
    ks = list(jax.random.split(key, 40))
    n_pages = PAST_LEN // PAGE_SIZE
    n_used = DEC_BATCH * n_pages
    n_phys = n_used + (n_used + 3) // 4
    na, ns = N_ATT_LAYERS, N_SSM_LAYERS

    def nrm(k, shape, scale=1.0):
        return jax.random.normal(k, shape, F32) * scale

    def gain(k, shape):
        return 1.0 + 0.02 * jax.random.normal(k, shape, F32)

    dt0 = jnp.exp(jax.random.uniform(ks[30], (ns, SSM_HEADS), F32, math.log(1e-3), math.log(1e-1)))
    dt_bias = dt0 + jnp.log(-jnp.expm1(-dt0))
    a_log = jnp.log(jax.random.uniform(ks[31], (ns, SSM_HEADS), F32, 1.0, 16.0))
    page_table = jax.random.permutation(ks[9], n_phys)[:n_used].reshape(DEC_BATCH, n_pages).astype(jnp.int32)
    return {
        'x_prompt': nrm(ks[0], (BATCH, SEQ, D_MODEL)),
        'x_sample': nrm(ks[1], (DEC_BATCH, DEC_SEQ, D_MODEL)),
        'cache_moba_k': nrm(ks[2], (na, n_phys, PAGE_SIZE, N_HEADS_MOBA, HEAD_DIM)),
        'cache_moba_v': nrm(ks[3], (na, n_phys, PAGE_SIZE, N_HEADS_MOBA, HEAD_DIM)),
        'cache_fox_k': nrm(ks[4], (na, n_phys, PAGE_SIZE, N_HEADS_FOX, HEAD_DIM)),
        'cache_fox_v': nrm(ks[5], (na, n_phys, PAGE_SIZE, N_HEADS_FOX, HEAD_DIM)),
        'cache_fox_logf': jax.nn.log_sigmoid(FOX_F_BIAS + nrm(ks[6], (na, n_phys, PAGE_SIZE, N_HEADS_FOX))),
        'state_ssm': nrm(ks[7], (ns, DEC_BATCH, SSM_HEADS, SSM_HEAD_DIM, D_STATE), 0.5),
        'state_conv': nrm(ks[8], (ns, DEC_BATCH, CONV_W - 1, CONV_DIM)),
        'page_table': page_table,
        'c_prompt': nrm(ks[10], (BATCH, D_MODEL)),
        'c_sample': nrm(ks[11], (DEC_BATCH, D_MODEL)),
        'w_ada': nrm(ks[12], (DEPTH, D_MODEL, 6 * D_MODEL), 0.5 * D_MODEL ** -0.5),
        'b_ada': nrm(ks[13], (DEPTH, 6 * D_MODEL), 0.02),
        'g_mix_pre': gain(ks[14], (DEPTH, D_MODEL)),
        'g_mix_post': gain(ks[15], (DEPTH, D_MODEL)),
        'g_ffn_pre': gain(ks[16], (DEPTH, D_MODEL)),
        'g_ffn_post': gain(ks[17], (DEPTH, D_MODEL)),
        'w_att_in': nrm(ks[18], (na, D_MODEL, ATT_IN), D_MODEL ** -0.5),
        'b_fox_f': FOX_F_BIAS + nrm(ks[19], (na, N_HEADS_FOX), 0.1),
        'w_att_out': nrm(ks[20], (na, ATT_W_A + ATT_W_B, D_MODEL), (ATT_W_A + ATT_W_B) ** -0.5),
        'w_ssm_in': nrm(ks[21], (ns, D_MODEL, SSM_IN), D_MODEL ** -0.5),
        'conv_w': nrm(ks[22], (ns, CONV_W, CONV_DIM), CONV_W ** -0.5),
        'conv_b': nrm(ks[23], (ns, CONV_DIM), 0.02),
        'dt_bias': dt_bias,
        'a_log': a_log,
        'd_skip': 1.0 + nrm(ks[24], (ns, SSM_HEADS), 0.1),
        'g_ssm_norm': gain(ks[25], (ns, D_INNER)),
        'w_ssm_out': nrm(ks[26], (ns, D_INNER, D_MODEL), D_INNER ** -0.5),
        'w_ffn_in': nrm(ks[27], (DEPTH, D_MODEL, 2 * D_FF), D_MODEL ** -0.5),
        'w_ffn_out': nrm(ks[28], (DEPTH, D_FF, D_MODEL), D_FF ** -0.5),
    }


def reference(x_prompt, x_sample, cache_moba_k, cache_moba_v, cache_fox_k, cache_fox_v, cache_fox_logf,
              state_ssm, state_conv, page_table, c_prompt, c_sample, w_ada, b_ada, g_mix_pre, g_mix_post,
              g_ffn_pre, g_ffn_post, w_att_in, b_fox_f, w_att_out, w_ssm_in, conv_w, conv_b, dt_bias, a_log,
              d_skip, g_ssm_norm, w_ssm_out, w_ffn_in, w_ffn_out):

    def run(x, c, pos, decode):
        b, t, _ = x.shape
        rows_ka, rows_va, rows_kb, rows_vb, rows_lf, ssm_out, conv_out = [], [], [], [], [], [], []
        for layer in range(DEPTH):
            sh1, sc1, gt1, sh2, sc2, gt2 = ada_terms(c, w_ada[layer], b_ada[layer])
            h = rms_norm(x, g_mix_pre[layer]) * (1.0 + sc1) + sh1
            li = layer // 2
            if layer % 2 == 0:
                qa, ka, va, qb, kb, vb, lf = attn_project(h, w_att_in[li], b_fox_f[li], pos)
                if decode:
                    oa = moba_decode(qa, ka, va, cache_moba_k, cache_moba_v, li, page_table)
                    ob = fox_decode(qb, kb, vb, lf, cache_fox_k, cache_fox_v, cache_fox_logf, li, page_table)
                else:
                    oa = moba_prompt(qa, ka, va)
                    ob = fox_prompt(qb, kb, vb, lf)
                mix = jnp.concatenate([oa.reshape(b, t, ATT_W_A), ob.reshape(b, t, ATT_W_B)], axis=-1) @ w_att_out[li]
                rows_ka.append(ka)
                rows_va.append(va)
                rows_kb.append(kb)
                rows_vb.append(vb)
                rows_lf.append(lf)
            else:
                if decode:
                    conv_prev, h0 = state_conv[li], state_ssm[li]
                else:
                    conv_prev = jnp.zeros((b, CONV_W - 1, CONV_DIM), x.dtype)
                    h0 = jnp.zeros((b, SSM_HEADS, SSM_HEAD_DIM, D_STATE), F32)
                mix, hT, cst = ssm_mixer(h, conv_prev, h0, w_ssm_in[li], conv_w[li], conv_b[li], dt_bias[li],
                                         a_log[li], d_skip[li], g_ssm_norm[li], w_ssm_out[li])
                ssm_out.append(hT)
                conv_out.append(cst)
            x = x + gt1 * rms_norm(mix, g_mix_post[layer])
            h = rms_norm(x, g_ffn_pre[layer]) * (1.0 + sc2) + sh2
            x = x + gt2 * rms_norm(swiglu(h, w_ffn_in[layer], w_ffn_out[layer]), g_ffn_post[layer])
        return (x, jnp.stack(rows_ka), jnp.stack(rows_va), jnp.stack(rows_kb), jnp.stack(rows_vb),
                jnp.stack(rows_lf), jnp.stack(ssm_out), jnp.stack(conv_out))

    pos_prompt = jnp.arange(x_prompt.shape[1])
    past = page_table.shape[1] * PAGE_SIZE
    pos_sample = past + jnp.arange(x_sample.shape[1])
    (y_prompt, moba_k_prompt, moba_v_prompt, fox_k_prompt, fox_v_prompt, fox_logf_prompt,
     ssm_state_prompt, conv_state_prompt) = run(x_prompt, c_prompt, pos_prompt, False)
    (y_sample, moba_k_sample, moba_v_sample, fox_k_sample, fox_v_sample, fox_logf_sample,
     ssm_state_sample, conv_state_sample) = run(x_sample, c_sample, pos_sample, True)
    return (y_prompt, y_sample, moba_k_prompt, moba_v_prompt, fox_k_prompt, fox_v_prompt, fox_logf_prompt,
            ssm_state_prompt, conv_state_prompt, moba_k_sample, moba_v_sample, fox_k_sample, fox_v_sample,
            fox_logf_sample, ssm_state_sample, conv_state_sample)
```

```python
import functools
import math

import numpy as np
import jax
import jax.numpy as jnp
from jax import lax
from jax.experimental import pallas as pl
from jax.experimental.pallas import tpu as pltpu

F32 = jnp.float32
BF16 = jnp.bfloat16
HIGHEST = lax.Precision.HIGHEST

D_MODEL = 1024
HEAD_DIM = 64
N_HEADS = 8
ATT_W = N_HEADS * HEAD_DIM
ROT_DIM = HEAD_DIM // 4
ROPE_THETA = 500000.0
MOBA_BLOCK = 256
MOBA_TOPK = 3
PAGE_SIZE = 128
D_INNER = 2 * D_MODEL
SSM_HEAD_DIM = 64
SSM_HEADS = D_INNER // SSM_HEAD_DIM
SSM_GROUPS = 4
SSM_HPG = SSM_HEADS // SSM_GROUPS
D_STATE = 128
CONV_W = 4
CONV_DIM = D_INNER + 2 * SSM_GROUPS * D_STATE
SSD_CHUNK = 128
D_FF = ((8 * D_MODEL + 3 * 256 - 1) // (3 * 256)) * 256
FFN_CHUNK = D_FF // 2
EPS = 1e-6
LANES = 128
VMEM_LIMIT = 56 * 1024 * 1024
PAGES_PER_STEP = 8

_NT = (((1,), (1,)), ((), ()))
_TN = (((0,), (0,)), ((), ()))


def _cparams(*sem):
    return pltpu.CompilerParams(dimension_semantics=sem, vmem_limit_bytes=VMEM_LIMIT)


def _const_spec(shape):
    nd = len(shape)
    return pl.BlockSpec(shape, lambda *_: (0,) * nd, pipeline_mode=pl.Buffered(1))


def _rms(x, g):
    return x * lax.rsqrt(jnp.mean(x * x, axis=-1, keepdims=True) + EPS) * g


def _silu(x):
    return x * jax.nn.sigmoid(x)


def _softplus(x):
    return jnp.maximum(x, 0.0) + jnp.log1p(jnp.exp(-jnp.abs(x)))


def _dot(a, b, **kw):
    return jnp.dot(a, b, preferred_element_type=F32, **kw)


def _dot_nt(a, b, **kw):
    return lax.dot_general(a, b, _NT, preferred_element_type=F32, **kw)


def _ada_kernel(c_ref, w_ref, b_ref, o_ref):
    a = _silu(c_ref[...]).astype(BF16)
    o_ref[0] = _dot(a, w_ref[0].astype(BF16)) + b_ref[0]


def _ada(c_all, w_ada, b_ada):
    depth, _, n = w_ada.shape
    rows = c_all.shape[0]
    tn = 1024
    return pl.pallas_call(
        _ada_kernel,
        grid=(depth, n // tn),
        in_specs=[pl.BlockSpec((rows, D_MODEL), lambda l, j: (0, 0)),
                  pl.BlockSpec((1, D_MODEL, tn), lambda l, j: (l, 0, j)),
                  pl.BlockSpec((1, 1, tn), lambda l, j: (l, 0, j))],
        out_specs=pl.BlockSpec((1, rows, tn), lambda l, j: (l, 0, j)),
        out_shape=jax.ShapeDtypeStruct((depth, rows, n), F32),
        compiler_params=_cparams("arbitrary", "arbitrary"),
        name="ada_terms",
    )(c_all, w_ada, b_ada.reshape(depth, 1, n))


def _attn_proj_kernel(x_ref, sc_ref, sh_ref, g_ref, w_ref, wf_ref, bf_ref, cos_ref, s1_ref, s2_ref,
                      qa_ref, ka_ref, va_ref, qb_ref, kb_ref, vb_ref, lf_ref):
    h = (_rms(x_ref[...], g_ref[...]) * (1.0 + sc_ref[0]) + sh_ref[0]).astype(BF16)
    outs = (qa_ref, ka_ref, va_ref, qb_ref, kb_ref, vb_ref)
    for n in range(6):
        z = _dot(h, w_ref[:, n * ATT_W:(n + 1) * ATT_W])
        if n < 2:
            z = (z * cos_ref[0] + pltpu.roll(z, ATT_W - ROT_DIM // 2, 1) * s1_ref[0]
                 + pltpu.roll(z, ROT_DIM // 2, 1) * s2_ref[0])
        outs[n][...] = z
    zf = _dot(h, wf_ref[...]) + bf_ref[...]
    lf_ref[...] = jnp.minimum(zf, 0.0) - jnp.log1p(jnp.exp(-jnp.abs(zf)))


def _rope_tables(pos):
    half = ROT_DIM // 2
    inv = ROPE_THETA ** (-2.0 * np.arange(half, dtype=np.float64) / ROT_DIM)
    ang = np.asarray(pos, np.float64)[:, None] * inv[None, :]
    cos, sin = np.cos(ang), np.sin(ang)
    n = len(pos)
    c64 = np.concatenate([cos, cos, np.ones((n, HEAD_DIM - ROT_DIM))], axis=1)
    s1 = np.concatenate([-sin, np.zeros((n, HEAD_DIM - half))], axis=1)
    s2 = np.concatenate([np.zeros((n, half)), sin, np.zeros((n, HEAD_DIM - ROT_DIM))], axis=1)
    return [jnp.asarray(np.tile(t, (1, N_HEADS)), F32) for t in (c64, s1, s2)]


def _attn_proj(x2d, sc, sh, g, w6, wf, bf, tables, tm, mod_map, tab_map):
    m = x2d.shape[0]
    row = lambda i: (i, 0)
    mod_spec = pl.BlockSpec((1,) + sc.shape[1:], mod_map)
    tab_spec = pl.BlockSpec((1,) + tables[0].shape[1:], tab_map)
    wide = jax.ShapeDtypeStruct((m, ATT_W), F32)
    return pl.pallas_call(
        _attn_proj_kernel,
        grid=(m // tm,),
        in_specs=[pl.BlockSpec((tm, D_MODEL), row), mod_spec, mod_spec, _const_spec((1, D_MODEL)),
                  _const_spec(w6.shape), _const_spec(wf.shape), _const_spec((1, LANES)),
                  tab_spec, tab_spec, tab_spec],
        out_specs=[pl.BlockSpec((tm, ATT_W), row)] * 6 + [pl.BlockSpec((tm, LANES), row)],
        out_shape=[wide] * 6 + [jax.ShapeDtypeStruct((m, LANES), F32)],
        compiler_params=_cparams("arbitrary"),
        name="attn_in_proj",
    )(x2d, sc, sh, g, w6, wf, bf, *tables)


def _cumsum_kernel(lf_ref, f_ref, ft_ref):
    t = lf_ref.shape[0]
    r = lax.broadcasted_iota(jnp.int32, (LANES, LANES), 0)
    c = lax.broadcasted_iota(jnp.int32, (LANES, LANES), 1)
    tri = (r >= c).astype(F32)
    carry = jnp.zeros((1, LANES), F32)
    for i in range(t // LANES):
        blk = _dot(tri, lf_ref[i * LANES:(i + 1) * LANES, :], precision=HIGHEST) + carry
        carry = blk[LANES - 1:LANES, :]
        f_ref[i * LANES:(i + 1) * LANES, :] = blk
        ft_ref[0, :, i * LANES:(i + 1) * LANES] = blk.T[0:N_HEADS, :]


def _fox_cumsum(lf, b, t):
    return pl.pallas_call(
        _cumsum_kernel,
        grid=(b,),
        in_specs=[pl.BlockSpec((t, LANES), lambda i: (i, 0))],
        out_specs=[pl.BlockSpec((t, LANES), lambda i: (i, 0)),
                   pl.BlockSpec((1, N_HEADS, t), lambda i: (i, 0, 0))],
        out_shape=[jax.ShapeDtypeStruct((b * t, LANES), F32), jax.ShapeDtypeStruct((b, N_HEADS, t), F32)],
        compiler_params=_cparams("arbitrary"),
        name="fox_cumsum",
    )(lf)


def _attn_prompt_kernel(*refs, fox, t):
    if fox:
        q_ref, k_ref, v_ref, f_ref, ft_ref, o_ref, kbf, vbf = refs
    else:
        q_ref, k_ref, v_ref, o_ref, kbf, vbf, kmean = refs
    blk = MOBA_BLOCK
    nb = t // blk
    pair = pl.program_id(1)
    kbf[...] = k_ref[...].astype(BF16)
    vbf[...] = v_ref[...].astype(BF16)
    lane = lax.broadcasted_iota(jnp.int32, (blk, LANES), 1)
    half_masks = (lane < HEAD_DIM, lane >= HEAD_DIM)
    rowi = lax.broadcasted_iota(jnp.int32, (blk, blk), 0)
    coli = lax.broadcasted_iota(jnp.int32, (blk, blk), 1)
    causal = coli <= rowi
    lane_nb = lax.broadcasted_iota(jnp.int32, (blk, nb), 1)
    if not fox:
        for n in range(nb):
            kmean[n:n + 1, :] = jnp.mean(k_ref[n * blk:(n + 1) * blk, :], axis=0, keepdims=True)

    def q_block(qi, carry):
        q0 = pl.multiple_of(qi * blk, blk)
        q = q_ref[pl.ds(q0, blk), :]
        head_out = []
        for hd in range(2):
            qm = jnp.where(half_masks[hd], q, 0.0)
            qs = (qm * (HEAD_DIM ** -0.5)).astype(BF16)
            if fox:
                hidx = 2 * pair + hd
                fcol = jnp.sum(jnp.where(lane == hidx, f_ref[pl.ds(q0, blk), :], 0.0), axis=1, keepdims=True)
            else:
                gate = _dot_nt(qm, kmean[...], precision=HIGHEST)
                beaten = jnp.zeros((blk, nb), F32)
                for n2 in range(nb):
                    g2 = gate[:, n2:n2 + 1]
                    better = (g2 > gate) | ((g2 == gate) & (n2 < lane_nb))
                    beaten = beaten + jnp.where(better, 1.0, 0.0) * (n2 < qi).astype(F32)
                sel = jnp.where((beaten < MOBA_TOPK) & (lane_nb < qi), 1.0, 0.0)

            def scores(kb):
                k0 = pl.multiple_of(kb * blk, blk)
                s = _dot_nt(qs, kbf[pl.ds(k0, blk), :])
                if fox:
                    s = s + (fcol - ft_ref[0, pl.ds(hidx, 1), pl.ds(k0, blk)])
                return s, vbf[pl.ds(k0, blk), :]

            s, vv = scores(qi)
            s = jnp.where(causal, s, -jnp.inf)
            m0 = jnp.max(s, axis=1, keepdims=True)
            p = jnp.exp(s - m0)
            l0 = jnp.sum(p, axis=1, keepdims=True)
            acc0 = _dot(p.astype(BF16), vv)

            def kv_block(kb, c):
                m, l, acc = c
                s, vv = scores(kb)
                if not fox:
                    picked = jnp.sum(jnp.where(lane_nb == kb, sel, 0.0), axis=1, keepdims=True)
                    s = jnp.where(picked > 0.0, s, -jnp.inf)
                m_new = jnp.maximum(m, jnp.max(s, axis=1, keepdims=True))
                alpha = jnp.exp(m - m_new)
                p = jnp.exp(s - m_new)
                l = alpha * l + jnp.sum(p, axis=1, keepdims=True)
                acc = alpha * acc + _dot(p.astype(BF16), vv)
                return m_new, l, acc

            _, l, acc = lax.fori_loop(0, qi, kv_block, (m0, l0, acc0))
            head_out.append(acc / l)
        o_ref[pl.ds(q0, blk), :] = jnp.where(half_masks[0], head_out[0], head_out[1])
        return carry

    lax.fori_loop(0, nb, q_block, 0)


def _attn_prompt(q, k, v, b, t, fox, f=None, ft=None):
    pairs = ATT_W // LANES
    slab = pl.BlockSpec((t, LANES), lambda i, p: (i, p))
    in_specs = [slab, slab, slab]
    args = [q, k, v]
    scratch = [pltpu.VMEM((t, LANES), BF16), pltpu.VMEM((t, LANES), BF16)]
    if fox:
        in_specs += [pl.BlockSpec((t, LANES), lambda i, p: (i, 0)),
                     pl.BlockSpec((1, N_HEADS, t), lambda i, p: (i, 0, 0))]
        args += [f, ft]
    else:
        scratch.append(pltpu.VMEM((t // MOBA_BLOCK, LANES), F32))
    return pl.pallas_call(
        functools.partial(_attn_prompt_kernel, fox=fox, t=t),
        grid=(b, pairs),
        in_specs=in_specs,
        out_specs=slab,
        out_shape=jax.ShapeDtypeStruct((b * t, ATT_W), F32),
        scratch_shapes=scratch,
        compiler_params=_cparams("arbitrary", "arbitrary"),
        name="fox_prompt_attn" if fox else "moba_prompt_attn",
    )(*args)


def _post_kernel(*refs, ssm):
    if ssm:
        x_ref, y_ref, z_ref, gn_ref, wmix_ref = refs[:5]
        rest = refs[5:]
    else:
        x_ref, oa_ref, ob_ref, wmix_ref = refs[:4]
        rest = refs[4:]
    gt1_ref, gpost_ref, gpre_ref, sc2_ref, sh2_ref, gt2_ref, gfpost_ref, win_ref, wout_ref, o_ref = rest
    if ssm:
        y = y_ref[...] * _silu(z_ref[...])
        gs = D_INNER // SSM_GROUPS
        parts = []
        for g in range(SSM_GROUPS):
            yg = y[:, g * gs:(g + 1) * gs]
            parts.append(yg * lax.rsqrt(jnp.mean(yg * yg, axis=-1, keepdims=True) + EPS))
        yn = (jnp.concatenate(parts, axis=-1) * gn_ref[...]).astype(BF16)
        mix = _dot(yn, wmix_ref[...])
    else:
        mix = (_dot(oa_ref[...].astype(BF16), wmix_ref[0:ATT_W, :])
               + _dot(ob_ref[...].astype(BF16), wmix_ref[ATT_W:2 * ATT_W, :]))
    x1 = x_ref[...] + gt1_ref[0] * _rms(mix, gpost_ref[...])
    h2 = (_rms(x1, gpre_ref[...]) * (1.0 + sc2_ref[0]) + sh2_ref[0]).astype(BF16)
    acc = None
    for c in range(D_FF // FFN_CHUNK):
        lo = c * FFN_CHUNK
        gate = _dot(h2, win_ref[:, lo:lo + FFN_CHUNK])
        up = _dot(h2, win_ref[:, D_FF + lo:D_FF + lo + FFN_CHUNK])
        part = _dot((_silu(gate) * up).astype(BF16), wout_ref[lo:lo + FFN_CHUNK, :])
        acc = part if acc is None else acc + part
    o_ref[...] = x1 + gt2_ref[0] * _rms(acc, gfpost_ref[...])


def _post(x2d, mix_in, wmix, mods, gains, win, wout, tm, mod_map, ssm, gn=None):
    m = x2d.shape[0]
    row = lambda i: (i, 0)
    gt1, sc2, sh2, gt2 = mods
    gpost, gpre, gfpost = gains
    mod_spec = pl.BlockSpec((1,) + gt1.shape[1:], mod_map)
    vec = _const_spec((1, D_MODEL))
    in_specs = [pl.BlockSpec((tm, D_MODEL), row)]
    args = [x2d]
    for a in mix_in:
        in_specs.append(pl.BlockSpec((tm, a.shape[1]), row))
        args.append(a)
    if ssm:
        in_specs.append(_const_spec((1, D_INNER)))
        args.append(gn)
    in_specs += [_const_spec(wmix.shape), mod_spec, vec, vec, mod_spec, mod_spec, mod_spec, vec,
                 _const_spec(win.shape), _const_spec(wout.shape)]
    args += [wmix, gt1, gpost, gpre, sc2, sh2, gt2, gfpost, win, wout]
    return pl.pallas_call(
        functools.partial(_post_kernel, ssm=ssm),
        grid=(m // tm,),
        in_specs=in_specs,
        out_specs=pl.BlockSpec((tm, D_MODEL), row),
        out_shape=jax.ShapeDtypeStruct((m, D_MODEL), F32),
        compiler_params=_cparams("arbitrary"),
        name="ssm_out_ffn" if ssm else "attn_out_ffn",
    )(*args)


def _ssm_proj_kernel(*refs, decode, tiles_per_seq):
    if decode:
        (x_ref, sc_ref, sh_ref, g_ref, wz_ref, wx_ref, wdt_ref, dtb_ref, cw_ref, cb_ref, prev_ref,
         z_ref, xc_ref, dt_ref, cs_ref) = refs
    else:
        (x_ref, sc_ref, sh_ref, g_ref, wz_ref, wx_ref, wdt_ref, dtb_ref, cw_ref, cb_ref,
         z_ref, xc_ref, dt_ref, cs_ref, ubuf) = refs
    tm = x_ref.shape[0]
    h = (_rms(x_ref[...], g_ref[...]) * (1.0 + sc_ref[0]) + sh_ref[0]).astype(BF16)
    z_ref[...] = _dot(h, wz_ref[...])
    dt_ref[...] = _softplus(_dot(h, wdt_ref[...]) + dtb_ref[...])
    u = _dot(h, wx_ref[...])
    w = [cw_ref[j:j + 1, :] for j in range(CONV_W)]
    if decode:
        p0, p1, p2 = prev_ref[:, 0, :], prev_ref[:, 1, :], prev_ref[:, 2, :]
        y = cb_ref[...] + w[3] * u + w[2] * p2 + w[1] * p1 + w[0] * p0
        cs_ref[:, 0, :] = p1
        cs_ref[:, 1, :] = p2
        cs_ref[:, 2, :] = u
    else:
        pad = 8
        @pl.when(pl.program_id(0) % tiles_per_seq == 0)
        def _():
            ubuf[0:pad, :] = jnp.zeros((pad, CONV_DIM), F32)
        ubuf[pad:pad + tm, :] = u
        y = cb_ref[...] + w[3] * u
        for j in range(CONV_W - 1):
            y = y + w[j] * ubuf[pad - (CONV_W - 1) + j:pad - (CONV_W - 1) + j + tm, :]
        cs_ref[0] = ubuf[pad + tm - (CONV_W - 1):pad + tm, :]
        ubuf[0:pad, :] = ubuf[tm:tm + pad, :]
    xc_ref[...] = _silu(y)


def _ssm_proj(x2d, sc, sh, g, wz, wx, wdt, dtb, cw, cb, tm, mod_map, seq_len, prev=None):
    m = x2d.shape[0]
    decode = prev is not None
    row = lambda i: (i, 0)
    mod_spec = pl.BlockSpec((1,) + sc.shape[1:], mod_map)
    in_specs = [pl.BlockSpec((tm, D_MODEL), row), mod_spec, mod_spec, _const_spec((1, D_MODEL)),
                _const_spec(wz.shape), _const_spec(wx.shape), _const_spec(wdt.shape), _const_spec((1, LANES)),
                _const_spec(cw.shape), _const_spec((1, CONV_DIM))]
    args = [x2d, sc, sh, g, wz, wx, wdt, dtb, cw, cb]
    scratch = []
    if decode:
        in_specs.append(_const_spec(prev.shape))
        args.append(prev)
        cs_shape = prev.shape
        cs_spec = _const_spec(prev.shape)
        tiles_per_seq = 1
    else:
        tiles_per_seq = seq_len // tm
        nseq = m // seq_len
        cs_shape = (nseq, CONV_W - 1, CONV_DIM)
        cs_spec = pl.BlockSpec((1, CONV_W - 1, CONV_DIM), lambda i: (i // tiles_per_seq, 0, 0))
        scratch = [pltpu.VMEM((tm + 8, CONV_DIM), F32)]
    return pl.pallas_call(
        functools.partial(_ssm_proj_kernel, decode=decode, tiles_per_seq=tiles_per_seq),
        grid=(m // tm,),
        in_specs=in_specs,
        out_specs=[pl.BlockSpec((tm, D_INNER), row), pl.BlockSpec((tm, CONV_DIM), row),
                   pl.BlockSpec((tm, LANES), row), cs_spec],
        out_shape=[jax.ShapeDtypeStruct((m, D_INNER), F32), jax.ShapeDtypeStruct((m, CONV_DIM), F32),
                   jax.ShapeDtypeStruct((m, LANES), F32), jax.ShapeDtypeStruct(cs_shape, F32)],
        scratch_shapes=scratch,
        compiler_params=_cparams("arbitrary"),
        name="ssm_in_proj_decode" if decode else "ssm_in_proj",
    )(*args)


def _head_expand():
    e = np.zeros((LANES, D_INNER), np.float32)
    for hh in range(SSM_HEADS):
        e[hh, hh * SSM_HEAD_DIM:(hh + 1) * SSM_HEAD_DIM] = 1.0
    return e


def _ssd_kernel(xs_ref, b_ref, c_ref, dt_ref, alog_ref, dsk_ref, e_ref, et_ref, y_ref, ht_ref, hst):
    q = SSD_CHUNK
    gw = D_INNER // SSM_GROUPS
    ci = pl.program_id(1)

    @pl.when(ci == 0)
    def _():
        hst[...] = jnp.zeros_like(hst)

    r = lax.broadcasted_iota(jnp.int32, (q, q), 0)
    c = lax.broadcasted_iota(jnp.int32, (q, q), 1)
    causal = r >= c
    lane = lax.broadcasted_iota(jnp.int32, (q, LANES), 1)
    first_half = lane < SSM_HEAD_DIM
    dt = dt_ref[...]
    a = -jnp.exp(alog_ref[...])
    acum = _dot(causal.astype(F32), dt * a, precision=HIGHEST)
    acum_t = acum.T
    dt_t = dt.T
    a_last = acum[q - 1:q, :]
    expand = e_ref[...]
    w_full = _dot(jnp.exp(a_last - acum) * dt, expand, precision=HIGHEST)
    ea_full = _dot(jnp.exp(acum), expand, precision=HIGHEST)
    dec_rows = _dot(et_ref[...], jnp.exp(jnp.broadcast_to(acum_t[:, q - 1:q], (LANES, LANES))),
                    precision=HIGHEST)
    xs = xs_ref[...]
    xs_bf = xs.astype(BF16)
    xw = xs * w_full
    dsk = dsk_ref[...]
    for g in range(SSM_GROUPS):
        bg = b_ref[:, g * D_STATE:(g + 1) * D_STATE].astype(BF16)
        cg = c_ref[:, g * D_STATE:(g + 1) * D_STATE].astype(BF16)
        cb = _dot_nt(cg, bg)
        rows = slice(g * gw, (g + 1) * gw)
        h_in = hst[rows, :]
        y_inter = _dot_nt(cg, h_in.astype(BF16))
        parts = []
        for pr in range(SSM_HPG // 2):
            h0 = g * SSM_HPG + 2 * pr
            x_pair = xs_bf[:, h0 * SSM_HEAD_DIM:(h0 + 2) * SSM_HEAD_DIM]
            outs = []
            for hh in (h0, h0 + 1):
                seg = acum[:, hh:hh + 1] - acum_t[hh:hh + 1, :]
                mh = cb * jnp.exp(jnp.where(causal, seg, -jnp.inf)) * dt_t[hh:hh + 1, :]
                outs.append(_dot(mh.astype(BF16), x_pair))
            parts.append(jnp.where(first_half, outs[0], outs[1]))
        y_intra = jnp.concatenate(parts, axis=-1)
        y_ref[:, rows] = y_intra + y_inter * ea_full[:, rows] + dsk[:, rows] * xs[:, rows]
        s_inc = _dot(xw[:, rows].T.astype(BF16), bg)
        hst[rows, :] = h_in * dec_rows[rows, :] + s_inc

    @pl.when(ci == pl.num_programs(1) - 1)
    def _():
        ht_ref[0] = hst[...]


def _ssd_prompt(xc, dt, alog, dsk, b, t):
    nc = t // SSD_CHUNK
    e = jnp.asarray(_head_expand())
    et = jnp.asarray(_head_expand().T.copy())
    rowmap = lambda i, c: (i * nc + c, 0)
    bc_w = SSM_GROUPS * D_STATE
    return pl.pallas_call(
        _ssd_kernel,
        grid=(b, nc),
        in_specs=[pl.BlockSpec((SSD_CHUNK, D_INNER), rowmap),
                  pl.BlockSpec((SSD_CHUNK, bc_w), lambda i, c: (i * nc + c, D_INNER // bc_w)),
                  pl.BlockSpec((SSD_CHUNK, bc_w), lambda i, c: (i * nc + c, D_INNER // bc_w + 1)),
                  pl.BlockSpec((SSD_CHUNK, LANES), rowmap),
                  _const_spec((1, LANES)), _const_spec((1, D_INNER)),
                  _const_spec((LANES, D_INNER)), _const_spec((D_INNER, LANES))],
        out_specs=[pl.BlockSpec((SSD_CHUNK, D_INNER), rowmap),
                   pl.BlockSpec((1, D_INNER, D_STATE), lambda i, c: (i, 0, 0))],
        out_shape=[jax.ShapeDtypeStruct((b * t, D_INNER), F32),
                   jax.ShapeDtypeStruct((b, D_INNER, D_STATE), F32)],
        scratch_shapes=[pltpu.VMEM((D_INNER, D_STATE), F32)],
        compiler_params=_cparams("arbitrary", "arbitrary"),
        name="ssd_scan",
    )(xc, xc, xc, dt, alog, dsk, e, et)


def _ssd_step_kernel(xs_ref, b_ref, c_ref, dt_ref, alog_ref, dsk_ref, e_ref, et_ref, h0_ref, y_ref, ht_ref):
    gw = D_INNER // SSM_GROUPS
    dt = dt_ref[0]
    a = -jnp.exp(alog_ref[...])
    dec = jnp.exp(dt * a)
    expand = e_ref[...]
    eye = (lax.broadcasted_iota(jnp.int32, (LANES, LANES), 0)
           == lax.broadcasted_iota(jnp.int32, (LANES, LANES), 1))
    dec_diag = jnp.where(eye, jnp.broadcast_to(dec, (LANES, LANES)), 0.0)
    dec_rows = _dot(_dot(et_ref[...], dec_diag, precision=HIGHEST), jnp.ones((LANES, LANES), F32),
                    precision=HIGHEST)
    xs = xs_ref[0]
    dtx = xs * _dot(dt, expand, precision=HIGHEST)
    eye_g = (lax.broadcasted_iota(jnp.int32, (gw, gw), 0) == lax.broadcasted_iota(jnp.int32, (gw, gw), 1))
    bm = b_ref[0]
    cm = c_ref[0]
    ys = []
    for g in range(SSM_GROUPS):
        rows = slice(g * gw, (g + 1) * gw)
        bg = bm[:, g * D_STATE:(g + 1) * D_STATE]
        cg = cm[:, g * D_STATE:(g + 1) * D_STATE]
        diag = jnp.where(eye_g, jnp.broadcast_to(dtx[:, rows], (gw, gw)), 0.0).astype(BF16)
        outer = _dot(diag, jnp.broadcast_to(bg, (gw, D_STATE)).astype(BF16))
        h_new = h0_ref[0, rows, :] * dec_rows[rows, :] + outer
        ht_ref[0, rows, :] = h_new
        yg = _dot_nt(jnp.broadcast_to(cg, (8, D_STATE)).astype(BF16), h_new.astype(BF16))
        ys.append(yg[0:1, :])
    y_ref[0] = jnp.concatenate(ys, axis=-1) + dsk_ref[...] * xs


def _ssd_step(xc, dt, alog, dsk, h0):
    b = xc.shape[0]
    e = jnp.asarray(_head_expand())
    et = jnp.asarray(_head_expand().T.copy())
    bc_w = SSM_GROUPS * D_STATE
    xc3 = xc.reshape(b, 1, CONV_DIM)
    return pl.pallas_call(
        _ssd_step_kernel,
        grid=(b,),
        in_specs=[pl.BlockSpec((1, 1, D_INNER), lambda i: (i, 0, 0)),
                  pl.BlockSpec((1, 1, bc_w), lambda i: (i, 0, D_INNER // bc_w)),
                  pl.BlockSpec((1, 1, bc_w), lambda i: (i, 0, D_INNER // bc_w + 1)),
                  pl.BlockSpec((1, 1, LANES), lambda i: (i, 0, 0)),
                  _const_spec((1, LANES)), _const_spec((1, D_INNER)),
                  _const_spec((LANES, D_INNER)), _const_spec((D_INNER, LANES)),
                  pl.BlockSpec((1, D_INNER, D_STATE), lambda i: (i, 0, 0))],
        out_specs=[pl.BlockSpec((1, 1, D_INNER), lambda i: (i, 0, 0)),
                   pl.BlockSpec((1, D_INNER, D_STATE), lambda i: (i, 0, 0))],
        out_shape=[jax.ShapeDtypeStruct((b, 1, D_INNER), F32),
                   jax.ShapeDtypeStruct((b, D_INNER, D_STATE), F32)],
        compiler_params=_cparams("arbitrary"),
        name="ssd_step",
    )(xc3, xc3, xc3, dt.reshape(b, 1, LANES), alog, dsk, e, et, h0)


def _head_scores(q8, k_ref):
    rows = lax.broadcasted_iota(jnp.int32, (LANES, HEAD_DIM), 0)
    q_rows = jnp.concatenate([q8, jnp.zeros((LANES - N_HEADS, HEAD_DIM), F32)], axis=0)
    s = None
    for h in range(N_HEADS):
        qsel = jnp.where(rows == h, q_rows, 0.0).astype(BF16)
        part = _dot_nt(k_ref[:, h, :].astype(BF16), qsel)
        s = part if s is None else s + part
    return s


def _head_values(p, v_ref, acc):
    p_t = p.T[0:N_HEADS, :].astype(BF16)
    rows = lax.broadcasted_iota(jnp.int32, (N_HEADS, HEAD_DIM), 0)
    for h in range(N_HEADS):
        o = _dot(p_t, v_ref[:, h, :].astype(BF16))
        acc = acc + jnp.where(rows == h, o, 0.0)
    return acc


def _row_to_col(row8):
    eye = (lax.broadcasted_iota(jnp.int32, (N_HEADS, LANES), 0)
           == lax.broadcasted_iota(jnp.int32, (N_HEADS, LANES), 1))
    return jnp.sum(jnp.where(eye, jnp.broadcast_to(row8, (N_HEADS, LANES)), 0.0), axis=1, keepdims=True)


def _self_scores(q8, k8):
    k_rows = jnp.concatenate([k8, jnp.zeros((LANES - N_HEADS, HEAD_DIM), F32)], axis=0)
    full = _dot_nt(q8.astype(BF16), k_rows.astype(BF16))
    eye = (lax.broadcasted_iota(jnp.int32, (N_HEADS, LANES), 0)
           == lax.broadcasted_iota(jnp.int32, (N_HEADS, LANES), 1))
    return jnp.sum(jnp.where(eye, full, 0.0), axis=0, keepdims=True)


def _fox_decode_kernel(pt_ref, q_ref, kn_ref, vn_ref, lfn_ref, *refs):
    n = PAGES_PER_STEP
    k_refs, v_refs, lf_refs = refs[:n], refs[n:2 * n], refs[2 * n:3 * n]
    o_ref, m_s, l_s, r_s, acc_s = refs[3 * n:]
    j = pl.program_id(1)
    q8 = q_ref[0] * (HEAD_DIM ** -0.5)

    @pl.when(j == 0)
    def _():
        m_s[...] = _self_scores(q8, kn_ref[0])
        l_s[...] = jnp.ones_like(l_s)
        r_s[...] = lfn_ref[0]
        acc_s[...] = vn_ref[0]

    r = lax.broadcasted_iota(jnp.int32, (LANES, LANES), 0)
    c = lax.broadcasted_iota(jnp.int32, (LANES, LANES), 1)
    tri = (r >= c).astype(F32)
    for i in range(n):
        lf = lf_refs[i][...]
        lf = jnp.concatenate([lf, jnp.zeros((PAGE_SIZE, LANES - N_HEADS), F32)], axis=1)
        cs = _dot(tri, lf, precision=HIGHEST)
        total = cs[PAGE_SIZE - 1:PAGE_SIZE, :]
        s = _head_scores(q8, k_refs[i]) + (r_s[...] + total - cs)
        m_old = m_s[...]
        m_new = jnp.maximum(m_old, jnp.max(s, axis=0, keepdims=True))
        alpha = jnp.exp(m_old - m_new)
        p = jnp.exp(s - m_new)
        l_s[...] = alpha * l_s[...] + jnp.sum(p, axis=0, keepdims=True)
        m_s[...] = m_new
        r_s[...] = r_s[...] + total
        acc_s[...] = _head_values(p, v_refs[i], acc_s[...] * _row_to_col(alpha))

    @pl.when(j == pl.num_programs(1) - 1)
    def _():
        o_ref[0] = acc_s[...] / _row_to_col(l_s[...])


def _fox_decode(q, k_new, v_new, lf_new, pool_k, pool_v, pool_lf, page_table):
    b, n_pages = page_table.shape
    n = PAGES_PER_STEP
    steps = n_pages // n
    row3 = lambda i, j, pt: (i, 0, 0)

    def page_spec(shape, off):
        nd = len(shape)
        return pl.BlockSpec((None, None) + shape,
                            lambda i, j, pt: (0, pt[i, n_pages - 1 - (j * n + off)]) + (0,) * nd)

    in_specs = ([pl.BlockSpec((1, N_HEADS, HEAD_DIM), row3)] * 3 + [pl.BlockSpec((1, 1, LANES), row3)]
                + [page_spec((PAGE_SIZE, N_HEADS, HEAD_DIM), i) for i in range(n)] * 2
                + [page_spec((PAGE_SIZE, N_HEADS), i) for i in range(n)])
    grid_spec = pltpu.PrefetchScalarGridSpec(
        num_scalar_prefetch=1, grid=(b, steps), in_specs=in_specs,
        out_specs=pl.BlockSpec((1, N_HEADS, HEAD_DIM), row3),
        scratch_shapes=[pltpu.VMEM((1, LANES), F32)] * 3 + [pltpu.VMEM((N_HEADS, HEAD_DIM), F32)])
    return pl.pallas_call(
        _fox_decode_kernel, grid_spec=grid_spec,
        out_shape=jax.ShapeDtypeStruct((b, N_HEADS, HEAD_DIM), F32),
        compiler_params=_cparams("arbitrary", "arbitrary"),
        name="fox_decode_attn",
    )(page_table, q, k_new, v_new, lf_new, *([pool_k] * n), *([pool_v] * n), *([pool_lf] * n))


def _moba_gate_kernel(pt_ref, q_ref, *refs):
    n = PAGES_PER_STEP
    k_refs = refs[:n]
    idx_ref, gate_s = refs[n:]
    j = pl.program_id(1)
    ppb = MOBA_BLOCK // PAGE_SIZE
    lane = lax.broadcasted_iota(jnp.int32, (N_HEADS, LANES), 1)

    @pl.when(j == 0)
    def _():
        gate_s[...] = jnp.full_like(gate_s, -jnp.inf)

    q8 = q_ref[0]
    for blk in range(n // ppb):
        ksum = None
        for i in range(ppb):
            part = jnp.sum(k_refs[blk * ppb + i][...], axis=0)
            ksum = part if ksum is None else ksum + part
        g = jnp.sum(q8 * (ksum * (1.0 / MOBA_BLOCK)), axis=1, keepdims=True)
        gate_s[...] = jnp.where(lane == j * (n // ppb) + blk, g, gate_s[...])

    @pl.when(j == pl.num_programs(1) - 1)
    def _():
        gate = gate_s[...]
        lane_f = lane.astype(F32)
        picks = jnp.zeros((N_HEADS, LANES), F32)
        for k in range(MOBA_TOPK):
            best = jnp.max(gate, axis=1, keepdims=True)
            first = jnp.min(jnp.where(gate == best, lane_f, float(LANES)), axis=1, keepdims=True)
            picks = jnp.where(lane == k, first, picks)
            gate = jnp.where(lane_f == first, -jnp.inf, gate)
        idx_ref[0] = picks.astype(jnp.int32)


def _moba_gate(q, pool_k, page_table):
    b, n_pages = page_table.shape
    n = PAGES_PER_STEP
    steps = n_pages // n
    row3 = lambda i, j, pt: (i, 0, 0)
    in_specs = [pl.BlockSpec((1, N_HEADS, HEAD_DIM), row3)] + [
        pl.BlockSpec((None, None, PAGE_SIZE, N_HEADS, HEAD_DIM),
                     functools.partial(lambda i, j, pt, off: (0, pt[i, j * n + off], 0, 0, 0), off=off))
        for off in range(n)]
    grid_spec = pltpu.PrefetchScalarGridSpec(
        num_scalar_prefetch=1, grid=(b, steps), in_specs=in_specs,
        out_specs=pl.BlockSpec((1, N_HEADS, LANES), row3),
        scratch_shapes=[pltpu.VMEM((N_HEADS, LANES), F32)])
    return pl.pallas_call(
        _moba_gate_kernel, grid_spec=grid_spec,
        out_shape=jax.ShapeDtypeStruct((b, N_HEADS, LANES), jnp.int32),
        compiler_params=_cparams("arbitrary", "arbitrary"),
        name="moba_decode_gate",
    )(page_table, q, *([pool_k] * n))


def _moba_decode_kernel(pt_ref, idx_ref, q_ref, kn_ref, vn_ref, *refs):
    ppb = MOBA_BLOCK // PAGE_SIZE
    k_refs, v_refs = refs[:ppb], refs[ppb:2 * ppb]
    o_ref, m_s, l_s, acc_s = refs[2 * ppb:]
    h = pl.program_id(1)
    k = pl.program_id(2)
    q8 = q_ref[0] * (HEAD_DIM ** -0.5)
    rows = lax.broadcasted_iota(jnp.int32, (N_HEADS, HEAD_DIM), 0)
    lane = lax.broadcasted_iota(jnp.int32, (1, LANES), 1)
    qh = jnp.where(rows == h, q8, 0.0)

    @pl.when((h == 0) & (k == 0))
    def _():
        m_s[...] = _self_scores(q8, kn_ref[0])
        l_s[...] = jnp.ones_like(l_s)
        acc_s[...] = vn_ref[0]

    mine = lane == h
    for i in range(ppb):
        s = _head_scores(qh, k_refs[i])
        m_old = m_s[...]
        m_new = jnp.where(mine, jnp.maximum(m_old, jnp.max(s, axis=0, keepdims=True)), m_old)
        alpha = jnp.exp(m_old - m_new)
        p = jnp.where(mine, jnp.exp(s - m_new), 0.0)
        l_s[...] = alpha * l_s[...] + jnp.sum(p, axis=0, keepdims=True)
        m_s[...] = m_new
        acc_s[...] = _head_values(p, v_refs[i], acc_s[...] * _row_to_col(alpha))

    @pl.when((h == pl.num_programs(1) - 1) & (k == pl.num_programs(2) - 1))
    def _():
        o_ref[0] = acc_s[...] / _row_to_col(l_s[...])


def _moba_decode(q, k_new, v_new, idx, pool_k, pool_v, page_table):
    b, _ = page_table.shape
    ppb = MOBA_BLOCK // PAGE_SIZE
    row3 = lambda i, h, k, pt, ix: (i, 0, 0)
    page = lambda off: pl.BlockSpec(
        (None, None, PAGE_SIZE, N_HEADS, HEAD_DIM),
        lambda i, h, k, pt, ix: (0, pt[i, ix[i, h * MOBA_TOPK + k] * ppb + off], 0, 0, 0))
    in_specs = ([pl.BlockSpec((1, N_HEADS, HEAD_DIM), row3)] * 3
                + [page(off) for off in range(ppb)] * 2)
    grid_spec = pltpu.PrefetchScalarGridSpec(
        num_scalar_prefetch=2, grid=(b, N_HEADS, MOBA_TOPK), in_specs=in_specs,
        out_specs=pl.BlockSpec((1, N_HEADS, HEAD_DIM), row3),
        scratch_shapes=[pltpu.VMEM((1, LANES), F32)] * 2 + [pltpu.VMEM((N_HEADS, HEAD_DIM), F32)])
    return pl.pallas_call(
        _moba_decode_kernel, grid_spec=grid_spec,
        out_shape=jax.ShapeDtypeStruct((b, N_HEADS, HEAD_DIM), F32),
        compiler_params=_cparams("arbitrary", "arbitrary", "arbitrary"),
        name="moba_decode_attn",
    )(page_table, idx, q, k_new, v_new, *([pool_k] * ppb), *([pool_v] * ppb))


def _pad_lanes(a, width=LANES):
    return jnp.pad(a, [(0, 0)] * (a.ndim - 1) + [(0, width - a.shape[-1])])


def _run_group(x, ada, weights, decode, caches=None):
    b, t, _ = x.shape
    m = b * t
    x2d = x.reshape(m, D_MODEL)
    if decode:
        tm = m
        mod_map = lambda i: (0, 0, 0)
        as_mod = lambda a: a.reshape(1, b, D_MODEL)
        tab_map = lambda i: (0, 0, 0)
    else:
        tm = 512
        tps = t // tm
        mod_map = lambda i: (i // tps, 0, 0)
        as_mod = lambda a: a.reshape(b, 1, D_MODEL)
        tab_map = lambda i: (i % tps, 0, 0)
    row = lambda v: v.reshape(1, -1)

    sh1, sc1, gt1, sh2, sc2, gt2 = [as_mod(a) for a in jnp.split(ada[0], 6, axis=-1)]
    if decode:
        past = caches["page_table"].shape[1] * PAGE_SIZE
        tables = [tb.reshape(1, 1, ATT_W) for tb in _rope_tables(past + np.arange(t))]
    else:
        tables = [tb.reshape(tps, tm, ATT_W) for tb in _rope_tables(np.arange(t))]
    qa, ka, va, qb, kb, vb, lf = _attn_proj(
        x2d, sc1, sh1, row(weights["g_mix_pre"][0]), weights["w_att6"], weights["w_att_f"], weights["b_fox_f"],
        tables, tm, mod_map, tab_map)
    if decode:
        hd = lambda a: a.reshape(b, N_HEADS, HEAD_DIM)
        pt = caches["page_table"]
        idx = _moba_gate(hd(qa), caches["moba_k"], pt)[:, :, :MOBA_TOPK].reshape(b, N_HEADS * MOBA_TOPK)
        oa = _moba_decode(hd(qa), hd(ka), hd(va), idx, caches["moba_k"], caches["moba_v"], pt)
        ob = _fox_decode(hd(qb), hd(kb), hd(vb), lf.reshape(b, 1, LANES), caches["fox_k"], caches["fox_v"],
                         caches["fox_lf"], pt)
        oa, ob = oa.reshape(m, ATT_W), ob.reshape(m, ATT_W)
    else:
        f, ft = _fox_cumsum(lf, b, t)
        oa = _attn_prompt(qa, ka, va, b, t, fox=False)
        ob = _attn_prompt(qb, kb, vb, b, t, fox=True, f=f, ft=ft)
    x2d = _post(x2d, [oa, ob], weights["w_att_out"], (gt1, sc2, sh2, gt2),
                (row(weights["g_mix_post"][0]), row(weights["g_ffn_pre"][0]), row(weights["g_ffn_post"][0])),
                weights["w_ffn_in"][0], weights["w_ffn_out"][0], tm, mod_map, ssm=False)

    sh1, sc1, gt1, sh2, sc2, gt2 = [as_mod(a) for a in jnp.split(ada[1], 6, axis=-1)]
    prev = caches["state_conv"] if decode else None
    z, xc, dt, conv_state = _ssm_proj(
        x2d, sc1, sh1, row(weights["g_mix_pre"][1]), weights["w_ssm_z"], weights["w_ssm_x"], weights["w_ssm_dt"],
        weights["dt_bias"], weights["conv_w"], weights["conv_b"], tm, mod_map, t, prev=prev)
    if decode:
        y, h_t = _ssd_step(xc, dt, weights["a_log"], weights["d_skip"], caches["state_ssm"])
        y = y.reshape(m, D_INNER)
    else:
        y, h_t = _ssd_prompt(xc, dt, weights["a_log"], weights["d_skip"], b, t)
    x2d = _post(x2d, [y, z], weights["w_ssm_out"], (gt1, sc2, sh2, gt2),
                (row(weights["g_mix_post"][1]), row(weights["g_ffn_pre"][1]), row(weights["g_ffn_post"][1])),
                weights["w_ffn_in"][1], weights["w_ffn_out"][1], tm, mod_map, ssm=True,
                gn=row(weights["g_ssm_norm"]))

    rows5 = lambda a: a.reshape(1, b, t, N_HEADS, HEAD_DIM)
    return (x2d.reshape(b, t, D_MODEL), rows5(ka), rows5(va), rows5(kb), rows5(vb),
            lf[:, :N_HEADS].reshape(1, b, t, N_HEADS),
            h_t.reshape(1, b, SSM_HEADS, SSM_HEAD_DIM, D_STATE), conv_state.reshape(1, b, CONV_W - 1, CONV_DIM))


def kernel(x_prompt, x_sample, cache_moba_k, cache_moba_v, cache_fox_k, cache_fox_v, cache_fox_logf, state_ssm, state_conv, page_table, c_prompt, c_sample, w_ada, b_ada, g_mix_pre, g_mix_post, g_ffn_pre, g_ffn_post, w_att_in, b_fox_f, w_att_out, w_ssm_in, conv_w, conv_b, dt_bias, a_log, d_skip, g_ssm_norm, w_ssm_out, w_ffn_in, w_ffn_out):
    bp = x_prompt.shape[0]
    bs = x_sample.shape[0]
    ada = _ada(jnp.concatenate([c_prompt, c_sample], axis=0), w_ada, b_ada)
    w_att = w_att_in[0]
    w_ssm = w_ssm_in[0]
    weights = dict(
        g_mix_pre=g_mix_pre, g_mix_post=g_mix_post, g_ffn_pre=g_ffn_pre, g_ffn_post=g_ffn_post,
        w_att6=w_att[:, :6 * ATT_W].astype(BF16),
        w_att_f=_pad_lanes(w_att[:, 6 * ATT_W:]).astype(BF16),
        b_fox_f=_pad_lanes(b_fox_f[0].reshape(1, -1)),
        w_att_out=w_att_out[0].astype(BF16),
        w_ssm_z=w_ssm[:, :D_INNER].astype(BF16),
        w_ssm_x=w_ssm[:, D_INNER:D_INNER + CONV_DIM].astype(BF16),
        w_ssm_dt=_pad_lanes(w_ssm[:, D_INNER + CONV_DIM:]).astype(BF16),
        dt_bias=_pad_lanes(dt_bias[0].reshape(1, -1)),
        conv_w=conv_w[0], conv_b=conv_b[0].reshape(1, -1),
        a_log=_pad_lanes(a_log[0].reshape(1, -1)),
        d_skip=jnp.repeat(d_skip[0], SSM_HEAD_DIM).reshape(1, -1),
        g_ssm_norm=g_ssm_norm[0],
        w_ssm_out=w_ssm_out[0].astype(BF16),
        w_ffn_in=w_ffn_in.astype(BF16), w_ffn_out=w_ffn_out.astype(BF16),
    )
    caches = dict(page_table=page_table, moba_k=cache_moba_k, moba_v=cache_moba_v, fox_k=cache_fox_k,
                  fox_v=cache_fox_v, fox_lf=cache_fox_logf,
                  state_ssm=state_ssm[0].reshape(bs, D_INNER, D_STATE), state_conv=state_conv[0])
    prompt = _run_group(x_prompt, ada[:, :bp], weights, decode=False)
    sample = _run_group(x_sample, ada[:, bp:], weights, decode=True, caches=caches)
    return (prompt[0], sample[0]) + prompt[1:] + sample[1:]
```

```python
import functools
import math

import numpy as np
import jax
import jax.numpy as jnp
from jax import lax
from jax.experimental import pallas as pl
from jax.experimental.pallas import tpu as pltpu

F32 = jnp.float32
BF16 = jnp.bfloat16
HIGHEST = lax.Precision.HIGHEST

D_MODEL = 1024
HEAD_DIM = 64
N_HEADS = 8
ATT_W = N_HEADS * HEAD_DIM
ROT_DIM = HEAD_DIM // 4
ROPE_THETA = 500000.0
MOBA_BLOCK = 256
MOBA_TOPK = 3
PAGE_SIZE = 128
D_INNER = 2 * D_MODEL
SSM_HEAD_DIM = 64
SSM_HEADS = D_INNER // SSM_HEAD_DIM
SSM_GROUPS = 4
SSM_HPG = SSM_HEADS // SSM_GROUPS
D_STATE = 128
CONV_W = 4
CONV_DIM = D_INNER + 2 * SSM_GROUPS * D_STATE
SSD_CHUNK = 128
D_FF = ((8 * D_MODEL + 3 * 256 - 1) // (3 * 256)) * 256
FFN_CHUNK = D_FF // 2
EPS = 1e-6
LANES = 128
VMEM_LIMIT = 56 * 1024 * 1024
PAGES_PER_STEP = 8
LOG2E = math.log2(math.e)
NEG = -1e30

_NT = (((1,), (1,)), ((), ()))


def _cparams(*sem):
    return pltpu.CompilerParams(dimension_semantics=sem, vmem_limit_bytes=VMEM_LIMIT)


def _const_spec(shape):
    nd = len(shape)
    return pl.BlockSpec(shape, lambda *_: (0,) * nd, pipeline_mode=pl.Buffered(1))


def _rms(x, g):
    return x * lax.rsqrt(jnp.mean(x * x, axis=-1, keepdims=True) + EPS) * g


def _silu(x):
    return x * jax.nn.sigmoid(x)


def _softplus(x):
    return jnp.maximum(x, 0.0) + jnp.log1p(jnp.exp(-jnp.abs(x)))


def _log_sigmoid(x):
    return jnp.minimum(x, 0.0) - jnp.log1p(jnp.exp(-jnp.abs(x)))


def _dot(a, b, **kw):
    return jnp.dot(a, b, preferred_element_type=F32, **kw)


def _dot_nt(a, b, **kw):
    return lax.dot_general(a, b, _NT, preferred_element_type=F32, **kw)


def _split3(x):
    hi = x.astype(BF16).astype(F32)
    r = x - hi
    mid = r.astype(BF16).astype(F32)
    return hi, mid, r - mid


def _ada_kernel(c_ref, w_ref, b_ref, o_ref):
    a = _silu(c_ref[...]).astype(BF16)
    o_ref[0] = _dot(a, w_ref[0].astype(BF16)) + b_ref[0]


def _ada(c_all, w_ada, b_ada):
    depth, _, n = w_ada.shape
    rows = c_all.shape[0]
    tn = 1024
    return pl.pallas_call(
        _ada_kernel,
        grid=(depth, n // tn),
        in_specs=[pl.BlockSpec((rows, D_MODEL), lambda l, j: (0, 0)),
                  pl.BlockSpec((1, D_MODEL, tn), lambda l, j: (l, 0, j)),
                  pl.BlockSpec((1, 1, tn), lambda l, j: (l, 0, j))],
        out_specs=pl.BlockSpec((1, rows, tn), lambda l, j: (l, 0, j)),
        out_shape=jax.ShapeDtypeStruct((depth, rows, n), F32),
        compiler_params=_cparams("arbitrary", "arbitrary"),
        name="ada_terms",
    )(c_all, w_ada, b_ada.reshape(depth, 1, n))


def _rope_tables(pos):
    half = ROT_DIM // 2
    inv = ROPE_THETA ** (-2.0 * np.arange(half, dtype=np.float64) / ROT_DIM)
    ang = np.asarray(pos, np.float64)[:, None] * inv[None, :]
    cos, sin = np.cos(ang), np.sin(ang)
    n = len(pos)
    c64 = np.concatenate([cos, cos, np.ones((n, HEAD_DIM - ROT_DIM))], axis=1)
    s1 = np.concatenate([-sin, np.zeros((n, HEAD_DIM - half))], axis=1)
    s2 = np.concatenate([np.zeros((n, half)), sin, np.zeros((n, HEAD_DIM - ROT_DIM))], axis=1)
    return [np.tile(t, (1, N_HEADS)).astype(np.float32) for t in (c64, s1, s2)]


def _rope(z, cos, s1, s2, axis):
    half = ROT_DIM // 2
    return z * cos + pltpu.roll(z, ATT_W - half, axis) * s1 + pltpu.roll(z, half, axis) * s2


def _attn_proj_kernel(*refs, prompt):
    x_ref, sc_ref, sh_ref, g_ref, wt_ref, wtf_ref, bf_ref, cos_ref, s1_ref, s2_ref = refs[:10]
    if prompt:
        cost_ref, s1t_ref, s2t_ref = refs[10:13]
        qa_ref, qb_ref, ka_ref, va_ref, kb_ref, vb_ref, lf_ref = refs[13:]
    else:
        qa_ref, qb_ref, ka_ref, va_ref, kb_ref, vb_ref, lf_ref = refs[10:]
    h = (_rms(x_ref[...], g_ref[...]) * (1.0 + sc_ref[0]) + sh_ref[0]).astype(BF16)
    w = lambda n: wt_ref[n * ATT_W:(n + 1) * ATT_W, :]
    qa_ref[...] = _rope(_dot_nt(h, w(0)), cos_ref[0], s1_ref[0], s2_ref[0], 1)
    qb_ref[...] = _dot_nt(h, w(3))
    wtf = wtf_ref[...].astype(BF16)
    if prompt:
        ka_ref[0] = _rope(_dot_nt(w(1), h), cost_ref[...], s1t_ref[...], s2t_ref[...], 0)
        va_ref[0] = _dot_nt(w(2), h)
        kb_ref[0] = _dot_nt(w(4), h)
        vb_ref[0] = _dot_nt(w(5), h)
        lf_ref[0] = _log_sigmoid(_dot_nt(wtf, h) + bf_ref[...])
    else:
        ka_ref[...] = _rope(_dot_nt(h, w(1)), cos_ref[0], s1_ref[0], s2_ref[0], 1)
        va_ref[...] = _dot_nt(h, w(2))
        kb_ref[...] = _dot_nt(h, w(4))
        vb_ref[...] = _dot_nt(h, w(5))
        lf_ref[...] = _log_sigmoid(_dot_nt(h, wtf) + bf_ref[...])


def _attn_proj(x2d, sc, sh, g, wt6, wtf, bf, pos, tm, mod_map, seq_len, prompt):
    m = x2d.shape[0]
    row = lambda i: (i, 0)
    mod_spec = pl.BlockSpec((1,) + sc.shape[1:], mod_map)
    tabs = _rope_tables(pos)
    wide = jax.ShapeDtypeStruct((m, ATT_W), F32)
    in_specs = [pl.BlockSpec((tm, D_MODEL), row), mod_spec, mod_spec, _const_spec((1, D_MODEL)),
                _const_spec(wt6.shape), _const_spec(wtf.shape)]
    if prompt:
        tps = seq_len // tm
        nseq = m // seq_len
        tab_spec = pl.BlockSpec((1, tm, ATT_W), lambda i: (i % tps, 0, 0))
        tabt_spec = pl.BlockSpec((ATT_W, tm), lambda i: (0, i % tps))
        fm_spec = pl.BlockSpec((1, ATT_W, tm), lambda i: (i // tps, 0, i % tps))
        fm = jax.ShapeDtypeStruct((nseq, ATT_W, seq_len), F32)
        in_specs += [_const_spec((N_HEADS, 1))] + [tab_spec] * 3 + [tabt_spec] * 3
        args = ([jnp.asarray(tb.reshape(tps, tm, ATT_W)) for tb in tabs]
                + [jnp.asarray(np.ascontiguousarray(tb.T)) for tb in tabs])
        out_specs = [pl.BlockSpec((tm, ATT_W), row)] * 2 + [fm_spec] * 4 + [
            pl.BlockSpec((1, N_HEADS, tm), lambda i: (i // tps, 0, i % tps))]
        out_shape = [wide] * 2 + [fm] * 4 + [jax.ShapeDtypeStruct((nseq, N_HEADS, seq_len), F32)]
        bf = bf.reshape(N_HEADS, 1)
    else:
        in_specs += [_const_spec((1, N_HEADS))] + [_const_spec((1, 1, ATT_W))] * 3
        args = [jnp.asarray(tb.reshape(1, 1, ATT_W)) for tb in tabs]
        out_specs = [pl.BlockSpec((tm, ATT_W), row)] * 6 + [pl.BlockSpec((tm, N_HEADS), row)]
        out_shape = [wide] * 6 + [jax.ShapeDtypeStruct((m, N_HEADS), F32)]
        bf = bf.reshape(1, N_HEADS)
    return pl.pallas_call(
        functools.partial(_attn_proj_kernel, prompt=prompt),
        grid=(m // tm,),
        in_specs=in_specs,
        out_specs=out_specs,
        out_shape=out_shape,
        compiler_params=_cparams("arbitrary"),
        name="attn_in_proj" if prompt else "attn_in_proj_decode",
    )(x2d, sc, sh, g, wt6, wtf, bf, *args)


def _cumsum_kernel(lf_ref, ft_ref, fcol_ref):
    t = lf_ref.shape[2]
    r = lax.broadcasted_iota(jnp.int32, (LANES, LANES), 0)
    c = lax.broadcasted_iota(jnp.int32, (LANES, LANES), 1)
    triu = (r <= c).astype(F32)
    carry = jnp.zeros((N_HEADS, 1), F32)
    pad = jnp.zeros((LANES - N_HEADS, LANES), F32)
    for i in range(t // LANES):
        cols = slice(i * LANES, (i + 1) * LANES)
        blk = _dot(lf_ref[0, :, cols], triu, precision=HIGHEST) + carry
        carry = blk[:, LANES - 1:LANES]
        ft_ref[0, :, cols] = blk
        fcol_ref[cols, :] = jnp.concatenate([blk, pad], axis=0).T


def _fox_cumsum(lf_t, b, t):
    return pl.pallas_call(
        _cumsum_kernel,
        grid=(b,),
        in_specs=[pl.BlockSpec((1, N_HEADS, t), lambda i: (i, 0, 0))],
        out_specs=[pl.BlockSpec((1, N_HEADS, t), lambda i: (i, 0, 0)),
                   pl.BlockSpec((t, LANES), lambda i: (i, 0))],
        out_shape=[jax.ShapeDtypeStruct((b, N_HEADS, t), F32), jax.ShapeDtypeStruct((b * t, LANES), F32)],
        compiler_params=_cparams("arbitrary"),
        name="fox_cumsum",
    )(lf_t)


def _attn_prompt_kernel(*refs, fox, t):
    if fox:
        q_ref, k_ref, v_ref, fcol_ref, ft_ref, o_ref, kaug, vaug = refs
    else:
        q_ref, k_ref, v_ref, o_ref, kaug, vaug = refs
    blk = MOBA_BLOCK
    nb = t // blk
    pair = pl.program_id(1)
    c_exp = (HEAD_DIM ** -0.5) * LOG2E
    row_t = lax.broadcasted_iota(jnp.int32, (LANES, t), 0)
    low_t = row_t < HEAD_DIM
    r64 = row_t & (HEAD_DIM - 1)
    kf = k_ref[0]
    vf = v_ref[0]
    if fox:
        augs = []
        for hd in range(2):
            hi, mid, lo = _split3(ft_ref[0, pl.ds(2 * pair + hd, 1), :] * LOG2E)
            augs.append(jnp.where(r64 < 3, 1.0, jnp.where(r64 == 3, -hi, jnp.where(
                r64 == 4, -mid, jnp.where(r64 == 5, -lo, 0.0)))))
        kaug[0] = jnp.where(low_t, kf, augs[0]).astype(BF16)
        kaug[1] = jnp.where(low_t, augs[1], kf).astype(BF16)
    else:
        lane_t = lax.broadcasted_iota(jnp.int32, (LANES, t), 1)
        key_blk = lane_t >> (blk.bit_length() - 1)
        ind = jnp.where((key_blk == r64) & (r64 < nb), 1.0, 0.0)
        kaug[0] = jnp.where(low_t, kf, ind).astype(BF16)
        kaug[1] = jnp.where(low_t, ind, kf).astype(BF16)
        hs = 8 * pl.cdiv(nb, 8)
        l128 = lax.broadcasted_iota(jnp.int32, (LANES, LANES), 1)
        r128 = lax.broadcasted_iota(jnp.int32, (LANES, LANES), 0)
        kcols = jnp.zeros((LANES, LANES), F32)
        for n in range(nb):
            col = jnp.mean(kf[:, n * blk:(n + 1) * blk], axis=1, keepdims=True)
            kcols = jnp.where((l128 == n) | (l128 == hs + n), col, kcols)
        kmean = jnp.where(((r128 < hs) & (l128 < HEAD_DIM)) | ((r128 >= hs) & (l128 >= HEAD_DIM)), kcols.T, 0.0)
        blk_id = lax.broadcasted_iota(jnp.int32, (hs, blk), 0)
        fill = jnp.zeros((HEAD_DIM - hs, blk), F32)
    vaug[0] = jnp.where(low_t, vf, jnp.where(row_t == HEAD_DIM, 1.0, 0.0)).astype(BF16)
    vaug[1] = jnp.where(low_t, jnp.where(row_t == 0, 1.0, 0.0), vf).astype(BF16)

    lane = lax.broadcasted_iota(jnp.int32, (blk, LANES), 1)
    low = lane < HEAD_DIM
    l64 = lane & (HEAD_DIM - 1)
    causal = (lax.broadcasted_iota(jnp.int32, (blk, blk), 1) <= lax.broadcasted_iota(jnp.int32, (blk, blk), 0))

    def q_block(qi, carry):
        q0 = pl.multiple_of(qi * blk, blk)
        q = q_ref[pl.ds(q0, blk), :]
        if fox:
            fq = fcol_ref[pl.ds(q0, blk), :] * LOG2E
            parts = []
            for hd in range(2):
                col = jnp.sum(jnp.where(lane == 2 * pair + hd, fq, 0.0), axis=1, keepdims=True)
                hi, mid, lo = _split3(col)
                parts.append(jnp.where(l64 == 0, hi, jnp.where(l64 == 1, mid, jnp.where(
                    l64 == 2, lo, jnp.where(l64 < 6, 1.0, 0.0)))))
            aug = jnp.where(low, parts[1], parts[0])
        else:
            gate_t = _dot_nt(kmean, q, precision=HIGHEST)
            aug_t = []
            for hd in range(2):
                gate = gate_t[hd * hs:(hd + 1) * hs, :]
                beaten = jnp.zeros((hs, blk), F32)
                for n2 in range(nb):
                    g2 = gate[n2:n2 + 1, :]
                    better = (g2 > gate) | ((g2 == gate) & (n2 < blk_id))
                    beaten = beaten + jnp.where(better, 1.0, 0.0) * (n2 < qi).astype(F32)
                chosen = ((beaten < MOBA_TOPK) & (blk_id < qi)) | (blk_id == qi)
                aug_t.append(jnp.where((blk_id < nb) & jnp.logical_not(chosen), NEG, 0.0))
            aug = jnp.concatenate([aug_t[1], fill, aug_t[0], fill], axis=0).T
        qc = q * c_exp
        qaug = (jnp.where(low, qc, aug).astype(BF16), jnp.where(low, aug, qc).astype(BF16))

        def step(kb, state, diag):
            k0 = pl.multiple_of(kb * blk, blk)
            new = []
            for hd in range(2):
                m, acc = state[hd]
                s = _dot(qaug[hd], kaug[hd, :, pl.ds(k0, blk)])
                if diag:
                    s = jnp.where(causal, s, NEG)
                m_new = jnp.maximum(m, jnp.max(s, axis=1, keepdims=True))
                p = jnp.exp2(s - m_new).astype(BF16)
                acc = jnp.exp2(m - m_new) * acc + _dot_nt(p, vaug[hd, :, pl.ds(k0, blk)])
                new.append((m_new, acc))
            return tuple(new)

        init = tuple((jnp.full((blk, 1), NEG, F32), jnp.zeros((blk, LANES), F32)) for _ in range(2))
        st = step(qi, init, True)
        st = lax.fori_loop(0, qi, lambda kb, s_: step(kb, s_, False), st)
        outs = []
        for hd in range(2):
            acc = st[hd][1]
            denom = jnp.sum(jnp.where(lane == (HEAD_DIM if hd == 0 else 0), acc, 0.0), axis=1, keepdims=True)
            outs.append(acc / denom)
        o_ref[pl.ds(q0, blk), :] = jnp.where(low, outs[0], outs[1])
        return carry

    lax.fori_loop(0, nb, q_block, 0)


def _attn_prompt(q, k_t, v_t, b, t, fox, ft=None, fcol=None):
    pairs = ATT_W // LANES
    slab = pl.BlockSpec((t, LANES), lambda i, p: (i, p))
    slab_t = pl.BlockSpec((1, LANES, t), lambda i, p: (i, p, 0))
    in_specs = [slab, slab_t, slab_t]
    args = [q, k_t, v_t]
    if fox:
        in_specs += [pl.BlockSpec((t, LANES), lambda i, p: (i, 0)),
                     pl.BlockSpec((1, N_HEADS, t), lambda i, p: (i, 0, 0))]
        args += [fcol, ft]
    return pl.pallas_call(
        functools.partial(_attn_prompt_kernel, fox=fox, t=t),
        grid=(b, pairs),
        in_specs=in_specs,
        out_specs=slab,
        out_shape=jax.ShapeDtypeStruct((b * t, ATT_W), F32),
        scratch_shapes=[pltpu.VMEM((2, LANES, t), BF16), pltpu.VMEM((2, LANES, t), BF16)],
        compiler_params=_cparams("arbitrary", "arbitrary"),
        name="fox_prompt_attn" if fox else "moba_prompt_attn",
    )(*args)


def _post_kernel(*refs, ssm):
    if ssm:
        x_ref, y_ref, z_ref, gn_ref, wmix_ref = refs[:5]
        rest = refs[5:]
    else:
        x_ref, oa_ref, ob_ref, wmix_ref = refs[:4]
        rest = refs[4:]
    gt1_ref, gpost_ref, gpre_ref, sc2_ref, sh2_ref, gt2_ref, gfpost_ref, win_ref, wout_ref, o_ref = rest
    if ssm:
        y = y_ref[...] * _silu(z_ref[...])
        gs = D_INNER // SSM_GROUPS
        parts = []
        for g in range(SSM_GROUPS):
            yg = y[:, g * gs:(g + 1) * gs]
            parts.append(yg * lax.rsqrt(jnp.mean(yg * yg, axis=-1, keepdims=True) + EPS))
        yn = (jnp.concatenate(parts, axis=-1) * gn_ref[...]).astype(BF16)
        mix = _dot(yn, wmix_ref[...])
    else:
        mix = (_dot(oa_ref[...].astype(BF16), wmix_ref[0:ATT_W, :])
               + _dot(ob_ref[...].astype(BF16), wmix_ref[ATT_W:2 * ATT_W, :]))
    x1 = x_ref[...] + gt1_ref[0] * _rms(mix, gpost_ref[...])
    h2 = (_rms(x1, gpre_ref[...]) * (1.0 + sc2_ref[0]) + sh2_ref[0]).astype(BF16)
    acc = None
    for c in range(D_FF // FFN_CHUNK):
        lo = c * FFN_CHUNK
        gate = _dot(h2, win_ref[:, lo:lo + FFN_CHUNK])
        up = _dot(h2, win_ref[:, D_FF + lo:D_FF + lo + FFN_CHUNK])
        part = _dot((_silu(gate) * up).astype(BF16), wout_ref[lo:lo + FFN_CHUNK, :])
        acc = part if acc is None else acc + part
    o_ref[...] = x1 + gt2_ref[0] * _rms(acc, gfpost_ref[...])


def _post(x2d, mix_in, wmix, mods, gains, win, wout, tm, mod_map, ssm, gn=None):
    m = x2d.shape[0]
    row = lambda i: (i, 0)
    gt1, sc2, sh2, gt2 = mods
    gpost, gpre, gfpost = gains
    mod_spec = pl.BlockSpec((1,) + gt1.shape[1:], mod_map)
    vec = _const_spec((1, D_MODEL))
    in_specs = [pl.BlockSpec((tm, D_MODEL), row)]
    args = [x2d]
    for a in mix_in:
        in_specs.append(pl.BlockSpec((tm, a.shape[1]), row))
        args.append(a)
    if ssm:
        in_specs.append(_const_spec((1, D_INNER)))
        args.append(gn)
    in_specs += [_const_spec(wmix.shape), mod_spec, vec, vec, mod_spec, mod_spec, mod_spec, vec,
                 _const_spec(win.shape), _const_spec(wout.shape)]
    args += [wmix, gt1, gpost, gpre, sc2, sh2, gt2, gfpost, win, wout]
    return pl.pallas_call(
        functools.partial(_post_kernel, ssm=ssm),
        grid=(m // tm,),
        in_specs=in_specs,
        out_specs=pl.BlockSpec((tm, D_MODEL), row),
        out_shape=jax.ShapeDtypeStruct((m, D_MODEL), F32),
        compiler_params=_cparams("arbitrary"),
        name="ssm_out_ffn" if ssm else "attn_out_ffn",
    )(*args)


def _ssm_proj_kernel(*refs, decode, tiles_per_seq):
    if decode:
        (x_ref, sc_ref, sh_ref, g_ref, wz_ref, wx_ref, wdt_ref, dtb_ref, cw_ref, cb_ref, prev_ref,
         z_ref, xc_ref, dt_ref, cs_ref) = refs
    else:
        (x_ref, sc_ref, sh_ref, g_ref, wz_ref, wx_ref, wdt_ref, dtb_ref, cw_ref, cb_ref,
         z_ref, xc_ref, dt_ref, cs_ref, ubuf) = refs
    tm = x_ref.shape[0]
    h = (_rms(x_ref[...], g_ref[...]) * (1.0 + sc_ref[0]) + sh_ref[0]).astype(BF16)
    z_ref[...] = _dot_nt(h, wz_ref[...])
    dt_ref[...] = _softplus(_dot_nt(h, wdt_ref[...]) + dtb_ref[...])
    u = _dot_nt(h, wx_ref[...])
    w = [cw_ref[j:j + 1, :] for j in range(CONV_W)]
    if decode:
        p0, p1, p2 = prev_ref[0], prev_ref[1], prev_ref[2]
        y = cb_ref[...] + w[3] * u + w[2] * p2 + w[1] * p1 + w[0] * p0
        cs_ref[0] = p1
        cs_ref[1] = p2
        cs_ref[2] = u
    else:
        pad = 8
        @pl.when(pl.program_id(0) % tiles_per_seq == 0)
        def _():
            ubuf[0:pad, :] = jnp.zeros((pad, CONV_DIM), F32)
        ubuf[pad:pad + tm, :] = u
        y = cb_ref[...] + w[3] * u
        for j in range(CONV_W - 1):
            y = y + w[j] * ubuf[pad - (CONV_W - 1) + j:pad - (CONV_W - 1) + j + tm, :]
        cs_ref[:, 0, 0, :] = ubuf[pad + tm - (CONV_W - 1):pad + tm, :]
        ubuf[0:pad, :] = ubuf[tm:tm + pad, :]
    xc_ref[...] = _silu(y)


def _ssm_proj(x2d, sc, sh, g, wz, wx, wdt, dtb, cw, cb, tm, mod_map, seq_len, prev=None):
    m = x2d.shape[0]
    decode = prev is not None
    row = lambda i: (i, 0)
    mod_spec = pl.BlockSpec((1,) + sc.shape[1:], mod_map)
    in_specs = [pl.BlockSpec((tm, D_MODEL), row), mod_spec, mod_spec, _const_spec((1, D_MODEL)),
                _const_spec(wz.shape), _const_spec(wx.shape), _const_spec(wdt.shape), _const_spec((1, LANES)),
                _const_spec(cw.shape), _const_spec((1, CONV_DIM))]
    args = [x2d, sc, sh, g, wz, wx, wdt, dtb, cw, cb]
    scratch = []
    if decode:
        in_specs.append(_const_spec(prev.shape))
        args.append(prev)
        cs_shape = prev.shape
        cs_spec = _const_spec(prev.shape)
        tiles_per_seq = 1
    else:
        tiles_per_seq = seq_len // tm
        nseq = m // seq_len
        cs_shape = (CONV_W - 1, nseq, 1, CONV_DIM)
        cs_spec = pl.BlockSpec((CONV_W - 1, 1, 1, CONV_DIM), lambda i: (0, i // tiles_per_seq, 0, 0))
        scratch = [pltpu.VMEM((tm + 8, CONV_DIM), F32)]
    z, xc, dt, cs = pl.pallas_call(
        functools.partial(_ssm_proj_kernel, decode=decode, tiles_per_seq=tiles_per_seq),
        grid=(m // tm,),
        in_specs=in_specs,
        out_specs=[pl.BlockSpec((tm, D_INNER), row), pl.BlockSpec((tm, CONV_DIM), row),
                   pl.BlockSpec((tm, LANES), row), cs_spec],
        out_shape=[jax.ShapeDtypeStruct((m, D_INNER), F32), jax.ShapeDtypeStruct((m, CONV_DIM), F32),
                   jax.ShapeDtypeStruct((m, LANES), F32), jax.ShapeDtypeStruct(cs_shape, F32)],
        scratch_shapes=scratch,
        compiler_params=_cparams("arbitrary"),
        name="ssm_in_proj_decode" if decode else "ssm_in_proj",
    )(*args)
    return z, xc, dt, cs.reshape(CONV_W - 1, -1, CONV_DIM)


def _head_expand():
    e = np.zeros((LANES, D_INNER), np.float32)
    for hh in range(SSM_HEADS):
        e[hh, hh * SSM_HEAD_DIM:(hh + 1) * SSM_HEAD_DIM] = 1.0
    return e


def _ssd_kernel(xs_ref, b_ref, c_ref, dt_ref, alog_ref, dsk_ref, e_ref, et_ref, y_ref, ht_ref, hst):
    q = SSD_CHUNK
    gw = D_INNER // SSM_GROUPS
    ci = pl.program_id(1)

    @pl.when(ci == 0)
    def _():
        hst[...] = jnp.zeros_like(hst)

    r = lax.broadcasted_iota(jnp.int32, (q, q), 0)
    c = lax.broadcasted_iota(jnp.int32, (q, q), 1)
    causal = r >= c
    lane = lax.broadcasted_iota(jnp.int32, (q, LANES), 1)
    first_half = lane < SSM_HEAD_DIM
    dt = dt_ref[...]
    a = -jnp.exp(alog_ref[...])
    acum = _dot(causal.astype(F32), dt * a, precision=HIGHEST)
    acum_t = acum.T
    dt_t = dt.T
    a_last = acum[q - 1:q, :]
    expand = e_ref[...]
    w_full = _dot(jnp.exp(a_last - acum) * dt, expand, precision=HIGHEST)
    ea_full = _dot(jnp.exp(acum), expand, precision=HIGHEST)
    dec_rows = _dot(et_ref[...], jnp.exp(jnp.broadcast_to(acum_t[:, q - 1:q], (LANES, LANES))),
                    precision=HIGHEST)
    xs = xs_ref[...]
    xs_bf = xs.astype(BF16)
    xw = xs * w_full
    dsk = dsk_ref[...]
    for g in range(SSM_GROUPS):
        bg = b_ref[:, g * D_STATE:(g + 1) * D_STATE].astype(BF16)
        cg = c_ref[:, g * D_STATE:(g + 1) * D_STATE].astype(BF16)
        cb = _dot_nt(cg, bg)
        rows = slice(g * gw, (g + 1) * gw)
        h_in = hst[rows, :]
        y_inter = _dot_nt(cg, h_in.astype(BF16))
        parts = []
        for pr in range(SSM_HPG // 2):
            h0 = g * SSM_HPG + 2 * pr
            x_pair = xs_bf[:, h0 * SSM_HEAD_DIM:(h0 + 2) * SSM_HEAD_DIM]
            outs = []
            for hh in (h0, h0 + 1):
                seg = acum[:, hh:hh + 1] - acum_t[hh:hh + 1, :]
                mh = cb * jnp.exp(jnp.where(causal, seg, -jnp.inf)) * dt_t[hh:hh + 1, :]
                outs.append(_dot(mh.astype(BF16), x_pair))
            parts.append(jnp.where(first_half, outs[0], outs[1]))
        y_intra = jnp.concatenate(parts, axis=-1)
        y_ref[:, rows] = y_intra + y_inter * ea_full[:, rows] + dsk[:, rows] * xs[:, rows]
        s_inc = _dot(xw[:, rows].T.astype(BF16), bg)
        hst[rows, :] = h_in * dec_rows[rows, :] + s_inc

    @pl.when(ci == pl.num_programs(1) - 1)
    def _():
        ht_ref[0] = hst[...]


def _ssd_prompt(xc, dt, alog, dsk, b, t):
    nc = t // SSD_CHUNK
    e = jnp.asarray(_head_expand())
    et = jnp.asarray(_head_expand().T.copy())
    rowmap = lambda i, c: (i * nc + c, 0)
    bc_w = SSM_GROUPS * D_STATE
    return pl.pallas_call(
        _ssd_kernel,
        grid=(b, nc),
        in_specs=[pl.BlockSpec((SSD_CHUNK, D_INNER), rowmap),
                  pl.BlockSpec((SSD_CHUNK, bc_w), lambda i, c: (i * nc + c, D_INNER // bc_w)),
                  pl.BlockSpec((SSD_CHUNK, bc_w), lambda i, c: (i * nc + c, D_INNER // bc_w + 1)),
                  pl.BlockSpec((SSD_CHUNK, LANES), rowmap),
                  _const_spec((1, LANES)), _const_spec((1, D_INNER)),
                  _const_spec((LANES, D_INNER)), _const_spec((D_INNER, LANES))],
        out_specs=[pl.BlockSpec((SSD_CHUNK, D_INNER), rowmap),
                   pl.BlockSpec((1, D_INNER, D_STATE), lambda i, c: (i, 0, 0))],
        out_shape=[jax.ShapeDtypeStruct((b * t, D_INNER), F32),
                   jax.ShapeDtypeStruct((b, D_INNER, D_STATE), F32)],
        scratch_shapes=[pltpu.VMEM((D_INNER, D_STATE), F32)],
        compiler_params=_cparams("arbitrary", "arbitrary"),
        name="ssd_scan",
    )(xc, xc, xc, dt, alog, dsk, e, et)


def _ssd_step_kernel(xs_ref, b_ref, c_ref, dt_ref, alog_ref, dsk_ref, e_ref, et_ref, h0_ref, y_ref, ht_ref):
    gw = D_INNER // SSM_GROUPS
    dt = dt_ref[0]
    a = -jnp.exp(alog_ref[...])
    dec = jnp.exp(dt * a)
    expand = e_ref[...]
    eye = (lax.broadcasted_iota(jnp.int32, (LANES, LANES), 0)
           == lax.broadcasted_iota(jnp.int32, (LANES, LANES), 1))
    dec_diag = jnp.where(eye, jnp.broadcast_to(dec, (LANES, LANES)), 0.0)
    dec_rows = _dot(_dot(et_ref[...], dec_diag, precision=HIGHEST), jnp.ones((LANES, LANES), F32),
                    precision=HIGHEST)
    xs = xs_ref[0]
    dtx = xs * _dot(dt, expand, precision=HIGHEST)
    eye_g = (lax.broadcasted_iota(jnp.int32, (gw, gw), 0) == lax.broadcasted_iota(jnp.int32, (gw, gw), 1))
    bm = b_ref[0]
    cm = c_ref[0]
    ys = []
    for g in range(SSM_GROUPS):
        rows = slice(g * gw, (g + 1) * gw)
        bg = bm[:, g * D_STATE:(g + 1) * D_STATE]
        cg = cm[:, g * D_STATE:(g + 1) * D_STATE]
        diag = jnp.where(eye_g, jnp.broadcast_to(dtx[:, rows], (gw, gw)), 0.0).astype(BF16)
        outer = _dot(diag, jnp.broadcast_to(bg, (gw, D_STATE)).astype(BF16))
        h_new = h0_ref[0, rows, :] * dec_rows[rows, :] + outer
        ht_ref[0, rows, :] = h_new
        yg = _dot_nt(jnp.broadcast_to(cg, (8, D_STATE)).astype(BF16), h_new.astype(BF16))
        ys.append(yg[0:1, :])
    y_ref[0] = jnp.concatenate(ys, axis=-1) + dsk_ref[...] * xs


def _ssd_step(xc, dt, alog, dsk, h0):
    b = xc.shape[0]
    e = jnp.asarray(_head_expand())
    et = jnp.asarray(_head_expand().T.copy())
    bc_w = SSM_GROUPS * D_STATE
    xc3 = xc.reshape(b, 1, CONV_DIM)
    return pl.pallas_call(
        _ssd_step_kernel,
        grid=(b,),
        in_specs=[pl.BlockSpec((1, 1, D_INNER), lambda i: (i, 0, 0)),
                  pl.BlockSpec((1, 1, bc_w), lambda i: (i, 0, D_INNER // bc_w)),
                  pl.BlockSpec((1, 1, bc_w), lambda i: (i, 0, D_INNER // bc_w + 1)),
                  pl.BlockSpec((1, 1, LANES), lambda i: (i, 0, 0)),
                  _const_spec((1, LANES)), _const_spec((1, D_INNER)),
                  _const_spec((LANES, D_INNER)), _const_spec((D_INNER, LANES)),
                  pl.BlockSpec((1, D_INNER, D_STATE), lambda i: (i, 0, 0))],
        out_specs=[pl.BlockSpec((1, 1, D_INNER), lambda i: (i, 0, 0)),
                   pl.BlockSpec((1, D_INNER, D_STATE), lambda i: (i, 0, 0))],
        out_shape=[jax.ShapeDtypeStruct((b, 1, D_INNER), F32),
                   jax.ShapeDtypeStruct((b, D_INNER, D_STATE), F32)],
        compiler_params=_cparams("arbitrary"),
        name="ssd_step",
    )(xc3, xc3, xc3, dt.reshape(b, 1, LANES), alog, dsk, e, et, h0)


def _fox_decode_kernel(pt_ref, qrow_ref, kn_ref, vn_ref, lfn_ref, *refs):
    n = PAGES_PER_STEP
    k_refs, v_refs, lf_refs = refs[:n], refs[n:2 * n], refs[2 * n:3 * n]
    o_ref, m_s, l_s, r_s, acc_s, qbd_s = refs[3 * n:]
    j = pl.program_id(1)
    rows8 = lax.broadcasted_iota(jnp.int32, (N_HEADS, HEAD_DIM), 0)

    @pl.when(j == 0)
    def _():
        on_diag = (lax.broadcasted_iota(jnp.int32, (N_HEADS, ATT_W), 1) >> (HEAD_DIM.bit_length() - 1)
                   == lax.broadcasted_iota(jnp.int32, (N_HEADS, ATT_W), 0))
        qrow = qrow_ref[0] * (HEAD_DIM ** -0.5)
        qbd_s[...] = jnp.where(on_diag, jnp.broadcast_to(qrow, (N_HEADS, ATT_W)), 0.0).astype(BF16)
        m_s[...] = jnp.sum(jnp.where(on_diag, jnp.broadcast_to(qrow * kn_ref[0], (N_HEADS, ATT_W)), 0.0),
                           axis=1, keepdims=True)
        l_s[...] = jnp.ones_like(l_s)
        r_s[...] = lfn_ref[0]
        acc_s[...] = vn_ref[0]

    r = lax.broadcasted_iota(jnp.int32, (LANES, LANES), 0)
    c = lax.broadcasted_iota(jnp.int32, (LANES, LANES), 1)
    triu = (r <= c).astype(F32)
    run = r_s[...]
    bias = []
    for i in range(n):
        cs = _dot(lf_refs[i][...], triu, precision=HIGHEST)
        total = cs[:, PAGE_SIZE - 1:PAGE_SIZE]
        bias.append(run + total - cs)
        run = run + total
    r_s[...] = run
    k_all = jnp.concatenate([k_refs[i][...].reshape(ATT_W, PAGE_SIZE).astype(BF16) for i in range(n)], axis=1)
    s = _dot(qbd_s[...], k_all) + jnp.concatenate(bias, axis=1)
    m_old = m_s[...]
    m_new = jnp.maximum(m_old, jnp.max(s, axis=1, keepdims=True))
    alpha = jnp.exp(m_old - m_new)
    p = jnp.exp(s - m_new)
    l_s[...] = alpha * l_s[...] + jnp.sum(p, axis=1, keepdims=True)
    m_s[...] = m_new
    pb = p.astype(BF16)
    acc = acc_s[...] * alpha
    for h in range(N_HEADS):
        v_h = jnp.concatenate([v_refs[i][h].astype(BF16) for i in range(n)], axis=1)
        acc = acc + jnp.where(rows8 == h, _dot_nt(pb, v_h), 0.0)
    acc_s[...] = acc

    @pl.when(j == pl.num_programs(1) - 1)
    def _():
        o_ref[0] = acc_s[...] / l_s[...]


def _fox_decode(qrow, kn_row, v_new, lf_new, pool_k, pool_v, pool_lf, page_table):
    b, n_pages = page_table.shape
    n = PAGES_PER_STEP
    steps = n_pages // n
    row3 = lambda i, j, pt: (i, 0, 0)

    def page_spec(shape, off):
        nd = len(shape)
        return pl.BlockSpec((None, None) + shape,
                            lambda i, j, pt: (0, pt[i, n_pages - 1 - (j * n + off)]) + (0,) * nd)

    in_specs = ([pl.BlockSpec((1, 1, ATT_W), row3)] * 2 + [pl.BlockSpec((1, N_HEADS, HEAD_DIM), row3),
                                                          pl.BlockSpec((1, N_HEADS, 1), row3)]
                + [page_spec((N_HEADS, HEAD_DIM, PAGE_SIZE), i) for i in range(n)] * 2
                + [page_spec((N_HEADS, PAGE_SIZE), i) for i in range(n)])
    grid_spec = pltpu.PrefetchScalarGridSpec(
        num_scalar_prefetch=1, grid=(b, steps), in_specs=in_specs,
        out_specs=pl.BlockSpec((1, N_HEADS, HEAD_DIM), row3),
        scratch_shapes=[pltpu.VMEM((N_HEADS, 1), F32)] * 3 + [pltpu.VMEM((N_HEADS, HEAD_DIM), F32),
                                                              pltpu.VMEM((N_HEADS, ATT_W), BF16)])
    return pl.pallas_call(
        _fox_decode_kernel, grid_spec=grid_spec,
        out_shape=jax.ShapeDtypeStruct((b, N_HEADS, HEAD_DIM), F32),
        compiler_params=_cparams("arbitrary", "arbitrary"),
        name="fox_decode_attn",
    )(page_table, qrow, kn_row, v_new, lf_new, *([pool_k] * n), *([pool_v] * n), *([pool_lf] * n))


def _moba_gate_kernel(pt_ref, qcol_ref, *refs):
    n = PAGES_PER_STEP
    k_refs = refs[:n]
    idx_ref, gate_s, qb_s = refs[n:]
    j = pl.program_id(1)
    ppb = MOBA_BLOCK // PAGE_SIZE
    lane = lax.broadcasted_iota(jnp.int32, (N_HEADS, LANES), 1)

    @pl.when(j == 0)
    def _():
        gate_s[...] = jnp.full_like(gate_s, -jnp.inf)
        qb_s[...] = jnp.broadcast_to(qcol_ref[0], (ATT_W, PAGE_SIZE))

    qb = qb_s[...]
    sub = 8
    fold = (lax.broadcasted_iota(jnp.int32, (N_HEADS, N_HEADS * sub), 1) >> 3
            == lax.broadcasted_iota(jnp.int32, (N_HEADS, N_HEADS * sub), 0)).astype(F32)
    gates = gate_s[...]
    for blk in range(n // ppb):
        ksum = None
        for i in range(ppb):
            page = k_refs[blk * ppb + i][...].reshape(ATT_W, PAGE_SIZE)
            ksum = page if ksum is None else ksum + page
        part = jnp.sum((ksum * qb).reshape(N_HEADS, HEAD_DIM // sub, sub, PAGE_SIZE), axis=1)
        per_head = _dot(fold, part.reshape(N_HEADS * sub, PAGE_SIZE), precision=HIGHEST)
        g = jnp.sum(per_head, axis=1, keepdims=True) * (1.0 / MOBA_BLOCK)
        gates = jnp.where(lane == j * (n // ppb) + blk, g, gates)
    gate_s[...] = gates

    @pl.when(j == pl.num_programs(1) - 1)
    def _():
        gate = gate_s[...]
        lane_f = lane.astype(F32)
        picks = jnp.zeros((N_HEADS, LANES), F32)
        for k in range(MOBA_TOPK):
            best = jnp.max(gate, axis=1, keepdims=True)
            first = jnp.min(jnp.where(gate == best, lane_f, float(LANES)), axis=1, keepdims=True)
            picks = jnp.where(lane == k, first, picks)
            gate = jnp.where(lane_f == first, -jnp.inf, gate)
        idx_ref[0] = picks.astype(jnp.int32)


def _moba_gate(qcol, pool_k, page_table):
    b, n_pages = page_table.shape
    n = PAGES_PER_STEP
    steps = n_pages // n
    row3 = lambda i, j, pt: (i, 0, 0)
    in_specs = [pl.BlockSpec((1, ATT_W, 1), row3)] + [
        pl.BlockSpec((None, None, N_HEADS, HEAD_DIM, PAGE_SIZE),
                     functools.partial(lambda i, j, pt, off: (0, pt[i, j * n + off], 0, 0, 0), off=off))
        for off in range(n)]
    grid_spec = pltpu.PrefetchScalarGridSpec(
        num_scalar_prefetch=1, grid=(b, steps), in_specs=in_specs,
        out_specs=pl.BlockSpec((1, N_HEADS, LANES), row3),
        scratch_shapes=[pltpu.VMEM((N_HEADS, LANES), F32), pltpu.VMEM((ATT_W, PAGE_SIZE), F32)])
    return pl.pallas_call(
        _moba_gate_kernel, grid_spec=grid_spec,
        out_shape=jax.ShapeDtypeStruct((b, N_HEADS, LANES), jnp.int32),
        compiler_params=_cparams("arbitrary", "arbitrary"),
        name="moba_decode_gate",
    )(page_table, qcol, *([pool_k] * n))


def _moba_decode_kernel(pt_ref, idx_ref, q_ref, kn_ref, vn_ref, *refs):
    npg = MOBA_TOPK * (MOBA_BLOCK // PAGE_SIZE)
    k_refs, v_refs = refs[:npg], refs[npg:2 * npg]
    o_ref, acc_s = refs[2 * npg:]
    h = pl.program_id(1)
    q8 = q_ref[0] * (HEAD_DIM ** -0.5)
    rows8 = lax.broadcasted_iota(jnp.int32, (N_HEADS, HEAD_DIM), 0)
    qh = jnp.where(rows8 == h, q8, 0.0).astype(BF16)

    @pl.when(h == 0)
    def _():
        acc_s[...] = jnp.zeros_like(acc_s)

    k_h = jnp.concatenate([k_refs[i][h].astype(BF16) for i in range(npg)], axis=1)
    v_h = jnp.concatenate([v_refs[i][h].astype(BF16) for i in range(npg)], axis=1)
    s = _dot(qh, k_h)
    s_new = jnp.sum(q8 * kn_ref[0], axis=1, keepdims=True)
    m = jnp.maximum(jnp.max(s, axis=1, keepdims=True), s_new)
    p = jnp.exp(s - m)
    p_new = jnp.exp(s_new - m)
    denom = jnp.sum(p, axis=1, keepdims=True) + p_new
    out = (_dot_nt(p.astype(BF16), v_h) + p_new * vn_ref[0]) / denom
    acc_s[...] = jnp.where(rows8 == h, out, acc_s[...])

    @pl.when(h == pl.num_programs(1) - 1)
    def _():
        o_ref[0] = acc_s[...]


def _moba_decode(q, k_new, v_new, idx, pool_k, pool_v, page_table):
    b, _ = page_table.shape
    ppb = MOBA_BLOCK // PAGE_SIZE
    npg = MOBA_TOPK * ppb
    row3 = lambda i, h, pt, ix: (i, 0, 0)
    page = lambda k, off: pl.BlockSpec(
        (None, None, N_HEADS, HEAD_DIM, PAGE_SIZE),
        lambda i, h, pt, ix: (0, pt[i, ix[i, h * MOBA_TOPK + k] * ppb + off], 0, 0, 0))
    in_specs = ([pl.BlockSpec((1, N_HEADS, HEAD_DIM), row3)] * 3
                + [page(k, off) for k in range(MOBA_TOPK) for off in range(ppb)] * 2)
    grid_spec = pltpu.PrefetchScalarGridSpec(
        num_scalar_prefetch=2, grid=(b, N_HEADS), in_specs=in_specs,
        out_specs=pl.BlockSpec((1, N_HEADS, HEAD_DIM), row3),
        scratch_shapes=[pltpu.VMEM((N_HEADS, HEAD_DIM), F32)])
    return pl.pallas_call(
        _moba_decode_kernel, grid_spec=grid_spec,
        out_shape=jax.ShapeDtypeStruct((b, N_HEADS, HEAD_DIM), F32),
        compiler_params=_cparams("arbitrary", "arbitrary"),
        name="moba_decode_attn",
    )(page_table, idx, q, k_new, v_new, *([pool_k] * npg), *([pool_v] * npg))


def _pad_lanes(a, width=LANES):
    return jnp.pad(a, [(0, 0)] * (a.ndim - 1) + [(0, width - a.shape[-1])])


def _run_group(x, ada, weights, decode, caches=None):
    b, t, _ = x.shape
    m = b * t
    x2d = x.reshape(m, D_MODEL)
    if decode:
        tm = m
        mod_map = lambda i: (0, 0, 0)
        as_mod = lambda a: a.reshape(1, b, D_MODEL)
    else:
        tm = 512
        tps = t // tm
        mod_map = lambda i: (i // tps, 0, 0)
        as_mod = lambda a: a.reshape(b, 1, D_MODEL)
    row = lambda v: v.reshape(1, -1)

    sh1, sc1, gt1, sh2, sc2, gt2 = [as_mod(a) for a in jnp.split(ada[0], 6, axis=-1)]
    pos = (caches["page_table"].shape[1] * PAGE_SIZE if decode else 0) + np.arange(t)
    qa, qb, ka, va, kb, vb, lf = _attn_proj(
        x2d, sc1, sh1, row(weights["g_mix_pre"][0]), weights["wt_att6"], weights["wt_att_f"],
        weights["b_fox_f"], pos, tm, mod_map, t, prompt=not decode)
    if decode:
        hd = lambda a: a.reshape(b, N_HEADS, HEAD_DIM)
        pt = caches["page_table"]
        idx = _moba_gate(qa.reshape(b, ATT_W, 1), caches["moba_k"], pt)[:, :, :MOBA_TOPK]
        oa = _moba_decode(hd(qa), hd(ka), hd(va), idx.reshape(b, N_HEADS * MOBA_TOPK),
                          caches["moba_k"], caches["moba_v"], pt)
        ob = _fox_decode(qb.reshape(b, 1, ATT_W), kb.reshape(b, 1, ATT_W), hd(vb), lf.reshape(b, N_HEADS, 1),
                         caches["fox_k"], caches["fox_v"], caches["fox_lf"], pt)
        oa, ob = oa.reshape(m, ATT_W), ob.reshape(m, ATT_W)
        rows5 = lambda a: a.reshape(1, b, t, N_HEADS, HEAD_DIM)
        lf_out = lf.reshape(1, b, t, N_HEADS)
    else:
        ft, fcol = _fox_cumsum(lf, b, t)
        oa = _attn_prompt(qa, ka, va, b, t, fox=False)
        ob = _attn_prompt(qb, kb, vb, b, t, fox=True, ft=ft, fcol=fcol)
        rows5 = lambda a: a.reshape(1, b, N_HEADS, HEAD_DIM, t).transpose(0, 1, 4, 2, 3)
        lf_out = lf.reshape(1, b, N_HEADS, t).transpose(0, 1, 3, 2)
    x2d = _post(x2d, [oa, ob], weights["w_att_out"], (gt1, sc2, sh2, gt2),
                (row(weights["g_mix_post"][0]), row(weights["g_ffn_pre"][0]), row(weights["g_ffn_post"][0])),
                weights["w_ffn_in"][0], weights["w_ffn_out"][0], tm, mod_map, ssm=False)

    sh1, sc1, gt1, sh2, sc2, gt2 = [as_mod(a) for a in jnp.split(ada[1], 6, axis=-1)]
    prev = caches["state_conv"] if decode else None
    z, xc, dt, conv_state = _ssm_proj(
        x2d, sc1, sh1, row(weights["g_mix_pre"][1]), weights["wt_ssm_z"], weights["wt_ssm_x"],
        weights["wt_ssm_dt"], weights["dt_bias"], weights["conv_w"], weights["conv_b"], tm, mod_map, t, prev=prev)
    if decode:
        y, h_t = _ssd_step(xc, dt, weights["a_log"], weights["d_skip"], caches["state_ssm"])
        y = y.reshape(m, D_INNER)
    else:
        y, h_t = _ssd_prompt(xc, dt, weights["a_log"], weights["d_skip"], b, t)
    x2d = _post(x2d, [y, z], weights["w_ssm_out"], (gt1, sc2, sh2, gt2),
                (row(weights["g_mix_post"][1]), row(weights["g_ffn_pre"][1]), row(weights["g_ffn_post"][1])),
                weights["w_ffn_in"][1], weights["w_ffn_out"][1], tm, mod_map, ssm=True,
                gn=row(weights["g_ssm_norm"]))

    return (x2d.reshape(b, t, D_MODEL), rows5(ka), rows5(va), rows5(kb), rows5(vb), lf_out,
            h_t.reshape(1, b, SSM_HEADS, SSM_HEAD_DIM, D_STATE),
            conv_state.transpose(1, 0, 2).reshape(1, b, CONV_W - 1, CONV_DIM))


def kernel(x_prompt, x_sample, cache_moba_k, cache_moba_v, cache_fox_k, cache_fox_v, cache_fox_logf, state_ssm, state_conv, page_table, c_prompt, c_sample, w_ada, b_ada, g_mix_pre, g_mix_post, g_ffn_pre, g_ffn_post, w_att_in, b_fox_f, w_att_out, w_ssm_in, conv_w, conv_b, dt_bias, a_log, d_skip, g_ssm_norm, w_ssm_out, w_ffn_in, w_ffn_out):
    bp = x_prompt.shape[0]
    bs = x_sample.shape[0]
    ada = _ada(jnp.concatenate([c_prompt, c_sample], axis=0), w_ada, b_ada)
    wt_att = w_att_in[0].T
    wt_ssm = w_ssm_in[0].T
    n_dt = wt_ssm.shape[0] - D_INNER - CONV_DIM
    weights = dict(
        g_mix_pre=g_mix_pre, g_mix_post=g_mix_post, g_ffn_pre=g_ffn_pre, g_ffn_post=g_ffn_post,
        wt_att6=wt_att[:6 * ATT_W].astype(BF16),
        wt_att_f=wt_att[6 * ATT_W:],
        b_fox_f=b_fox_f[0],
        w_att_out=w_att_out[0].astype(BF16),
        wt_ssm_z=wt_ssm[:D_INNER].astype(BF16),
        wt_ssm_x=wt_ssm[D_INNER:D_INNER + CONV_DIM].astype(BF16),
        wt_ssm_dt=jnp.pad(wt_ssm[D_INNER + CONV_DIM:], ((0, LANES - n_dt), (0, 0))).astype(BF16),
        dt_bias=_pad_lanes(dt_bias[0].reshape(1, -1)),
        conv_w=conv_w[0], conv_b=conv_b[0].reshape(1, -1),
        a_log=_pad_lanes(a_log[0].reshape(1, -1)),
        d_skip=jnp.repeat(d_skip[0], SSM_HEAD_DIM).reshape(1, -1),
        g_ssm_norm=g_ssm_norm[0],
        w_ssm_out=w_ssm_out[0].astype(BF16),
        w_ffn_in=w_ffn_in.astype(BF16), w_ffn_out=w_ffn_out.astype(BF16),
    )
    kv_t = lambda a: a.transpose(0, 1, 3, 4, 2)
    caches = dict(page_table=page_table, moba_k=kv_t(cache_moba_k), moba_v=kv_t(cache_moba_v),
                  fox_k=kv_t(cache_fox_k), fox_v=kv_t(cache_fox_v),
                  fox_lf=cache_fox_logf.transpose(0, 1, 3, 2),
                  state_ssm=state_ssm[0].reshape(bs, D_INNER, D_STATE),
                  state_conv=state_conv[0].transpose(1, 0, 2))
    prompt = _run_group(x_prompt, ada[:, :bp], weights, decode=False)
    sample = _run_group(x_sample, ada[:, bp:], weights, decode=True, caches=caches)
    return (prompt[0], sample[0]) + prompt[1:] + sample[1:]
```

```python
import functools
import math

import numpy as np
import jax
import jax.numpy as jnp
from jax import lax
from jax.experimental import pallas as pl
from jax.experimental.pallas import tpu as pltpu

F32 = jnp.float32
BF16 = jnp.bfloat16
HIGHEST = lax.Precision.HIGHEST

D_MODEL = 1024
HEAD_DIM = 64
N_HEADS = 8
ATT_W = N_HEADS * HEAD_DIM
ROT_DIM = HEAD_DIM // 4
ROPE_THETA = 500000.0
MOBA_BLOCK = 256
MOBA_TOPK = 3
PAGE_SIZE = 128
D_INNER = 2 * D_MODEL
SSM_HEAD_DIM = 64
SSM_HEADS = D_INNER // SSM_HEAD_DIM
SSM_GROUPS = 4
SSM_HPG = SSM_HEADS // SSM_GROUPS
D_STATE = 128
CONV_W = 4
CONV_DIM = D_INNER + 2 * SSM_GROUPS * D_STATE
SSD_CHUNK = 128
D_FF = ((8 * D_MODEL + 3 * 256 - 1) // (3 * 256)) * 256
FFN_CHUNK = D_FF // 2
EPS = 1e-6
LANES = 128
VMEM_LIMIT = 56 * 1024 * 1024
PAGES_PER_STEP = 16
LOG2E = math.log2(math.e)
NEG = -1e30

_NT = (((1,), (1,)), ((), ()))


def _cparams(*sem):
    return pltpu.CompilerParams(dimension_semantics=sem, vmem_limit_bytes=VMEM_LIMIT)


def _const_spec(shape):
    nd = len(shape)
    return pl.BlockSpec(shape, lambda *_: (0,) * nd, pipeline_mode=pl.Buffered(1))


def _rms(x, g):
    return x * lax.rsqrt(jnp.mean(x * x, axis=-1, keepdims=True) + EPS) * g


def _silu(x):
    return x * jax.nn.sigmoid(x)


def _softplus(x):
    return jnp.maximum(x, 0.0) + jnp.log1p(jnp.exp(-jnp.abs(x)))


def _log_sigmoid(x):
    return jnp.minimum(x, 0.0) - jnp.log1p(jnp.exp(-jnp.abs(x)))


def _dot(a, b, **kw):
    return jnp.dot(a, b, preferred_element_type=F32, **kw)


def _dot_nt(a, b, **kw):
    return lax.dot_general(a, b, _NT, preferred_element_type=F32, **kw)


def _split3(x):
    hi = x.astype(BF16).astype(F32)
    r = x - hi
    mid = r.astype(BF16).astype(F32)
    return hi, mid, r - mid


def _ada_kernel(c_ref, w_ref, b_ref, o_ref):
    a = _silu(c_ref[...]).astype(BF16)
    o_ref[0] = _dot(a, w_ref[0].astype(BF16)) + b_ref[0]


def _ada(c_all, w_ada, b_ada):
    depth, _, n = w_ada.shape
    rows = c_all.shape[0]
    tn = 1024
    return pl.pallas_call(
        _ada_kernel,
        grid=(depth, n // tn),
        in_specs=[pl.BlockSpec((rows, D_MODEL), lambda l, j: (0, 0)),
                  pl.BlockSpec((1, D_MODEL, tn), lambda l, j: (l, 0, j)),
                  pl.BlockSpec((1, 1, tn), lambda l, j: (l, 0, j))],
        out_specs=pl.BlockSpec((1, rows, tn), lambda l, j: (l, 0, j)),
        out_shape=jax.ShapeDtypeStruct((depth, rows, n), F32),
        compiler_params=_cparams("arbitrary", "arbitrary"),
        name="ada_terms",
    )(c_all, w_ada, b_ada.reshape(depth, 1, n))


def _rope_tables(pos):
    half = ROT_DIM // 2
    inv = ROPE_THETA ** (-2.0 * np.arange(half, dtype=np.float64) / ROT_DIM)
    ang = np.asarray(pos, np.float64)[:, None] * inv[None, :]
    cos, sin = np.cos(ang), np.sin(ang)
    n = len(pos)
    c64 = np.concatenate([cos, cos, np.ones((n, HEAD_DIM - ROT_DIM))], axis=1)
    s1 = np.concatenate([-sin, np.zeros((n, HEAD_DIM - half))], axis=1)
    s2 = np.concatenate([np.zeros((n, half)), sin, np.zeros((n, HEAD_DIM - ROT_DIM))], axis=1)
    return [np.tile(t, (1, N_HEADS)).astype(np.float32) for t in (c64, s1, s2)]


def _rope(z, cos, s1, s2, axis):
    half = ROT_DIM // 2
    return z * cos + pltpu.roll(z, ATT_W - half, axis) * s1 + pltpu.roll(z, half, axis) * s2


def _attn_proj_kernel(*refs, prompt):
    x_ref, sc_ref, sh_ref, g_ref, wt_ref, wtf_ref, bf_ref, cos_ref, s1_ref, s2_ref = refs[:10]
    if prompt:
        cost_ref, s1t_ref, s2t_ref = refs[10:13]
        qa_ref, qb_ref, ka_ref, va_ref, kb_ref, vb_ref, lf_ref = refs[13:]
    else:
        qa_ref, qb_ref, ka_ref, va_ref, kb_ref, vb_ref, lf_ref = refs[10:]
    h = (_rms(x_ref[...], g_ref[...]) * (1.0 + sc_ref[0]) + sh_ref[0]).astype(BF16)
    w = lambda n: wt_ref[n * ATT_W:(n + 1) * ATT_W, :]
    qa_ref[...] = _rope(_dot_nt(h, w(0)), cos_ref[0], s1_ref[0], s2_ref[0], 1)
    qb_ref[...] = _dot_nt(h, w(3))
    wtf = wtf_ref[...].astype(BF16)
    if prompt:
        ka_ref[0] = _rope(_dot_nt(w(1), h), cost_ref[...], s1t_ref[...], s2t_ref[...], 0)
        va_ref[0] = _dot_nt(w(2), h)
        kb_ref[0] = _dot_nt(w(4), h)
        vb_ref[0] = _dot_nt(w(5), h)
        lf_ref[0] = _log_sigmoid(_dot_nt(wtf, h) + bf_ref[...])
    else:
        ka_ref[...] = _rope(_dot_nt(h, w(1)), cos_ref[0], s1_ref[0], s2_ref[0], 1)
        va_ref[...] = _dot_nt(h, w(2))
        kb_ref[...] = _dot_nt(h, w(4))
        vb_ref[...] = _dot_nt(h, w(5))
        lf_ref[...] = _log_sigmoid(_dot_nt(h, wtf) + bf_ref[...])


def _attn_proj(x2d, sc, sh, g, wt6, wtf, bf, pos, tm, mod_map, seq_len, prompt):
    m = x2d.shape[0]
    row = lambda i: (i, 0)
    mod_spec = pl.BlockSpec((1,) + sc.shape[1:], mod_map)
    tabs = _rope_tables(pos)
    wide = jax.ShapeDtypeStruct((m, ATT_W), F32)
    in_specs = [pl.BlockSpec((tm, D_MODEL), row), mod_spec, mod_spec, _const_spec((1, D_MODEL)),
                _const_spec(wt6.shape), _const_spec(wtf.shape)]
    if prompt:
        tps = seq_len // tm
        nseq = m // seq_len
        tab_spec = pl.BlockSpec((1, tm, ATT_W), lambda i: (i % tps, 0, 0))
        tabt_spec = pl.BlockSpec((ATT_W, tm), lambda i: (0, i % tps))
        fm_spec = pl.BlockSpec((1, ATT_W, tm), lambda i: (i // tps, 0, i % tps))
        fm = jax.ShapeDtypeStruct((nseq, ATT_W, seq_len), F32)
        in_specs += [_const_spec((N_HEADS, 1))] + [tab_spec] * 3 + [tabt_spec] * 3
        args = ([jnp.asarray(tb.reshape(tps, tm, ATT_W)) for tb in tabs]
                + [jnp.asarray(np.ascontiguousarray(tb.T)) for tb in tabs])
        out_specs = [pl.BlockSpec((tm, ATT_W), row)] * 2 + [fm_spec] * 4 + [
            pl.BlockSpec((1, N_HEADS, tm), lambda i: (i // tps, 0, i % tps))]
        out_shape = [wide] * 2 + [fm] * 4 + [jax.ShapeDtypeStruct((nseq, N_HEADS, seq_len), F32)]
        bf = bf.reshape(N_HEADS, 1)
    else:
        in_specs += [_const_spec((1, N_HEADS))] + [_const_spec((1, 1, ATT_W))] * 3
        args = [jnp.asarray(tb.reshape(1, 1, ATT_W)) for tb in tabs]
        out_specs = [pl.BlockSpec((tm, ATT_W), row)] * 6 + [pl.BlockSpec((tm, N_HEADS), row)]
        out_shape = [wide] * 6 + [jax.ShapeDtypeStruct((m, N_HEADS), F32)]
        bf = bf.reshape(1, N_HEADS)
    return pl.pallas_call(
        functools.partial(_attn_proj_kernel, prompt=prompt),
        grid=(m // tm,),
        in_specs=in_specs,
        out_specs=out_specs,
        out_shape=out_shape,
        compiler_params=_cparams("arbitrary"),
        name="attn_in_proj" if prompt else "attn_in_proj_decode",
    )(x2d, sc, sh, g, wt6, wtf, bf, *args)


def _cumsum_kernel(lf_ref, ft_ref, fcol_ref):
    t = lf_ref.shape[2]
    r = lax.broadcasted_iota(jnp.int32, (LANES, LANES), 0)
    c = lax.broadcasted_iota(jnp.int32, (LANES, LANES), 1)
    triu = (r <= c).astype(F32)
    carry = jnp.zeros((N_HEADS, 1), F32)
    pad = jnp.zeros((LANES - N_HEADS, LANES), F32)
    for i in range(t // LANES):
        cols = slice(i * LANES, (i + 1) * LANES)
        blk = _dot(lf_ref[0, :, cols], triu, precision=HIGHEST) + carry
        carry = blk[:, LANES - 1:LANES]
        ft_ref[0, :, cols] = blk
        fcol_ref[cols, :] = jnp.concatenate([blk, pad], axis=0).T


def _fox_cumsum(lf_t, b, t):
    return pl.pallas_call(
        _cumsum_kernel,
        grid=(b,),
        in_specs=[pl.BlockSpec((1, N_HEADS, t), lambda i: (i, 0, 0))],
        out_specs=[pl.BlockSpec((1, N_HEADS, t), lambda i: (i, 0, 0)),
                   pl.BlockSpec((t, LANES), lambda i: (i, 0))],
        out_shape=[jax.ShapeDtypeStruct((b, N_HEADS, t), F32), jax.ShapeDtypeStruct((b * t, LANES), F32)],
        compiler_params=_cparams("arbitrary"),
        name="fox_cumsum",
    )(lf_t)


def _attn_prompt_kernel(*refs, fox, t):
    if fox:
        q_ref, k_ref, v_ref, fcol_ref, ft_ref, o_ref, kaug, vaug = refs
    else:
        q_ref, k_ref, v_ref, o_ref, kaug, vaug = refs
    blk = MOBA_BLOCK
    nb = t // blk
    pair = pl.program_id(1)
    c_exp = (HEAD_DIM ** -0.5) * LOG2E
    row_t = lax.broadcasted_iota(jnp.int32, (LANES, t), 0)
    low_t = row_t < HEAD_DIM
    r64 = row_t & (HEAD_DIM - 1)
    kf = k_ref[0]
    vf = v_ref[0]
    if fox:
        augs = []
        for hd in range(2):
            hi, mid, lo = _split3(ft_ref[0, pl.ds(2 * pair + hd, 1), :] * LOG2E)
            augs.append(jnp.where(r64 < 3, 1.0, jnp.where(r64 == 3, -hi, jnp.where(
                r64 == 4, -mid, jnp.where(r64 == 5, -lo, 0.0)))))
        kaug[0] = jnp.where(low_t, kf, augs[0]).astype(BF16)
        kaug[1] = jnp.where(low_t, augs[1], kf).astype(BF16)
    else:
        lane_t = lax.broadcasted_iota(jnp.int32, (LANES, t), 1)
        key_blk = lane_t >> (blk.bit_length() - 1)
        ind = jnp.where((key_blk == r64) & (r64 < nb), 1.0, 0.0)
        kaug[0] = jnp.where(low_t, kf, ind).astype(BF16)
        kaug[1] = jnp.where(low_t, ind, kf).astype(BF16)
        hs = 8 * pl.cdiv(nb, 8)
        l128 = lax.broadcasted_iota(jnp.int32, (LANES, LANES), 1)
        r128 = lax.broadcasted_iota(jnp.int32, (LANES, LANES), 0)
        kcols = jnp.zeros((LANES, LANES), F32)
        for n in range(nb):
            col = jnp.mean(kf[:, n * blk:(n + 1) * blk], axis=1, keepdims=True)
            kcols = jnp.where((l128 == n) | (l128 == hs + n), col, kcols)
        kmean = jnp.where(((r128 < hs) & (l128 < HEAD_DIM)) | ((r128 >= hs) & (l128 >= HEAD_DIM)), kcols.T, 0.0)
        blk_id = lax.broadcasted_iota(jnp.int32, (hs, blk), 0)
        fill = jnp.zeros((HEAD_DIM - hs, blk), F32)
    vaug[0] = jnp.where(low_t, vf, jnp.where(row_t == HEAD_DIM, 1.0, 0.0)).astype(BF16)
    vaug[1] = jnp.where(low_t, jnp.where(row_t == 0, 1.0, 0.0), vf).astype(BF16)

    lane = lax.broadcasted_iota(jnp.int32, (blk, LANES), 1)
    low = lane < HEAD_DIM
    l64 = lane & (HEAD_DIM - 1)
    causal = (lax.broadcasted_iota(jnp.int32, (blk, blk), 1) <= lax.broadcasted_iota(jnp.int32, (blk, blk), 0))

    def q_block(qi, carry):
        q0 = pl.multiple_of(qi * blk, blk)
        q = q_ref[pl.ds(q0, blk), :]
        if fox:
            fq = fcol_ref[pl.ds(q0, blk), :] * LOG2E
            parts = []
            for hd in range(2):
                col = jnp.sum(jnp.where(lane == 2 * pair + hd, fq, 0.0), axis=1, keepdims=True)
                hi, mid, lo = _split3(col)
                parts.append(jnp.where(l64 == 0, hi, jnp.where(l64 == 1, mid, jnp.where(
                    l64 == 2, lo, jnp.where(l64 < 6, 1.0, 0.0)))))
            aug = jnp.where(low, parts[1], parts[0])
        else:
            gate_t = _dot_nt(kmean, q, precision=HIGHEST)
            aug_t = []
            for hd in range(2):
                gate = gate_t[hd * hs:(hd + 1) * hs, :]
                beaten = jnp.zeros((hs, blk), F32)
                for n2 in range(nb):
                    g2 = gate[n2:n2 + 1, :]
                    better = (g2 > gate) | ((g2 == gate) & (n2 < blk_id))
                    beaten = beaten + jnp.where(better, 1.0, 0.0) * jnp.where(n2 < qi, 1.0, 0.0)
                chosen = ((beaten < MOBA_TOPK) & (blk_id < qi)) | (blk_id == qi)
                aug_t.append(jnp.where((blk_id < nb) & jnp.logical_not(chosen), NEG, 0.0))
            aug = jnp.concatenate([aug_t[1], fill, aug_t[0], fill], axis=0).T
        qc = q * c_exp
        qaug = (jnp.where(low, qc, aug).astype(BF16), jnp.where(low, aug, qc).astype(BF16))

        def step(kb, width, state, diag):
            k0 = pl.multiple_of(kb * blk, blk * width)
            keys = pl.ds(k0, blk * width)
            new = []
            for hd in range(2):
                m, acc = state[hd]
                s = _dot(qaug[hd], kaug[hd, :, keys])
                if diag:
                    s = jnp.where(causal, s, NEG)
                m_new = jnp.maximum(m, jnp.max(s, axis=1, keepdims=True))
                p = jnp.exp2(s - m_new).astype(BF16)
                acc = jnp.exp2(m - m_new) * acc + _dot_nt(p, vaug[hd, :, keys])
                new.append((m_new, acc))
            return tuple(new)

        init = tuple((jnp.full((blk, 1), NEG, F32), jnp.zeros((blk, LANES), F32)) for _ in range(2))
        st = step(qi, 1, init, True)
        st = lax.cond(qi % 2 == 1, lambda s_: step(qi - 1, 1, s_, False), lambda s_: s_, st)
        st = lax.fori_loop(0, qi // 2, lambda j, s_: step(2 * j, 2, s_, False), st)
        outs = []
        for hd in range(2):
            acc = st[hd][1]
            denom = jnp.sum(jnp.where(lane == (HEAD_DIM if hd == 0 else 0), acc, 0.0), axis=1, keepdims=True)
            outs.append(acc / denom)
        o_ref[pl.ds(q0, blk), :] = jnp.where(low, outs[0], outs[1])
        return carry

    lax.fori_loop(0, nb, q_block, 0)


def _attn_prompt(q, k_t, v_t, b, t, fox, ft=None, fcol=None):
    pairs = ATT_W // LANES
    slab = pl.BlockSpec((t, LANES), lambda i, p: (i, p))
    slab_t = pl.BlockSpec((1, LANES, t), lambda i, p: (i, p, 0))
    in_specs = [slab, slab_t, slab_t]
    args = [q, k_t, v_t]
    if fox:
        in_specs += [pl.BlockSpec((t, LANES), lambda i, p: (i, 0)),
                     pl.BlockSpec((1, N_HEADS, t), lambda i, p: (i, 0, 0))]
        args += [fcol, ft]
    return pl.pallas_call(
        functools.partial(_attn_prompt_kernel, fox=fox, t=t),
        grid=(b, pairs),
        in_specs=in_specs,
        out_specs=slab,
        out_shape=jax.ShapeDtypeStruct((b * t, ATT_W), F32),
        scratch_shapes=[pltpu.VMEM((2, LANES, t), BF16), pltpu.VMEM((2, LANES, t), BF16)],
        compiler_params=_cparams("arbitrary", "arbitrary"),
        name="fox_prompt_attn" if fox else "moba_prompt_attn",
    )(*args)


def _post_kernel(*refs, ssm):
    if ssm:
        x_ref, y_ref, z_ref, gn_ref, wmix_ref = refs[:5]
        rest = refs[5:]
    else:
        x_ref, oa_ref, ob_ref, wmix_ref = refs[:4]
        rest = refs[4:]
    gt1_ref, gpost_ref, gpre_ref, sc2_ref, sh2_ref, gt2_ref, gfpost_ref, win_ref, wout_ref, o_ref = rest
    if ssm:
        y = y_ref[...] * _silu(z_ref[...])
        gs = D_INNER // SSM_GROUPS
        parts = []
        for g in range(SSM_GROUPS):
            yg = y[:, g * gs:(g + 1) * gs]
            parts.append(yg * lax.rsqrt(jnp.mean(yg * yg, axis=-1, keepdims=True) + EPS))
        yn = (jnp.concatenate(parts, axis=-1) * gn_ref[...]).astype(BF16)
        mix = _dot(yn, wmix_ref[...])
    else:
        mix = (_dot(oa_ref[...].astype(BF16), wmix_ref[0:ATT_W, :])
               + _dot(ob_ref[...].astype(BF16), wmix_ref[ATT_W:2 * ATT_W, :]))
    x1 = x_ref[...] + gt1_ref[0] * _rms(mix, gpost_ref[...])
    h2 = (_rms(x1, gpre_ref[...]) * (1.0 + sc2_ref[0]) + sh2_ref[0]).astype(BF16)
    acc = None
    for c in range(D_FF // FFN_CHUNK):
        lo = c * FFN_CHUNK
        gate = _dot(h2, win_ref[:, lo:lo + FFN_CHUNK])
        up = _dot(h2, win_ref[:, D_FF + lo:D_FF + lo + FFN_CHUNK])
        part = _dot((_silu(gate) * up).astype(BF16), wout_ref[lo:lo + FFN_CHUNK, :])
        acc = part if acc is None else acc + part
    o_ref[...] = x1 + gt2_ref[0] * _rms(acc, gfpost_ref[...])


def _post(x2d, mix_in, wmix, mods, gains, win, wout, tm, mod_map, ssm, gn=None):
    m = x2d.shape[0]
    row = lambda i: (i, 0)
    gt1, sc2, sh2, gt2 = mods
    gpost, gpre, gfpost = gains
    mod_spec = pl.BlockSpec((1,) + gt1.shape[1:], mod_map)
    vec = _const_spec((1, D_MODEL))
    in_specs = [pl.BlockSpec((tm, D_MODEL), row)]
    args = [x2d]
    for a in mix_in:
        in_specs.append(pl.BlockSpec((tm, a.shape[1]), row))
        args.append(a)
    if ssm:
        in_specs.append(_const_spec((1, D_INNER)))
        args.append(gn)
    in_specs += [_const_spec(wmix.shape), mod_spec, vec, vec, mod_spec, mod_spec, mod_spec, vec,
                 _const_spec(win.shape), _const_spec(wout.shape)]
    args += [wmix, gt1, gpost, gpre, sc2, sh2, gt2, gfpost, win, wout]
    return pl.pallas_call(
        functools.partial(_post_kernel, ssm=ssm),
        grid=(m // tm,),
        in_specs=in_specs,
        out_specs=pl.BlockSpec((tm, D_MODEL), row),
        out_shape=jax.ShapeDtypeStruct((m, D_MODEL), F32),
        compiler_params=_cparams("arbitrary"),
        name="ssm_out_ffn" if ssm else "attn_out_ffn",
    )(*args)


def _ssm_proj_kernel(*refs, decode, tiles_per_seq):
    if decode:
        (x_ref, sc_ref, sh_ref, g_ref, wz_ref, wx_ref, wdt_ref, dtb_ref, cw_ref, cb_ref, prev_ref,
         z_ref, xc_ref, dt_ref, cs_ref) = refs
    else:
        (x_ref, sc_ref, sh_ref, g_ref, wz_ref, wx_ref, wdt_ref, dtb_ref, cw_ref, cb_ref,
         z_ref, xc_ref, dt_ref, cs_ref, ubuf) = refs
    tm = x_ref.shape[0]
    h = (_rms(x_ref[...], g_ref[...]) * (1.0 + sc_ref[0]) + sh_ref[0]).astype(BF16)
    z_ref[...] = _dot_nt(h, wz_ref[...])
    dt_ref[...] = _softplus(_dot_nt(h, wdt_ref[...]) + dtb_ref[...])
    if decode:
        u = _dot_nt(h, wx_ref[...])
        w = [cw_ref[j:j + 1, :] for j in range(CONV_W)]
        p0, p1, p2 = prev_ref[0], prev_ref[1], prev_ref[2]
        y = cb_ref[...] + w[3] * u + w[2] * p2 + w[1] * p1 + w[0] * p0
        cs_ref[0] = p1
        cs_ref[1] = p2
        cs_ref[2] = u
        xc_ref[...] = _silu(y)
    else:
        pad = 8
        tail = CONV_W - 1
        @pl.when(pl.program_id(0) % tiles_per_seq == 0)
        def _():
            ubuf[0:pad, :] = jnp.zeros((pad, CONV_DIM), F32)
        cw = CONV_DIM // 6
        for c in range(CONV_DIM // cw):
            cols = slice(c * cw, (c + 1) * cw)
            u = _dot_nt(h, wx_ref[cols, :])
            ubuf[pad:pad + tm, cols] = u
            y = cb_ref[:, cols] + cw_ref[tail:CONV_W, cols] * u
            for j in range(tail):
                y = y + cw_ref[j:j + 1, cols] * ubuf[pad - tail + j:pad - tail + j + tm, cols]
            xc_ref[:, cols] = _silu(y)
        cs_ref[:, 0, 0, :] = ubuf[pad + tm - tail:pad + tm, :]
        ubuf[0:pad, :] = ubuf[tm:tm + pad, :]


def _ssm_proj(x2d, sc, sh, g, wz, wx, wdt, dtb, cw, cb, tm, mod_map, seq_len, prev=None):
    m = x2d.shape[0]
    decode = prev is not None
    row = lambda i: (i, 0)
    mod_spec = pl.BlockSpec((1,) + sc.shape[1:], mod_map)
    in_specs = [pl.BlockSpec((tm, D_MODEL), row), mod_spec, mod_spec, _const_spec((1, D_MODEL)),
                _const_spec(wz.shape), _const_spec(wx.shape), _const_spec(wdt.shape), _const_spec((1, LANES)),
                _const_spec(cw.shape), _const_spec((1, CONV_DIM))]
    args = [x2d, sc, sh, g, wz, wx, wdt, dtb, cw, cb]
    scratch = []
    if decode:
        in_specs.append(_const_spec(prev.shape))
        args.append(prev)
        cs_shape = prev.shape
        cs_spec = _const_spec(prev.shape)
        tiles_per_seq = 1
    else:
        tiles_per_seq = seq_len // tm
        nseq = m // seq_len
        cs_shape = (CONV_W - 1, nseq, 1, CONV_DIM)
        cs_spec = pl.BlockSpec((CONV_W - 1, 1, 1, CONV_DIM), lambda i: (0, i // tiles_per_seq, 0, 0))
        scratch = [pltpu.VMEM((tm + 8, CONV_DIM), F32)]
    z, xc, dt, cs = pl.pallas_call(
        functools.partial(_ssm_proj_kernel, decode=decode, tiles_per_seq=tiles_per_seq),
        grid=(m // tm,),
        in_specs=in_specs,
        out_specs=[pl.BlockSpec((tm, D_INNER), row), pl.BlockSpec((tm, CONV_DIM), row),
                   pl.BlockSpec((tm, LANES), row), cs_spec],
        out_shape=[jax.ShapeDtypeStruct((m, D_INNER), F32), jax.ShapeDtypeStruct((m, CONV_DIM), F32),
                   jax.ShapeDtypeStruct((m, LANES), F32), jax.ShapeDtypeStruct(cs_shape, F32)],
        scratch_shapes=scratch,
        compiler_params=_cparams("arbitrary"),
        name="ssm_in_proj_decode" if decode else "ssm_in_proj",
    )(*args)
    return z, xc, dt, cs.reshape(CONV_W - 1, -1, CONV_DIM)


def _head_expand():
    e = np.zeros((LANES, D_INNER), np.float32)
    for hh in range(SSM_HEADS):
        e[hh, hh * SSM_HEAD_DIM:(hh + 1) * SSM_HEAD_DIM] = 1.0
    return e


def _spread(x, e):
    hi, mid, lo = _split3(x)
    return _dot(hi.astype(BF16), e) + _dot(mid.astype(BF16), e) + _dot(lo.astype(BF16), e)


def _spread_rows(e, x):
    hi, mid, lo = _split3(x)
    return _dot(e, hi.astype(BF16)) + _dot(e, mid.astype(BF16)) + _dot(e, lo.astype(BF16))


def _ssd_kernel(xs_ref, b_ref, c_ref, dt_ref, alog_ref, dsk_ref, e_ref, et_ref, y_ref, ht_ref, hst):
    q = SSD_CHUNK
    gw = D_INNER // SSM_GROUPS
    ci = pl.program_id(1)

    @pl.when(ci == 0)
    def _():
        hst[...] = jnp.zeros_like(hst)

    r = lax.broadcasted_iota(jnp.int32, (q, q), 0)
    c = lax.broadcasted_iota(jnp.int32, (q, q), 1)
    causal = r >= c
    lane = lax.broadcasted_iota(jnp.int32, (q, LANES), 1)
    first_half = lane < SSM_HEAD_DIM
    dt = dt_ref[...]
    a = -jnp.exp(alog_ref[...])
    acum = _dot(causal.astype(F32), dt * a, precision=HIGHEST)
    acum_t = acum.T
    dt_t = dt.T
    a_last = acum[q - 1:q, :]
    expand = e_ref[...]
    w_full = _spread(jnp.exp(a_last - acum) * dt, expand)
    ea_full = _spread(jnp.exp(acum), expand)
    dec_rows = _spread_rows(et_ref[...], jnp.exp(jnp.broadcast_to(acum_t[:, q - 1:q], (LANES, LANES))))
    xs = xs_ref[...]
    xs_bf = xs.astype(BF16)
    xw = xs * w_full
    dsk = dsk_ref[...]
    for g in range(SSM_GROUPS):
        bg = b_ref[:, g * D_STATE:(g + 1) * D_STATE].astype(BF16)
        cg = c_ref[:, g * D_STATE:(g + 1) * D_STATE].astype(BF16)
        cb = _dot_nt(cg, bg)
        rows = slice(g * gw, (g + 1) * gw)
        h_in = hst[rows, :]
        y_inter = _dot_nt(cg, h_in.astype(BF16))
        parts = []
        for pr in range(SSM_HPG // 2):
            h0 = g * SSM_HPG + 2 * pr
            x_pair = xs_bf[:, h0 * SSM_HEAD_DIM:(h0 + 2) * SSM_HEAD_DIM]
            outs = []
            for hh in (h0, h0 + 1):
                seg = acum[:, hh:hh + 1] - acum_t[hh:hh + 1, :]
                mh = cb * jnp.exp(jnp.where(causal, seg, -jnp.inf)) * dt_t[hh:hh + 1, :]
                outs.append(_dot(mh.astype(BF16), x_pair))
            parts.append(jnp.where(first_half, outs[0], outs[1]))
        y_intra = jnp.concatenate(parts, axis=-1)
        y_ref[:, rows] = y_intra + y_inter * ea_full[:, rows] + dsk[:, rows] * xs[:, rows]
        s_inc = _dot(xw[:, rows].T.astype(BF16), bg)
        hst[rows, :] = h_in * dec_rows[rows, :] + s_inc

    @pl.when(ci == pl.num_programs(1) - 1)
    def _():
        ht_ref[0] = hst[...]


def _ssd_prompt(xc, dt, alog, dsk, b, t):
    nc = t // SSD_CHUNK
    e = jnp.asarray(_head_expand(), BF16)
    et = jnp.asarray(_head_expand().T.copy(), BF16)
    rowmap = lambda i, c: (i * nc + c, 0)
    bc_w = SSM_GROUPS * D_STATE
    return pl.pallas_call(
        _ssd_kernel,
        grid=(b, nc),
        in_specs=[pl.BlockSpec((SSD_CHUNK, D_INNER), rowmap),
                  pl.BlockSpec((SSD_CHUNK, bc_w), lambda i, c: (i * nc + c, D_INNER // bc_w)),
                  pl.BlockSpec((SSD_CHUNK, bc_w), lambda i, c: (i * nc + c, D_INNER // bc_w + 1)),
                  pl.BlockSpec((SSD_CHUNK, LANES), rowmap),
                  _const_spec((1, LANES)), _const_spec((1, D_INNER)),
                  _const_spec((LANES, D_INNER)), _const_spec((D_INNER, LANES))],
        out_specs=[pl.BlockSpec((SSD_CHUNK, D_INNER), rowmap),
                   pl.BlockSpec((1, D_INNER, D_STATE), lambda i, c: (i, 0, 0))],
        out_shape=[jax.ShapeDtypeStruct((b * t, D_INNER), F32),
                   jax.ShapeDtypeStruct((b, D_INNER, D_STATE), F32)],
        scratch_shapes=[pltpu.VMEM((D_INNER, D_STATE), F32)],
        compiler_params=_cparams("arbitrary", "arbitrary"),
        name="ssd_scan",
    )(xc, xc, xc, dt, alog, dsk, e, et)


def _ssd_step_kernel(xs_ref, b_ref, c_ref, dt_ref, alog_ref, dsk_ref, e_ref, et_ref, h0_ref, y_ref, ht_ref):
    gw = D_INNER // SSM_GROUPS
    dt = dt_ref[0]
    a = -jnp.exp(alog_ref[...])
    dec = jnp.exp(dt * a)
    expand = e_ref[...]
    eye = (lax.broadcasted_iota(jnp.int32, (LANES, LANES), 0)
           == lax.broadcasted_iota(jnp.int32, (LANES, LANES), 1))
    dec_col = jnp.sum(jnp.where(eye, jnp.broadcast_to(dec, (LANES, LANES)), 0.0), axis=1, keepdims=True)
    dec_rows = _spread_rows(et_ref[...], jnp.broadcast_to(dec_col, (LANES, LANES)))
    xs = xs_ref[0]
    dtx = xs * _spread(dt, expand)
    eye_g = (lax.broadcasted_iota(jnp.int32, (gw, gw), 0) == lax.broadcasted_iota(jnp.int32, (gw, gw), 1))
    bm = b_ref[0]
    cm = c_ref[0]
    ys = []
    for g in range(SSM_GROUPS):
        rows = slice(g * gw, (g + 1) * gw)
        bg = bm[:, g * D_STATE:(g + 1) * D_STATE]
        cg = cm[:, g * D_STATE:(g + 1) * D_STATE]
        diag = jnp.where(eye_g, jnp.broadcast_to(dtx[:, rows], (gw, gw)), 0.0).astype(BF16)
        outer = _dot(diag, jnp.broadcast_to(bg, (gw, D_STATE)).astype(BF16))
        h_new = h0_ref[0, rows, :] * dec_rows[rows, :] + outer
        ht_ref[0, rows, :] = h_new
        yg = _dot_nt(jnp.broadcast_to(cg, (8, D_STATE)).astype(BF16), h_new.astype(BF16))
        ys.append(yg[0:1, :])
    y_ref[0] = jnp.concatenate(ys, axis=-1) + dsk_ref[...] * xs


def _ssd_step(xc, dt, alog, dsk, h0):
    b = xc.shape[0]
    e = jnp.asarray(_head_expand(), BF16)
    et = jnp.asarray(_head_expand().T.copy(), BF16)
    bc_w = SSM_GROUPS * D_STATE
    xc3 = xc.reshape(b, 1, CONV_DIM)
    return pl.pallas_call(
        _ssd_step_kernel,
        grid=(b,),
        in_specs=[pl.BlockSpec((1, 1, D_INNER), lambda i: (i, 0, 0)),
                  pl.BlockSpec((1, 1, bc_w), lambda i: (i, 0, D_INNER // bc_w)),
                  pl.BlockSpec((1, 1, bc_w), lambda i: (i, 0, D_INNER // bc_w + 1)),
                  pl.BlockSpec((1, 1, LANES), lambda i: (i, 0, 0)),
                  _const_spec((1, LANES)), _const_spec((1, D_INNER)),
                  _const_spec((LANES, D_INNER)), _const_spec((D_INNER, LANES)),
                  pl.BlockSpec((1, D_INNER, D_STATE), lambda i: (i, 0, 0))],
        out_specs=[pl.BlockSpec((1, 1, D_INNER), lambda i: (i, 0, 0)),
                   pl.BlockSpec((1, D_INNER, D_STATE), lambda i: (i, 0, 0))],
        out_shape=[jax.ShapeDtypeStruct((b, 1, D_INNER), F32),
                   jax.ShapeDtypeStruct((b, D_INNER, D_STATE), F32)],
        compiler_params=_cparams("arbitrary"),
        name="ssd_step",
    )(xc3, xc3, xc3, dt.reshape(b, 1, LANES), alog, dsk, e, et, h0)


def _fox_decode_kernel(pt_ref, qrow_ref, kn_ref, vn_ref, lfn_ref, *refs):
    n = PAGES_PER_STEP
    k_refs, v_refs, lf_refs = refs[:n], refs[n:2 * n], refs[2 * n:3 * n]
    o_ref, m_s, l_s, r_s, acc_s, qbd_s = refs[3 * n:]
    j = pl.program_id(1)
    rows8 = lax.broadcasted_iota(jnp.int32, (N_HEADS, HEAD_DIM), 0)

    @pl.when(j == 0)
    def _():
        on_diag = (lax.broadcasted_iota(jnp.int32, (N_HEADS, ATT_W), 1) >> (HEAD_DIM.bit_length() - 1)
                   == lax.broadcasted_iota(jnp.int32, (N_HEADS, ATT_W), 0))
        qrow = qrow_ref[0] * (HEAD_DIM ** -0.5)
        qbd_s[...] = jnp.where(on_diag, jnp.broadcast_to(qrow, (N_HEADS, ATT_W)), 0.0).astype(BF16)
        m_s[...] = jnp.sum(jnp.where(on_diag, jnp.broadcast_to(qrow * kn_ref[0], (N_HEADS, ATT_W)), 0.0),
                           axis=1, keepdims=True)
        l_s[...] = jnp.ones_like(l_s)
        r_s[...] = lfn_ref[0]
        acc_s[...] = vn_ref[0]

    r = lax.broadcasted_iota(jnp.int32, (LANES, LANES), 0)
    c = lax.broadcasted_iota(jnp.int32, (LANES, LANES), 1)
    triu = (r <= c).astype(F32)
    run = r_s[...]
    bias = []
    for i in range(n):
        cs = _dot(lf_refs[i][...], triu, precision=HIGHEST)
        total = cs[:, PAGE_SIZE - 1:PAGE_SIZE]
        bias.append(run + total - cs)
        run = run + total
    r_s[...] = run
    k_all = jnp.concatenate([k_refs[i][...].reshape(ATT_W, PAGE_SIZE).astype(BF16) for i in range(n)], axis=1)
    s = _dot(qbd_s[...], k_all) + jnp.concatenate(bias, axis=1)
    m_old = m_s[...]
    m_new = jnp.maximum(m_old, jnp.max(s, axis=1, keepdims=True))
    alpha = jnp.exp(m_old - m_new)
    p = jnp.exp(s - m_new)
    l_s[...] = alpha * l_s[...] + jnp.sum(p, axis=1, keepdims=True)
    m_s[...] = m_new
    pb = p.astype(BF16)
    acc = acc_s[...] * alpha
    for h in range(N_HEADS):
        v_h = jnp.concatenate([v_refs[i][h].astype(BF16) for i in range(n)], axis=1)
        acc = acc + jnp.where(rows8 == h, _dot_nt(pb, v_h), 0.0)
    acc_s[...] = acc

    @pl.when(j == pl.num_programs(1) - 1)
    def _():
        o_ref[0] = acc_s[...] / l_s[...]


def _fox_decode(qrow, kn_row, v_new, lf_new, pool_k, pool_v, pool_lf, page_table):
    b, n_pages = page_table.shape
    n = PAGES_PER_STEP
    steps = n_pages // n
    row3 = lambda i, j, pt: (i, 0, 0)

    def page_spec(shape, off):
        nd = len(shape)
        return pl.BlockSpec((None, None) + shape,
                            lambda i, j, pt: (0, pt[i, n_pages - 1 - (j * n + off)]) + (0,) * nd)

    in_specs = ([pl.BlockSpec((1, 1, ATT_W), row3)] * 2 + [pl.BlockSpec((1, N_HEADS, HEAD_DIM), row3),
                                                          pl.BlockSpec((1, N_HEADS, 1), row3)]
                + [page_spec((N_HEADS, HEAD_DIM, PAGE_SIZE), i) for i in range(n)] * 2
                + [page_spec((N_HEADS, PAGE_SIZE), i) for i in range(n)])
    grid_spec = pltpu.PrefetchScalarGridSpec(
        num_scalar_prefetch=1, grid=(b, steps), in_specs=in_specs,
        out_specs=pl.BlockSpec((1, N_HEADS, HEAD_DIM), row3),
        scratch_shapes=[pltpu.VMEM((N_HEADS, 1), F32)] * 3 + [pltpu.VMEM((N_HEADS, HEAD_DIM), F32),
                                                              pltpu.VMEM((N_HEADS, ATT_W), BF16)])
    return pl.pallas_call(
        _fox_decode_kernel, grid_spec=grid_spec,
        out_shape=jax.ShapeDtypeStruct((b, N_HEADS, HEAD_DIM), F32),
        compiler_params=_cparams("arbitrary", "arbitrary"),
        name="fox_decode_attn",
    )(page_table, qrow, kn_row, v_new, lf_new, *([pool_k] * n), *([pool_v] * n), *([pool_lf] * n))


def _moba_gate_kernel(pt_ref, qcol_ref, *refs):
    n = PAGES_PER_STEP
    k_refs = refs[:n]
    idx_ref, gate_s, qb_s = refs[n:]
    j = pl.program_id(1)
    ppb = MOBA_BLOCK // PAGE_SIZE
    lane = lax.broadcasted_iota(jnp.int32, (N_HEADS, LANES), 1)

    @pl.when(j == 0)
    def _():
        gate_s[...] = jnp.full_like(gate_s, -jnp.inf)
        qb_s[...] = jnp.broadcast_to(qcol_ref[0], (ATT_W, PAGE_SIZE))

    qb = qb_s[...]
    sub = 8
    fold = (lax.broadcasted_iota(jnp.int32, (N_HEADS, N_HEADS * sub), 1) >> 3
            == lax.broadcasted_iota(jnp.int32, (N_HEADS, N_HEADS * sub), 0)).astype(F32)
    gates = gate_s[...]
    for blk in range(n // ppb):
        ksum = None
        for i in range(ppb):
            page = k_refs[blk * ppb + i][...].reshape(ATT_W, PAGE_SIZE)
            ksum = page if ksum is None else ksum + page
        part = jnp.sum((ksum * qb).reshape(N_HEADS, HEAD_DIM // sub, sub, PAGE_SIZE), axis=1)
        per_head = _dot(fold, part.reshape(N_HEADS * sub, PAGE_SIZE), precision=HIGHEST)
        g = jnp.sum(per_head, axis=1, keepdims=True) * (1.0 / MOBA_BLOCK)
        gates = jnp.where(lane == j * (n // ppb) + blk, g, gates)
    gate_s[...] = gates

    @pl.when(j == pl.num_programs(1) - 1)
    def _():
        gate = gate_s[...]
        lane_f = lane.astype(F32)
        picks = jnp.zeros((N_HEADS, LANES), F32)
        for k in range(MOBA_TOPK):
            best = jnp.max(gate, axis=1, keepdims=True)
            first = jnp.min(jnp.where(gate == best, lane_f, float(LANES)), axis=1, keepdims=True)
            picks = jnp.where(lane == k, first, picks)
            gate = jnp.where(lane_f == first, -jnp.inf, gate)
        idx_ref[0] = picks.astype(jnp.int32)


def _moba_gate(qcol, pool_k, page_table):
    b, n_pages = page_table.shape
    n = PAGES_PER_STEP
    steps = n_pages // n
    row3 = lambda i, j, pt: (i, 0, 0)
    in_specs = [pl.BlockSpec((1, ATT_W, 1), row3)] + [
        pl.BlockSpec((None, None, N_HEADS, HEAD_DIM, PAGE_SIZE),
                     functools.partial(lambda i, j, pt, off: (0, pt[i, j * n + off], 0, 0, 0), off=off))
        for off in range(n)]
    grid_spec = pltpu.PrefetchScalarGridSpec(
        num_scalar_prefetch=1, grid=(b, steps), in_specs=in_specs,
        out_specs=pl.BlockSpec((1, N_HEADS, LANES), row3),
        scratch_shapes=[pltpu.VMEM((N_HEADS, LANES), F32), pltpu.VMEM((ATT_W, PAGE_SIZE), F32)])
    return pl.pallas_call(
        _moba_gate_kernel, grid_spec=grid_spec,
        out_shape=jax.ShapeDtypeStruct((b, N_HEADS, LANES), jnp.int32),
        compiler_params=_cparams("arbitrary", "arbitrary"),
        name="moba_decode_gate",
    )(page_table, qcol, *([pool_k] * n))


def _moba_decode_kernel(pt_ref, idx_ref, q_ref, kn_ref, vn_ref, *refs):
    npg = MOBA_TOPK * (MOBA_BLOCK // PAGE_SIZE)
    k_refs, v_refs = refs[:npg], refs[npg:2 * npg]
    o_ref, acc_s = refs[2 * npg:]
    h = pl.program_id(1)
    q8 = q_ref[0] * (HEAD_DIM ** -0.5)
    rows8 = lax.broadcasted_iota(jnp.int32, (N_HEADS, HEAD_DIM), 0)
    qh = jnp.where(rows8 == h, q8, 0.0).astype(BF16)

    @pl.when(h == 0)
    def _():
        acc_s[...] = jnp.zeros_like(acc_s)

    k_h = jnp.concatenate([k_refs[i][h].astype(BF16) for i in range(npg)], axis=1)
    v_h = jnp.concatenate([v_refs[i][h].astype(BF16) for i in range(npg)], axis=1)
    s = _dot(qh, k_h)
    s_new = jnp.sum(q8 * kn_ref[0], axis=1, keepdims=True)
    m = jnp.maximum(jnp.max(s, axis=1, keepdims=True), s_new)
    p = jnp.exp(s - m)
    p_new = jnp.exp(s_new - m)
    denom = jnp.sum(p, axis=1, keepdims=True) + p_new
    out = (_dot_nt(p.astype(BF16), v_h) + p_new * vn_ref[0]) / denom
    acc_s[...] = jnp.where(rows8 == h, out, acc_s[...])

    @pl.when(h == pl.num_programs(1) - 1)
    def _():
        o_ref[0] = acc_s[...]


def _moba_decode(q, k_new, v_new, idx, pool_k, pool_v, page_table):
    b, _ = page_table.shape
    ppb = MOBA_BLOCK // PAGE_SIZE
    npg = MOBA_TOPK * ppb
    row3 = lambda i, h, pt, ix: (i, 0, 0)
    page = lambda k, off: pl.BlockSpec(
        (None, None, N_HEADS, HEAD_DIM, PAGE_SIZE),
        lambda i, h, pt, ix: (0, pt[i, ix[i, h * MOBA_TOPK + k] * ppb + off], 0, 0, 0))
    in_specs = ([pl.BlockSpec((1, N_HEADS, HEAD_DIM), row3)] * 3
                + [page(k, off) for k in range(MOBA_TOPK) for off in range(ppb)] * 2)
    grid_spec = pltpu.PrefetchScalarGridSpec(
        num_scalar_prefetch=2, grid=(b, N_HEADS), in_specs=in_specs,
        out_specs=pl.BlockSpec((1, N_HEADS, HEAD_DIM), row3),
        scratch_shapes=[pltpu.VMEM((N_HEADS, HEAD_DIM), F32)])
    return pl.pallas_call(
        _moba_decode_kernel, grid_spec=grid_spec,
        out_shape=jax.ShapeDtypeStruct((b, N_HEADS, HEAD_DIM), F32),
        compiler_params=_cparams("arbitrary", "arbitrary"),
        name="moba_decode_attn",
    )(page_table, idx, q, k_new, v_new, *([pool_k] * npg), *([pool_v] * npg))


def _pad_lanes(a, width=LANES):
    return jnp.pad(a, [(0, 0)] * (a.ndim - 1) + [(0, width - a.shape[-1])])


def _run_group(x, ada, weights, decode, caches=None):
    b, t, _ = x.shape
    m = b * t
    x2d = x.reshape(m, D_MODEL)
    if decode:
        tm = m
        mod_map = lambda i: (0, 0, 0)
        as_mod = lambda a: a.reshape(1, b, D_MODEL)
    else:
        tm = 512
        tps = t // tm
        mod_map = lambda i: (i // tps, 0, 0)
        as_mod = lambda a: a.reshape(b, 1, D_MODEL)
    row = lambda v: v.reshape(1, -1)

    sh1, sc1, gt1, sh2, sc2, gt2 = [as_mod(a) for a in jnp.split(ada[0], 6, axis=-1)]
    pos = (caches["page_table"].shape[1] * PAGE_SIZE if decode else 0) + np.arange(t)
    qa, qb, ka, va, kb, vb, lf = _attn_proj(
        x2d, sc1, sh1, row(weights["g_mix_pre"][0]), weights["wt_att6"], weights["wt_att_f"],
        weights["b_fox_f"], pos, tm, mod_map, t, prompt=not decode)
    if decode:
        hd = lambda a: a.reshape(b, N_HEADS, HEAD_DIM)
        pt = caches["page_table"]
        idx = _moba_gate(qa.reshape(b, ATT_W, 1), caches["moba_k"], pt)[:, :, :MOBA_TOPK]
        oa = _moba_decode(hd(qa), hd(ka), hd(va), idx.reshape(b, N_HEADS * MOBA_TOPK),
                          caches["moba_k"], caches["moba_v"], pt)
        ob = _fox_decode(qb.reshape(b, 1, ATT_W), kb.reshape(b, 1, ATT_W), hd(vb), lf.reshape(b, N_HEADS, 1),
                         caches["fox_k"], caches["fox_v"], caches["fox_lf"], pt)
        oa, ob = oa.reshape(m, ATT_W), ob.reshape(m, ATT_W)
        rows5 = lambda a: a.reshape(1, b, t, N_HEADS, HEAD_DIM)
        lf_out = lf.reshape(1, b, t, N_HEADS)
    else:
        ft, fcol = _fox_cumsum(lf, b, t)
        oa = _attn_prompt(qa, ka, va, b, t, fox=False)
        ob = _attn_prompt(qb, kb, vb, b, t, fox=True, ft=ft, fcol=fcol)
        rows5 = lambda a: a.reshape(1, b, N_HEADS, HEAD_DIM, t).transpose(0, 1, 4, 2, 3)
        lf_out = lf.reshape(1, b, N_HEADS, t).transpose(0, 1, 3, 2)
    x2d = _post(x2d, [oa, ob], weights["w_att_out"], (gt1, sc2, sh2, gt2),
                (row(weights["g_mix_post"][0]), row(weights["g_ffn_pre"][0]), row(weights["g_ffn_post"][0])),
                weights["w_ffn_in"][0], weights["w_ffn_out"][0], tm, mod_map, ssm=False)

    sh1, sc1, gt1, sh2, sc2, gt2 = [as_mod(a) for a in jnp.split(ada[1], 6, axis=-1)]
    prev = caches["state_conv"] if decode else None
    z, xc, dt, conv_state = _ssm_proj(
        x2d, sc1, sh1, row(weights["g_mix_pre"][1]), weights["wt_ssm_z"], weights["wt_ssm_x"],
        weights["wt_ssm_dt"], weights["dt_bias"], weights["conv_w"], weights["conv_b"], tm, mod_map, t, prev=prev)
    if decode:
        y, h_t = _ssd_step(xc, dt, weights["a_log"], weights["d_skip"], caches["state_ssm"])
        y = y.reshape(m, D_INNER)
    else:
        y, h_t = _ssd_prompt(xc, dt, weights["a_log"], weights["d_skip"], b, t)
    x2d = _post(x2d, [y, z], weights["w_ssm_out"], (gt1, sc2, sh2, gt2),
                (row(weights["g_mix_post"][1]), row(weights["g_ffn_pre"][1]), row(weights["g_ffn_post"][1])),
                weights["w_ffn_in"][1], weights["w_ffn_out"][1], tm, mod_map, ssm=True,
                gn=row(weights["g_ssm_norm"]))

    return (x2d.reshape(b, t, D_MODEL), rows5(ka), rows5(va), rows5(kb), rows5(vb), lf_out,
            h_t.reshape(1, b, SSM_HEADS, SSM_HEAD_DIM, D_STATE),
            conv_state.transpose(1, 0, 2).reshape(1, b, CONV_W - 1, CONV_DIM))


def kernel(x_prompt, x_sample, cache_moba_k, cache_moba_v, cache_fox_k, cache_fox_v, cache_fox_logf, state_ssm, state_conv, page_table, c_prompt, c_sample, w_ada, b_ada, g_mix_pre, g_mix_post, g_ffn_pre, g_ffn_post, w_att_in, b_fox_f, w_att_out, w_ssm_in, conv_w, conv_b, dt_bias, a_log, d_skip, g_ssm_norm, w_ssm_out, w_ffn_in, w_ffn_out):
    bp = x_prompt.shape[0]
    bs = x_sample.shape[0]
    ada = _ada(jnp.concatenate([c_prompt, c_sample], axis=0), w_ada, b_ada)
    wt_att = w_att_in[0].T
    wt_ssm = w_ssm_in[0].T
    n_dt = wt_ssm.shape[0] - D_INNER - CONV_DIM
    weights = dict(
        g_mix_pre=g_mix_pre, g_mix_post=g_mix_post, g_ffn_pre=g_ffn_pre, g_ffn_post=g_ffn_post,
        wt_att6=wt_att[:6 * ATT_W].astype(BF16),
        wt_att_f=wt_att[6 * ATT_W:],
        b_fox_f=b_fox_f[0],
        w_att_out=w_att_out[0].astype(BF16),
        wt_ssm_z=wt_ssm[:D_INNER].astype(BF16),
        wt_ssm_x=wt_ssm[D_INNER:D_INNER + CONV_DIM].astype(BF16),
        wt_ssm_dt=jnp.pad(wt_ssm[D_INNER + CONV_DIM:], ((0, LANES - n_dt), (0, 0))).astype(BF16),
        dt_bias=_pad_lanes(dt_bias[0].reshape(1, -1)),
        conv_w=conv_w[0], conv_b=conv_b[0].reshape(1, -1),
        a_log=_pad_lanes(a_log[0].reshape(1, -1)),
        d_skip=jnp.repeat(d_skip[0], SSM_HEAD_DIM).reshape(1, -1),
        g_ssm_norm=g_ssm_norm[0],
        w_ssm_out=w_ssm_out[0].astype(BF16),
        w_ffn_in=w_ffn_in.astype(BF16), w_ffn_out=w_ffn_out.astype(BF16),
    )
    kv_t = lambda a: a.transpose(0, 1, 3, 4, 2)
    caches = dict(page_table=page_table, moba_k=kv_t(cache_moba_k), moba_v=kv_t(cache_moba_v),
                  fox_k=kv_t(cache_fox_k), fox_v=kv_t(cache_fox_v),
                  fox_lf=cache_fox_logf.transpose(0, 1, 3, 2),
                  state_ssm=state_ssm[0].reshape(bs, D_INNER, D_STATE),
                  state_conv=state_conv[0].transpose(1, 0, 2))
    prompt = _run_group(x_prompt, ada[:, :bp], weights, decode=False)
    sample = _run_group(x_sample, ada[:, bp:], weights, decode=True, caches=caches)
    return (prompt[0], sample[0]) + prompt[1:] + sample[1:]
```

```python
import functools
import math

import numpy as np
import jax
import jax.numpy as jnp
from jax import lax
from jax.experimental import pallas as pl
from jax.experimental.pallas import tpu as pltpu

F32 = jnp.float32
BF16 = jnp.bfloat16
HIGHEST = lax.Precision.HIGHEST

D_MODEL = 1024
HEAD_DIM = 64
N_HEADS = 8
ATT_W = N_HEADS * HEAD_DIM
ROT_DIM = HEAD_DIM // 4
ROPE_THETA = 500000.0
MOBA_BLOCK = 256
MOBA_TOPK = 3
PAGE_SIZE = 128
D_INNER = 2 * D_MODEL
SSM_HEAD_DIM = 64
SSM_HEADS = D_INNER // SSM_HEAD_DIM
SSM_GROUPS = 4
SSM_HPG = SSM_HEADS // SSM_GROUPS
D_STATE = 128
CONV_W = 4
CONV_DIM = D_INNER + 2 * SSM_GROUPS * D_STATE
SSD_CHUNK = 128
D_FF = ((8 * D_MODEL + 3 * 256 - 1) // (3 * 256)) * 256
MXU_COLS = 256
FFN_SPLIT = (D_FF // MXU_COLS + 1) // 2 * MXU_COLS
EPS = 1e-6
LANES = 128
VMEM_LIMIT = 56 * 1024 * 1024
PAGES_PER_STEP = 16
LOG2E = math.log2(math.e)
NEG = -1e30

_NT = (((1,), (1,)), ((), ()))


def _cparams(*sem):
    return pltpu.CompilerParams(dimension_semantics=sem, vmem_limit_bytes=VMEM_LIMIT)


def _const_spec(shape):
    nd = len(shape)
    return pl.BlockSpec(shape, lambda *_: (0,) * nd, pipeline_mode=pl.Buffered(1))


def _rms(x, g):
    return x * lax.rsqrt(jnp.mean(x * x, axis=-1, keepdims=True) + EPS) * g


def _silu(x):
    return x * jax.nn.sigmoid(x)


def _softplus(x):
    return jnp.maximum(x, 0.0) + jnp.log1p(jnp.exp(-jnp.abs(x)))


def _log_sigmoid(x):
    return jnp.minimum(x, 0.0) - jnp.log1p(jnp.exp(-jnp.abs(x)))


def _dot(a, b, **kw):
    return jnp.dot(a, b, preferred_element_type=F32, **kw)


def _dot_nt(a, b, **kw):
    return lax.dot_general(a, b, _NT, preferred_element_type=F32, **kw)


def _split3(x):
    hi = x.astype(BF16).astype(F32)
    r = x - hi
    mid = r.astype(BF16).astype(F32)
    return hi, mid, r - mid


def _ada_kernel(c_ref, w_ref, b_ref, o_ref):
    a = _silu(c_ref[...]).astype(BF16)
    o_ref[0] = _dot(a, w_ref[0].astype(BF16)) + b_ref[0]


def _ada(c_all, w_ada, b_ada):
    depth, _, n = w_ada.shape
    rows = c_all.shape[0]
    tn = 1024
    return pl.pallas_call(
        _ada_kernel,
        grid=(depth, n // tn),
        in_specs=[pl.BlockSpec((rows, D_MODEL), lambda l, j: (0, 0)),
                  pl.BlockSpec((1, D_MODEL, tn), lambda l, j: (l, 0, j)),
                  pl.BlockSpec((1, 1, tn), lambda l, j: (l, 0, j))],
        out_specs=pl.BlockSpec((1, rows, tn), lambda l, j: (l, 0, j)),
        out_shape=jax.ShapeDtypeStruct((depth, rows, n), F32),
        compiler_params=_cparams("arbitrary", "arbitrary"),
        name="ada_terms",
    )(c_all, w_ada, b_ada.reshape(depth, 1, n))


def _rope_tables(pos):
    half = ROT_DIM // 2
    inv = ROPE_THETA ** (-2.0 * np.arange(half, dtype=np.float64) / ROT_DIM)
    ang = np.asarray(pos, np.float64)[:, None] * inv[None, :]
    cos, sin = np.cos(ang), np.sin(ang)
    n = len(pos)
    c64 = np.concatenate([cos, cos, np.ones((n, HEAD_DIM - ROT_DIM))], axis=1)
    s1 = np.concatenate([-sin, np.zeros((n, HEAD_DIM - half))], axis=1)
    s2 = np.concatenate([np.zeros((n, half)), sin, np.zeros((n, HEAD_DIM - ROT_DIM))], axis=1)
    return [np.tile(t, (1, N_HEADS)).astype(np.float32) for t in (c64, s1, s2)]


def _rope(z, cos, s1, s2, axis):
    half = ROT_DIM // 2
    return z * cos + pltpu.roll(z, ATT_W - half, axis) * s1 + pltpu.roll(z, half, axis) * s2


def _attn_proj_kernel(*refs, prompt):
    x_ref, sc_ref, sh_ref, g_ref, wt_ref, wtf_ref, bf_ref, cos_ref, s1_ref, s2_ref = refs[:10]
    if prompt:
        cost_ref, s1t_ref, s2t_ref = refs[10:13]
        qa_ref, qb_ref, ka_ref, va_ref, kb_ref, vb_ref, lf_ref = refs[13:]
    else:
        qa_ref, qb_ref, ka_ref, va_ref, kb_ref, vb_ref, lf_ref = refs[10:]
    h = (_rms(x_ref[...], g_ref[...]) * (1.0 + sc_ref[0]) + sh_ref[0]).astype(BF16)
    w = lambda n: wt_ref[n * ATT_W:(n + 1) * ATT_W, :]
    qa_ref[...] = _rope(_dot_nt(h, w(0)), cos_ref[0], s1_ref[0], s2_ref[0], 1)
    qb_ref[...] = _dot_nt(h, w(3))
    wtf = wtf_ref[...].astype(BF16)
    if prompt:
        ka_ref[0] = _rope(_dot_nt(w(1), h), cost_ref[...], s1t_ref[...], s2t_ref[...], 0)
        va_ref[0] = _dot_nt(w(2), h)
        kb_ref[0] = _dot_nt(w(4), h)
        vb_ref[0] = _dot_nt(w(5), h)
        lf_ref[0] = _log_sigmoid(_dot_nt(wtf, h) + bf_ref[...])
    else:
        ka_ref[...] = _rope(_dot_nt(h, w(1)), cos_ref[0], s1_ref[0], s2_ref[0], 1)
        va_ref[...] = _dot_nt(h, w(2))
        kb_ref[...] = _dot_nt(h, w(4))
        vb_ref[...] = _dot_nt(h, w(5))
        lf_ref[...] = _log_sigmoid(_dot_nt(h, wtf) + bf_ref[...])


def _attn_proj(x2d, sc, sh, g, wt6, wtf, bf, pos, tm, mod_map, seq_len, prompt):
    m = x2d.shape[0]
    row = lambda i: (i, 0)
    mod_spec = pl.BlockSpec((1,) + sc.shape[1:], mod_map)
    tabs = _rope_tables(pos)
    wide = jax.ShapeDtypeStruct((m, ATT_W), F32)
    in_specs = [pl.BlockSpec((tm, D_MODEL), row), mod_spec, mod_spec, _const_spec((1, D_MODEL)),
                _const_spec(wt6.shape), _const_spec(wtf.shape)]
    if prompt:
        tps = seq_len // tm
        nseq = m // seq_len
        tab_spec = pl.BlockSpec((1, tm, ATT_W), lambda i: (i % tps, 0, 0))
        tabt_spec = pl.BlockSpec((ATT_W, tm), lambda i: (0, i % tps))
        fm_spec = pl.BlockSpec((1, ATT_W, tm), lambda i: (i // tps, 0, i % tps))
        fm = jax.ShapeDtypeStruct((nseq, ATT_W, seq_len), F32)
        in_specs += [_const_spec((N_HEADS, 1))] + [tab_spec] * 3 + [tabt_spec] * 3
        args = ([jnp.asarray(tb.reshape(tps, tm, ATT_W)) for tb in tabs]
                + [jnp.asarray(np.ascontiguousarray(tb.T)) for tb in tabs])
        out_specs = [pl.BlockSpec((tm, ATT_W), row)] * 2 + [fm_spec] * 4 + [
            pl.BlockSpec((1, N_HEADS, tm), lambda i: (i // tps, 0, i % tps))]
        out_shape = [wide] * 2 + [fm] * 4 + [jax.ShapeDtypeStruct((nseq, N_HEADS, seq_len), F32)]
        bf = bf.reshape(N_HEADS, 1)
    else:
        in_specs += [_const_spec((1, N_HEADS))] + [_const_spec((1, 1, ATT_W))] * 3
        args = [jnp.asarray(tb.reshape(1, 1, ATT_W)) for tb in tabs]
        out_specs = [pl.BlockSpec((tm, ATT_W), row)] * 6 + [pl.BlockSpec((tm, N_HEADS), row)]
        out_shape = [wide] * 6 + [jax.ShapeDtypeStruct((m, N_HEADS), F32)]
        bf = bf.reshape(1, N_HEADS)
    return pl.pallas_call(
        functools.partial(_attn_proj_kernel, prompt=prompt),
        grid=(m // tm,),
        in_specs=in_specs,
        out_specs=out_specs,
        out_shape=out_shape,
        compiler_params=_cparams("arbitrary"),
        name="attn_in_proj" if prompt else "attn_in_proj_decode",
    )(x2d, sc, sh, g, wt6, wtf, bf, *args)


def _cumsum_kernel(lf_ref, ft_ref, fcol_ref):
    t = lf_ref.shape[2]
    r = lax.broadcasted_iota(jnp.int32, (LANES, LANES), 0)
    c = lax.broadcasted_iota(jnp.int32, (LANES, LANES), 1)
    triu = (r <= c).astype(F32)
    carry = jnp.zeros((N_HEADS, 1), F32)
    pad = jnp.zeros((LANES - N_HEADS, LANES), F32)
    for i in range(t // LANES):
        cols = slice(i * LANES, (i + 1) * LANES)
        blk = _dot(lf_ref[0, :, cols], triu, precision=HIGHEST) + carry
        carry = blk[:, LANES - 1:LANES]
        ft_ref[0, :, cols] = blk
        fcol_ref[cols, :] = jnp.concatenate([blk, pad], axis=0).T


def _fox_cumsum(lf_t, b, t):
    return pl.pallas_call(
        _cumsum_kernel,
        grid=(b,),
        in_specs=[pl.BlockSpec((1, N_HEADS, t), lambda i: (i, 0, 0))],
        out_specs=[pl.BlockSpec((1, N_HEADS, t), lambda i: (i, 0, 0)),
                   pl.BlockSpec((t, LANES), lambda i: (i, 0))],
        out_shape=[jax.ShapeDtypeStruct((b, N_HEADS, t), F32), jax.ShapeDtypeStruct((b * t, LANES), F32)],
        compiler_params=_cparams("arbitrary"),
        name="fox_cumsum",
    )(lf_t)


def _attn_prompt_kernel(*refs, fox, t):
    if fox:
        q_ref, k_ref, v_ref, fcol_ref, ft_ref, o_ref, kaug, vaug = refs
    else:
        q_ref, k_ref, v_ref, o_ref, kaug, vaug = refs
    blk = MOBA_BLOCK
    nb = t // blk
    pair = pl.program_id(1)
    c_exp = (HEAD_DIM ** -0.5) * LOG2E
    row_t = lax.broadcasted_iota(jnp.int32, (LANES, t), 0)
    low_t = row_t < HEAD_DIM
    r64 = row_t & (HEAD_DIM - 1)
    kf = k_ref[0]
    vf = v_ref[0]
    if fox:
        augs = []
        for hd in range(2):
            hi, mid, lo = _split3(ft_ref[0, pl.ds(2 * pair + hd, 1), :] * LOG2E)
            augs.append(jnp.where(r64 < 3, 1.0, jnp.where(r64 == 3, -hi, jnp.where(
                r64 == 4, -mid, jnp.where(r64 == 5, -lo, 0.0)))))
        kaug[0] = jnp.where(low_t, kf, augs[0]).astype(BF16)
        kaug[1] = jnp.where(low_t, augs[1], kf).astype(BF16)
    else:
        lane_t = lax.broadcasted_iota(jnp.int32, (LANES, t), 1)
        key_blk = lane_t >> (blk.bit_length() - 1)
        ind = jnp.where((key_blk == r64) & (r64 < nb), 1.0, 0.0)
        kaug[0] = jnp.where(low_t, kf, ind).astype(BF16)
        kaug[1] = jnp.where(low_t, ind, kf).astype(BF16)
        hs = 8 * pl.cdiv(nb, 8)
        l128 = lax.broadcasted_iota(jnp.int32, (LANES, LANES), 1)
        r128 = lax.broadcasted_iota(jnp.int32, (LANES, LANES), 0)
        kcols = jnp.zeros((LANES, LANES), F32)
        for n in range(nb):
            col = jnp.mean(kf[:, n * blk:(n + 1) * blk], axis=1, keepdims=True)
            kcols = jnp.where((l128 == n) | (l128 == hs + n), col, kcols)
        kmean = jnp.where(((r128 < hs) & (l128 < HEAD_DIM)) | ((r128 >= hs) & (l128 >= HEAD_DIM)), kcols.T, 0.0)
        blk_id = lax.broadcasted_iota(jnp.int32, (hs, blk), 0)
        fill = jnp.zeros((HEAD_DIM - hs, blk), F32)
    vaug[0] = jnp.where(low_t, vf, jnp.where(row_t == HEAD_DIM, 1.0, 0.0)).astype(BF16)
    vaug[1] = jnp.where(low_t, jnp.where(row_t == 0, 1.0, 0.0), vf).astype(BF16)

    lane = lax.broadcasted_iota(jnp.int32, (blk, LANES), 1)
    low = lane < HEAD_DIM
    l64 = lane & (HEAD_DIM - 1)
    causal = (lax.broadcasted_iota(jnp.int32, (blk, blk), 1) <= lax.broadcasted_iota(jnp.int32, (blk, blk), 0))

    def q_block(qi):
        q0 = qi * blk
        q = q_ref[q0:q0 + blk, :]
        if fox:
            fq = fcol_ref[q0:q0 + blk, :] * LOG2E
            parts = []
            for hd in range(2):
                col = jnp.sum(jnp.where(lane == 2 * pair + hd, fq, 0.0), axis=1, keepdims=True)
                hi, mid, lo = _split3(col)
                parts.append(jnp.where(l64 == 0, hi, jnp.where(l64 == 1, mid, jnp.where(
                    l64 == 2, lo, jnp.where(l64 < 6, 1.0, 0.0)))))
            aug = jnp.where(low, parts[1], parts[0])
        else:
            gate_t = _dot_nt(kmean, q, precision=HIGHEST)
            aug_t = []
            for hd in range(2):
                gate = gate_t[hd * hs:(hd + 1) * hs, :]
                beaten = jnp.zeros((hs, blk), F32)
                for n2 in range(qi):
                    g2 = gate[n2:n2 + 1, :]
                    better = (g2 > gate) | ((g2 == gate) & (n2 < blk_id))
                    beaten = beaten + jnp.where(better, 1.0, 0.0)
                chosen = ((beaten < MOBA_TOPK) & (blk_id < qi)) | (blk_id == qi)
                aug_t.append(jnp.where((blk_id < nb) & jnp.logical_not(chosen), NEG, 0.0))
            aug = jnp.concatenate([aug_t[1], fill, aug_t[0], fill], axis=0).T
        qc = q * c_exp
        qaug = (jnp.where(low, qc, aug).astype(BF16), jnp.where(low, aug, qc).astype(BF16))

        own = slice(q0, q0 + blk)
        outs = []
        for hd in range(2):
            s_own = jnp.where(causal, _dot(qaug[hd], kaug[hd, :, own]), NEG)
            m = jnp.max(s_own, axis=1, keepdims=True)
            if qi > 0:
                s_prev = _dot(qaug[hd], kaug[hd, :, 0:q0])
                m = jnp.maximum(m, jnp.max(s_prev, axis=1, keepdims=True))
                acc = _dot_nt(jnp.exp2(s_prev - m).astype(BF16), vaug[hd, :, 0:q0])
            else:
                acc = jnp.zeros((blk, LANES), F32)
            acc = acc + _dot_nt(jnp.exp2(s_own - m).astype(BF16), vaug[hd, :, own])
            denom = jnp.sum(jnp.where(lane == (HEAD_DIM if hd == 0 else 0), acc, 0.0), axis=1, keepdims=True)
            outs.append(acc / denom)
        o_ref[own, :] = jnp.where(low, outs[0], outs[1])

    for qi in range(nb):
        q_block(qi)


def _attn_prompt(q, k_t, v_t, b, t, fox, ft=None, fcol=None):
    pairs = ATT_W // LANES
    slab = pl.BlockSpec((t, LANES), lambda i, p: (i, p))
    slab_t = pl.BlockSpec((1, LANES, t), lambda i, p: (i, p, 0))
    in_specs = [slab, slab_t, slab_t]
    args = [q, k_t, v_t]
    if fox:
        in_specs += [pl.BlockSpec((t, LANES), lambda i, p: (i, 0)),
                     pl.BlockSpec((1, N_HEADS, t), lambda i, p: (i, 0, 0))]
        args += [fcol, ft]
    return pl.pallas_call(
        functools.partial(_attn_prompt_kernel, fox=fox, t=t),
        grid=(b, pairs),
        in_specs=in_specs,
        out_specs=slab,
        out_shape=jax.ShapeDtypeStruct((b * t, ATT_W), F32),
        scratch_shapes=[pltpu.VMEM((2, LANES, t), BF16), pltpu.VMEM((2, LANES, t), BF16)],
        compiler_params=_cparams("arbitrary", "arbitrary"),
        name="fox_prompt_attn" if fox else "moba_prompt_attn",
    )(*args)


def _post_kernel(*refs, ssm):
    if ssm:
        x_ref, y_ref, z_ref, gn_ref, wmix_ref = refs[:5]
        rest = refs[5:]
    else:
        x_ref, oa_ref, ob_ref, wmix_ref = refs[:4]
        rest = refs[4:]
    gt1_ref, gpost_ref, gpre_ref, sc2_ref, sh2_ref, gt2_ref, gfpost_ref, win_ref, wout_ref, o_ref = rest
    if ssm:
        y = y_ref[...] * _silu(z_ref[...])
        gs = D_INNER // SSM_GROUPS
        parts = []
        for g in range(SSM_GROUPS):
            yg = y[:, g * gs:(g + 1) * gs]
            parts.append(yg * lax.rsqrt(jnp.mean(yg * yg, axis=-1, keepdims=True) + EPS))
        yn = (jnp.concatenate(parts, axis=-1) * gn_ref[...]).astype(BF16)
        mix = _dot(yn, wmix_ref[...])
    else:
        mix = (_dot(oa_ref[...].astype(BF16), wmix_ref[0:ATT_W, :])
               + _dot(ob_ref[...].astype(BF16), wmix_ref[ATT_W:2 * ATT_W, :]))
    x1 = x_ref[...] + gt1_ref[0] * _rms(mix, gpost_ref[...])
    h2 = (_rms(x1, gpre_ref[...]) * (1.0 + sc2_ref[0]) + sh2_ref[0]).astype(BF16)
    acc = None
    for lo, hi in ((0, FFN_SPLIT), (FFN_SPLIT, D_FF)):
        gate = _dot(h2, win_ref[:, lo:hi])
        up = _dot(h2, win_ref[:, D_FF + lo:D_FF + hi])
        part = _dot((_silu(gate) * up).astype(BF16), wout_ref[lo:hi, :])
        acc = part if acc is None else acc + part
    o_ref[...] = x1 + gt2_ref[0] * _rms(acc, gfpost_ref[...])


def _post(x2d, mix_in, wmix, mods, gains, win, wout, tm, mod_map, ssm, gn=None):
    m = x2d.shape[0]
    row = lambda i: (i, 0)
    gt1, sc2, sh2, gt2 = mods
    gpost, gpre, gfpost = gains
    mod_spec = pl.BlockSpec((1,) + gt1.shape[1:], mod_map)
    vec = _const_spec((1, D_MODEL))
    in_specs = [pl.BlockSpec((tm, D_MODEL), row)]
    args = [x2d]
    for a in mix_in:
        in_specs.append(pl.BlockSpec((tm, a.shape[1]), row))
        args.append(a)
    if ssm:
        in_specs.append(_const_spec((1, D_INNER)))
        args.append(gn)
    in_specs += [_const_spec(wmix.shape), mod_spec, vec, vec, mod_spec, mod_spec, mod_spec, vec,
                 _const_spec(win.shape), _const_spec(wout.shape)]
    args += [wmix, gt1, gpost, gpre, sc2, sh2, gt2, gfpost, win, wout]
    return pl.pallas_call(
        functools.partial(_post_kernel, ssm=ssm),
        grid=(m // tm,),
        in_specs=in_specs,
        out_specs=pl.BlockSpec((tm, D_MODEL), row),
        out_shape=jax.ShapeDtypeStruct((m, D_MODEL), F32),
        compiler_params=_cparams("arbitrary"),
        name="ssm_out_ffn" if ssm else "attn_out_ffn",
    )(*args)


def _ssm_proj_kernel(*refs, decode, tiles_per_seq):
    if decode:
        (x_ref, sc_ref, sh_ref, g_ref, wz_ref, wx_ref, wdt_ref, dtb_ref, cw_ref, cb_ref, prev_ref,
         z_ref, xc_ref, dt_ref, cs_ref) = refs
    else:
        (x_ref, sc_ref, sh_ref, g_ref, wz_ref, wx_ref, wdt_ref, dtb_ref, cw_ref, cb_ref,
         z_ref, xc_ref, dt_ref, cs_ref, ubuf) = refs
    tm = x_ref.shape[0]
    h = (_rms(x_ref[...], g_ref[...]) * (1.0 + sc_ref[0]) + sh_ref[0]).astype(BF16)
    z_ref[...] = _dot_nt(h, wz_ref[...])
    dt_ref[...] = _softplus(_dot_nt(h, wdt_ref[...]) + dtb_ref[...])
    if decode:
        u = _dot_nt(h, wx_ref[...])
        w = [cw_ref[j:j + 1, :] for j in range(CONV_W)]
        p0, p1, p2 = prev_ref[0], prev_ref[1], prev_ref[2]
        y = cb_ref[...] + w[3] * u + w[2] * p2 + w[1] * p1 + w[0] * p0
        cs_ref[0] = p1
        cs_ref[1] = p2
        cs_ref[2] = u
        xc_ref[...] = _silu(y)
    else:
        pad = 8
        tail = CONV_W - 1
        @pl.when(pl.program_id(0) % tiles_per_seq == 0)
        def _():
            ubuf[0:pad, :] = jnp.zeros((pad, CONV_DIM), F32)
        cw = CONV_DIM // 6
        for c in range(CONV_DIM // cw):
            cols = slice(c * cw, (c + 1) * cw)
            u = _dot_nt(h, wx_ref[cols, :])
            ubuf[pad:pad + tm, cols] = u
            y = cb_ref[:, cols] + cw_ref[tail:CONV_W, cols] * u
            for j in range(tail):
                y = y + cw_ref[j:j + 1, cols] * ubuf[pad - tail + j:pad - tail + j + tm, cols]
            xc_ref[:, cols] = _silu(y)
        cs_ref[:, 0, 0, :] = ubuf[pad + tm - tail:pad + tm, :]
        ubuf[0:pad, :] = ubuf[tm:tm + pad, :]


def _ssm_proj(x2d, sc, sh, g, wz, wx, wdt, dtb, cw, cb, tm, mod_map, seq_len, prev=None):
    m = x2d.shape[0]
    decode = prev is not None
    row = lambda i: (i, 0)
    mod_spec = pl.BlockSpec((1,) + sc.shape[1:], mod_map)
    in_specs = [pl.BlockSpec((tm, D_MODEL), row), mod_spec, mod_spec, _const_spec((1, D_MODEL)),
                _const_spec(wz.shape), _const_spec(wx.shape), _const_spec(wdt.shape), _const_spec((1, LANES)),
                _const_spec(cw.shape), _const_spec((1, CONV_DIM))]
    args = [x2d, sc, sh, g, wz, wx, wdt, dtb, cw, cb]
    scratch = []
    if decode:
        in_specs.append(_const_spec(prev.shape))
        args.append(prev)
        cs_shape = prev.shape
        cs_spec = _const_spec(prev.shape)
        tiles_per_seq = 1
    else:
        tiles_per_seq = seq_len // tm
        nseq = m // seq_len
        cs_shape = (CONV_W - 1, nseq, 1, CONV_DIM)
        cs_spec = pl.BlockSpec((CONV_W - 1, 1, 1, CONV_DIM), lambda i: (0, i // tiles_per_seq, 0, 0))
        scratch = [pltpu.VMEM((tm + 8, CONV_DIM), F32)]
    z, xc, dt, cs = pl.pallas_call(
        functools.partial(_ssm_proj_kernel, decode=decode, tiles_per_seq=tiles_per_seq),
        grid=(m // tm,),
        in_specs=in_specs,
        out_specs=[pl.BlockSpec((tm, D_INNER), row), pl.BlockSpec((tm, CONV_DIM), row),
                   pl.BlockSpec((tm, LANES), row), cs_spec],
        out_shape=[jax.ShapeDtypeStruct((m, D_INNER), F32), jax.ShapeDtypeStruct((m, CONV_DIM), F32),
                   jax.ShapeDtypeStruct((m, LANES), F32), jax.ShapeDtypeStruct(cs_shape, F32)],
        scratch_shapes=scratch,
        compiler_params=_cparams("arbitrary"),
        name="ssm_in_proj_decode" if decode else "ssm_in_proj",
    )(*args)
    return z, xc, dt, cs.reshape(CONV_W - 1, -1, CONV_DIM)


def _head_expand():
    e = np.zeros((LANES, D_INNER), np.float32)
    for hh in range(SSM_HEADS):
        e[hh, hh * SSM_HEAD_DIM:(hh + 1) * SSM_HEAD_DIM] = 1.0
    return e


def _spread(x, e):
    hi, mid, lo = _split3(x)
    return _dot(hi.astype(BF16), e) + _dot(mid.astype(BF16), e) + _dot(lo.astype(BF16), e)


def _spread_rows(e, x):
    hi, mid, lo = _split3(x)
    return _dot(e, hi.astype(BF16)) + _dot(e, mid.astype(BF16)) + _dot(e, lo.astype(BF16))


def _ssd_kernel(xs_ref, b_ref, c_ref, dt_ref, alog_ref, dsk_ref, e_ref, et_ref, y_ref, ht_ref, hst):
    q = SSD_CHUNK
    gw = D_INNER // SSM_GROUPS
    ci = pl.program_id(1)

    @pl.when(ci == 0)
    def _():
        hst[...] = jnp.zeros_like(hst)

    r = lax.broadcasted_iota(jnp.int32, (q, q), 0)
    c = lax.broadcasted_iota(jnp.int32, (q, q), 1)
    causal = r >= c
    lane = lax.broadcasted_iota(jnp.int32, (q, LANES), 1)
    first_half = lane < SSM_HEAD_DIM
    dt = dt_ref[...]
    a = -jnp.exp(alog_ref[...])
    acum = _dot(causal.astype(F32), dt * a, precision=HIGHEST)
    acum_t = acum.T
    dt_t = dt.T
    a_last = acum[q - 1:q, :]
    expand = e_ref[...]
    w_full = _spread(jnp.exp(a_last - acum) * dt, expand)
    ea_full = _spread(jnp.exp(acum), expand)
    dec_rows = _spread_rows(et_ref[...], jnp.exp(jnp.broadcast_to(acum_t[:, q - 1:q], (LANES, LANES))))
    xs = xs_ref[...]
    xs_bf = xs.astype(BF16)
    xw = xs * w_full
    dsk = dsk_ref[...]
    for g in range(SSM_GROUPS):
        bg = b_ref[:, g * D_STATE:(g + 1) * D_STATE].astype(BF16)
        cg = c_ref[:, g * D_STATE:(g + 1) * D_STATE].astype(BF16)
        cb = _dot_nt(cg, bg)
        rows = slice(g * gw, (g + 1) * gw)
        h_in = hst[rows, :]
        y_inter = _dot_nt(cg, h_in.astype(BF16))
        parts = []
        for pr in range(SSM_HPG // 2):
            h0 = g * SSM_HPG + 2 * pr
            x_pair = xs_bf[:, h0 * SSM_HEAD_DIM:(h0 + 2) * SSM_HEAD_DIM]
            outs = []
            for hh in (h0, h0 + 1):
                seg = acum[:, hh:hh + 1] - acum_t[hh:hh + 1, :]
                mh = cb * jnp.exp(jnp.where(causal, seg, -jnp.inf)) * dt_t[hh:hh + 1, :]
                outs.append(_dot(mh.astype(BF16), x_pair))
            parts.append(jnp.where(first_half, outs[0], outs[1]))
        y_intra = jnp.concatenate(parts, axis=-1)
        y_ref[:, rows] = y_intra + y_inter * ea_full[:, rows] + dsk[:, rows] * xs[:, rows]
        s_inc = _dot(xw[:, rows].T.astype(BF16), bg)
        hst[rows, :] = h_in * dec_rows[rows, :] + s_inc

    @pl.when(ci == pl.num_programs(1) - 1)
    def _():
        ht_ref[0] = hst[...]


def _ssd_prompt(xc, dt, alog, dsk, b, t):
    nc = t // SSD_CHUNK
    e = jnp.asarray(_head_expand(), BF16)
    et = jnp.asarray(_head_expand().T.copy(), BF16)
    rowmap = lambda i, c: (i * nc + c, 0)
    bc_w = SSM_GROUPS * D_STATE
    return pl.pallas_call(
        _ssd_kernel,
        grid=(b, nc),
        in_specs=[pl.BlockSpec((SSD_CHUNK, D_INNER), rowmap),
                  pl.BlockSpec((SSD_CHUNK, bc_w), lambda i, c: (i * nc + c, D_INNER // bc_w)),
                  pl.BlockSpec((SSD_CHUNK, bc_w), lambda i, c: (i * nc + c, D_INNER // bc_w + 1)),
                  pl.BlockSpec((SSD_CHUNK, LANES), rowmap),
                  _const_spec((1, LANES)), _const_spec((1, D_INNER)),
                  _const_spec((LANES, D_INNER)), _const_spec((D_INNER, LANES))],
        out_specs=[pl.BlockSpec((SSD_CHUNK, D_INNER), rowmap),
                   pl.BlockSpec((1, D_INNER, D_STATE), lambda i, c: (i, 0, 0))],
        out_shape=[jax.ShapeDtypeStruct((b * t, D_INNER), F32),
                   jax.ShapeDtypeStruct((b, D_INNER, D_STATE), F32)],
        scratch_shapes=[pltpu.VMEM((D_INNER, D_STATE), F32)],
        compiler_params=_cparams("arbitrary", "arbitrary"),
        name="ssd_scan",
    )(xc, xc, xc, dt, alog, dsk, e, et)


def _ssd_step_kernel(xs_ref, b_ref, c_ref, dt_ref, alog_ref, dsk_ref, e_ref, et_ref, h0_ref, y_ref, ht_ref):
    gw = D_INNER // SSM_GROUPS
    dt = dt_ref[0]
    a = -jnp.exp(alog_ref[...])
    dec = jnp.exp(dt * a)
    expand = e_ref[...]
    eye = (lax.broadcasted_iota(jnp.int32, (LANES, LANES), 0)
           == lax.broadcasted_iota(jnp.int32, (LANES, LANES), 1))
    dec_col = jnp.sum(jnp.where(eye, jnp.broadcast_to(dec, (LANES, LANES)), 0.0), axis=1, keepdims=True)
    dec_rows = _spread_rows(et_ref[...], jnp.broadcast_to(dec_col, (LANES, LANES)))
    xs = xs_ref[0]
    dtx = xs * _spread(dt, expand)
    eye_g = (lax.broadcasted_iota(jnp.int32, (gw, gw), 0) == lax.broadcasted_iota(jnp.int32, (gw, gw), 1))
    bm = b_ref[0]
    cm = c_ref[0]
    ys = []
    for g in range(SSM_GROUPS):
        rows = slice(g * gw, (g + 1) * gw)
        bg = bm[:, g * D_STATE:(g + 1) * D_STATE]
        cg = cm[:, g * D_STATE:(g + 1) * D_STATE]
        diag = jnp.where(eye_g, jnp.broadcast_to(dtx[:, rows], (gw, gw)), 0.0).astype(BF16)
        outer = _dot(diag, jnp.broadcast_to(bg, (gw, D_STATE)).astype(BF16))
        h_new = h0_ref[0, rows, :] * dec_rows[rows, :] + outer
        ht_ref[0, rows, :] = h_new
        yg = _dot_nt(jnp.broadcast_to(cg, (8, D_STATE)).astype(BF16), h_new.astype(BF16))
        ys.append(yg[0:1, :])
    y_ref[0] = jnp.concatenate(ys, axis=-1) + dsk_ref[...] * xs


def _ssd_step(xc, dt, alog, dsk, h0):
    b = xc.shape[0]
    e = jnp.asarray(_head_expand(), BF16)
    et = jnp.asarray(_head_expand().T.copy(), BF16)
    bc_w = SSM_GROUPS * D_STATE
    xc3 = xc.reshape(b, 1, CONV_DIM)
    return pl.pallas_call(
        _ssd_step_kernel,
        grid=(b,),
        in_specs=[pl.BlockSpec((1, 1, D_INNER), lambda i: (i, 0, 0)),
                  pl.BlockSpec((1, 1, bc_w), lambda i: (i, 0, D_INNER // bc_w)),
                  pl.BlockSpec((1, 1, bc_w), lambda i: (i, 0, D_INNER // bc_w + 1)),
                  pl.BlockSpec((1, 1, LANES), lambda i: (i, 0, 0)),
                  _const_spec((1, LANES)), _const_spec((1, D_INNER)),
                  _const_spec((LANES, D_INNER)), _const_spec((D_INNER, LANES)),
                  pl.BlockSpec((1, D_INNER, D_STATE), lambda i: (i, 0, 0))],
        out_specs=[pl.BlockSpec((1, 1, D_INNER), lambda i: (i, 0, 0)),
                   pl.BlockSpec((1, D_INNER, D_STATE), lambda i: (i, 0, 0))],
        out_shape=[jax.ShapeDtypeStruct((b, 1, D_INNER), F32),
                   jax.ShapeDtypeStruct((b, D_INNER, D_STATE), F32)],
        compiler_params=_cparams("arbitrary"),
        name="ssd_step",
    )(xc3, xc3, xc3, dt.reshape(b, 1, LANES), alog, dsk, e, et, h0)


def _fox_decode_kernel(pt_ref, qrow_ref, kn_ref, vn_ref, lfn_ref, *refs):
    n = PAGES_PER_STEP
    k_refs, v_refs, lf_refs = refs[:n], refs[n:2 * n], refs[2 * n:3 * n]
    o_ref, m_s, l_s, r_s, acc_s, qbd_s = refs[3 * n:]
    j = pl.program_id(1)
    rows8 = lax.broadcasted_iota(jnp.int32, (N_HEADS, HEAD_DIM), 0)

    @pl.when(j == 0)
    def _():
        on_diag = (lax.broadcasted_iota(jnp.int32, (N_HEADS, ATT_W), 1) >> (HEAD_DIM.bit_length() - 1)
                   == lax.broadcasted_iota(jnp.int32, (N_HEADS, ATT_W), 0))
        qrow = qrow_ref[0] * (HEAD_DIM ** -0.5)
        qbd_s[...] = jnp.where(on_diag, jnp.broadcast_to(qrow, (N_HEADS, ATT_W)), 0.0).astype(BF16)
        m_s[...] = jnp.sum(jnp.where(on_diag, jnp.broadcast_to(qrow * kn_ref[0], (N_HEADS, ATT_W)), 0.0),
                           axis=1, keepdims=True)
        l_s[...] = jnp.ones_like(l_s)
        r_s[...] = lfn_ref[0]
        acc_s[...] = vn_ref[0]

    r = lax.broadcasted_iota(jnp.int32, (LANES, LANES), 0)
    c = lax.broadcasted_iota(jnp.int32, (LANES, LANES), 1)
    triu = (r <= c).astype(F32)
    run = r_s[...]
    bias = []
    for i in range(n):
        cs = _dot(lf_refs[i][...], triu, precision=HIGHEST)
        total = cs[:, PAGE_SIZE - 1:PAGE_SIZE]
        bias.append(run + total - cs)
        run = run + total
    r_s[...] = run
    k_all = jnp.concatenate([k_refs[i][...].reshape(ATT_W, PAGE_SIZE).astype(BF16) for i in range(n)], axis=1)
    s = _dot(qbd_s[...], k_all) + jnp.concatenate(bias, axis=1)
    m_old = m_s[...]
    m_new = jnp.maximum(m_old, jnp.max(s, axis=1, keepdims=True))
    alpha = jnp.exp(m_old - m_new)
    p = jnp.exp(s - m_new)
    l_s[...] = alpha * l_s[...] + jnp.sum(p, axis=1, keepdims=True)
    m_s[...] = m_new
    pb = p.astype(BF16)
    acc = acc_s[...] * alpha
    for h in range(N_HEADS):
        v_h = jnp.concatenate([v_refs[i][h].astype(BF16) for i in range(n)], axis=1)
        acc = acc + jnp.where(rows8 == h, _dot_nt(pb, v_h), 0.0)
    acc_s[...] = acc

    @pl.when(j == pl.num_programs(1) - 1)
    def _():
        o_ref[0] = acc_s[...] / l_s[...]


def _fox_decode(qrow, kn_row, v_new, lf_new, pool_k, pool_v, pool_lf, page_table):
    b, n_pages = page_table.shape
    n = PAGES_PER_STEP
    steps = n_pages // n
    row3 = lambda i, j, pt: (i, 0, 0)

    def page_spec(shape, off):
        nd = len(shape)
        return pl.BlockSpec((None, None) + shape,
                            lambda i, j, pt: (0, pt[i, n_pages - 1 - (j * n + off)]) + (0,) * nd)

    in_specs = ([pl.BlockSpec((1, 1, ATT_W), row3)] * 2 + [pl.BlockSpec((1, N_HEADS, HEAD_DIM), row3),
                                                          pl.BlockSpec((1, N_HEADS, 1), row3)]
                + [page_spec((N_HEADS, HEAD_DIM, PAGE_SIZE), i) for i in range(n)] * 2
                + [page_spec((N_HEADS, PAGE_SIZE), i) for i in range(n)])
    grid_spec = pltpu.PrefetchScalarGridSpec(
        num_scalar_prefetch=1, grid=(b, steps), in_specs=in_specs,
        out_specs=pl.BlockSpec((1, N_HEADS, HEAD_DIM), row3),
        scratch_shapes=[pltpu.VMEM((N_HEADS, 1), F32)] * 3 + [pltpu.VMEM((N_HEADS, HEAD_DIM), F32),
                                                              pltpu.VMEM((N_HEADS, ATT_W), BF16)])
    return pl.pallas_call(
        _fox_decode_kernel, grid_spec=grid_spec,
        out_shape=jax.ShapeDtypeStruct((b, N_HEADS, HEAD_DIM), F32),
        compiler_params=_cparams("arbitrary", "arbitrary"),
        name="fox_decode_attn",
    )(page_table, qrow, kn_row, v_new, lf_new, *([pool_k] * n), *([pool_v] * n), *([pool_lf] * n))


def _moba_gate_kernel(pt_ref, qcol_ref, *refs):
    n = PAGES_PER_STEP
    k_refs = refs[:n]
    idx_ref, gate_s, qb_s = refs[n:]
    j = pl.program_id(1)
    ppb = MOBA_BLOCK // PAGE_SIZE
    lane = lax.broadcasted_iota(jnp.int32, (N_HEADS, LANES), 1)

    @pl.when(j == 0)
    def _():
        gate_s[...] = jnp.full_like(gate_s, -jnp.inf)
        qb_s[...] = jnp.broadcast_to(qcol_ref[0], (ATT_W, PAGE_SIZE))

    qb = qb_s[...]
    sub = 8
    fold = (lax.broadcasted_iota(jnp.int32, (N_HEADS, N_HEADS * sub), 1) >> 3
            == lax.broadcasted_iota(jnp.int32, (N_HEADS, N_HEADS * sub), 0)).astype(F32)
    gates = gate_s[...]
    for blk in range(n // ppb):
        ksum = None
        for i in range(ppb):
            page = k_refs[blk * ppb + i][...].reshape(ATT_W, PAGE_SIZE)
            ksum = page if ksum is None else ksum + page
        part = jnp.sum((ksum * qb).reshape(N_HEADS, HEAD_DIM // sub, sub, PAGE_SIZE), axis=1)
        per_head = _dot(fold, part.reshape(N_HEADS * sub, PAGE_SIZE), precision=HIGHEST)
        g = jnp.sum(per_head, axis=1, keepdims=True) * (1.0 / MOBA_BLOCK)
        gates = jnp.where(lane == j * (n // ppb) + blk, g, gates)
    gate_s[...] = gates

    @pl.when(j == pl.num_programs(1) - 1)
    def _():
        gate = gate_s[...]
        lane_f = lane.astype(F32)
        picks = jnp.zeros((N_HEADS, LANES), F32)
        for k in range(MOBA_TOPK):
            best = jnp.max(gate, axis=1, keepdims=True)
            first = jnp.min(jnp.where(gate == best, lane_f, float(LANES)), axis=1, keepdims=True)
            picks = jnp.where(lane == k, first, picks)
            gate = jnp.where(lane_f == first, -jnp.inf, gate)
        idx_ref[0] = picks.astype(jnp.int32)


def _moba_gate(qcol, pool_k, page_table):
    b, n_pages = page_table.shape
    n = PAGES_PER_STEP
    steps = n_pages // n
    row3 = lambda i, j, pt: (i, 0, 0)
    in_specs = [pl.BlockSpec((1, ATT_W, 1), row3)] + [
        pl.BlockSpec((None, None, N_HEADS, HEAD_DIM, PAGE_SIZE),
                     functools.partial(lambda i, j, pt, off: (0, pt[i, j * n + off], 0, 0, 0), off=off))
        for off in range(n)]
    grid_spec = pltpu.PrefetchScalarGridSpec(
        num_scalar_prefetch=1, grid=(b, steps), in_specs=in_specs,
        out_specs=pl.BlockSpec((1, N_HEADS, LANES), row3),
        scratch_shapes=[pltpu.VMEM((N_HEADS, LANES), F32), pltpu.VMEM((ATT_W, PAGE_SIZE), F32)])
    return pl.pallas_call(
        _moba_gate_kernel, grid_spec=grid_spec,
        out_shape=jax.ShapeDtypeStruct((b, N_HEADS, LANES), jnp.int32),
        compiler_params=_cparams("arbitrary", "arbitrary"),
        name="moba_decode_gate",
    )(page_table, qcol, *([pool_k] * n))


def _moba_decode_kernel(pt_ref, idx_ref, q_ref, kn_ref, vn_ref, *refs):
    npg = MOBA_TOPK * (MOBA_BLOCK // PAGE_SIZE)
    k_refs, v_refs = refs[:npg], refs[npg:2 * npg]
    o_ref, acc_s = refs[2 * npg:]
    h = pl.program_id(1)
    q8 = q_ref[0] * (HEAD_DIM ** -0.5)
    rows8 = lax.broadcasted_iota(jnp.int32, (N_HEADS, HEAD_DIM), 0)
    qh = jnp.where(rows8 == h, q8, 0.0).astype(BF16)

    @pl.when(h == 0)
    def _():
        acc_s[...] = jnp.zeros_like(acc_s)

    k_h = jnp.concatenate([k_refs[i][...].astype(BF16) for i in range(npg)], axis=1)
    v_h = jnp.concatenate([v_refs[i][...].astype(BF16) for i in range(npg)], axis=1)
    s = _dot(qh, k_h)
    s_new = jnp.sum(q8 * kn_ref[0], axis=1, keepdims=True)
    m = jnp.maximum(jnp.max(s, axis=1, keepdims=True), s_new)
    p = jnp.exp(s - m)
    p_new = jnp.exp(s_new - m)
    denom = jnp.sum(p, axis=1, keepdims=True) + p_new
    out = (_dot_nt(p.astype(BF16), v_h) + p_new * vn_ref[0]) / denom
    acc_s[...] = jnp.where(rows8 == h, out, acc_s[...])

    @pl.when(h == pl.num_programs(1) - 1)
    def _():
        o_ref[0] = acc_s[...]


def _moba_decode(q, k_new, v_new, idx, pool_k, pool_v, page_table):
    b, _ = page_table.shape
    ppb = MOBA_BLOCK // PAGE_SIZE
    npg = MOBA_TOPK * ppb
    row3 = lambda i, h, pt, ix: (i, 0, 0)
    page = lambda k, off: pl.BlockSpec(
        (None, None, None, HEAD_DIM, PAGE_SIZE),
        lambda i, h, pt, ix: (0, pt[i, ix[i, h * MOBA_TOPK + k] * ppb + off], h, 0, 0))
    in_specs = ([pl.BlockSpec((1, N_HEADS, HEAD_DIM), row3)] * 3
                + [page(k, off) for k in range(MOBA_TOPK) for off in range(ppb)] * 2)
    grid_spec = pltpu.PrefetchScalarGridSpec(
        num_scalar_prefetch=2, grid=(b, N_HEADS), in_specs=in_specs,
        out_specs=pl.BlockSpec((1, N_HEADS, HEAD_DIM), row3),
        scratch_shapes=[pltpu.VMEM((N_HEADS, HEAD_DIM), F32)])
    return pl.pallas_call(
        _moba_decode_kernel, grid_spec=grid_spec,
        out_shape=jax.ShapeDtypeStruct((b, N_HEADS, HEAD_DIM), F32),
        compiler_params=_cparams("arbitrary", "arbitrary"),
        name="moba_decode_attn",
    )(page_table, idx, q, k_new, v_new, *([pool_k] * npg), *([pool_v] * npg))


def _pad_lanes(a, width=LANES):
    return jnp.pad(a, [(0, 0)] * (a.ndim - 1) + [(0, width - a.shape[-1])])


def _run_group(x, ada, weights, decode, caches=None):
    b, t, _ = x.shape
    m = b * t
    x2d = x.reshape(m, D_MODEL)
    if decode:
        tm = m
        mod_map = lambda i: (0, 0, 0)
        as_mod = lambda a: a.reshape(1, b, D_MODEL)
    else:
        tm = 512
        tps = t // tm
        mod_map = lambda i: (i // tps, 0, 0)
        as_mod = lambda a: a.reshape(b, 1, D_MODEL)
    row = lambda v: v.reshape(1, -1)

    sh1, sc1, gt1, sh2, sc2, gt2 = [as_mod(a) for a in jnp.split(ada[0], 6, axis=-1)]
    pos = (caches["page_table"].shape[1] * PAGE_SIZE if decode else 0) + np.arange(t)
    qa, qb, ka, va, kb, vb, lf = _attn_proj(
        x2d, sc1, sh1, row(weights["g_mix_pre"][0]), weights["wt_att6"], weights["wt_att_f"],
        weights["b_fox_f"], pos, tm, mod_map, t, prompt=not decode)
    if decode:
        hd = lambda a: a.reshape(b, N_HEADS, HEAD_DIM)
        pt = caches["page_table"]
        idx = _moba_gate(qa.reshape(b, ATT_W, 1), caches["moba_k"], pt)[:, :, :MOBA_TOPK]
        oa = _moba_decode(hd(qa), hd(ka), hd(va), idx.reshape(b, N_HEADS * MOBA_TOPK),
                          caches["moba_k"], caches["moba_v"], pt)
        ob = _fox_decode(qb.reshape(b, 1, ATT_W), kb.reshape(b, 1, ATT_W), hd(vb), lf.reshape(b, N_HEADS, 1),
                         caches["fox_k"], caches["fox_v"], caches["fox_lf"], pt)
        oa, ob = oa.reshape(m, ATT_W), ob.reshape(m, ATT_W)
        rows5 = lambda a: a.reshape(1, b, t, N_HEADS, HEAD_DIM)
        lf_out = lf.reshape(1, b, t, N_HEADS)
    else:
        ft, fcol = _fox_cumsum(lf, b, t)
        oa = _attn_prompt(qa, ka, va, b, t, fox=False)
        ob = _attn_prompt(qb, kb, vb, b, t, fox=True, ft=ft, fcol=fcol)
        rows5 = lambda a: a.reshape(1, b, N_HEADS, HEAD_DIM, t).transpose(0, 1, 4, 2, 3)
        lf_out = lf.reshape(1, b, N_HEADS, t).transpose(0, 1, 3, 2)
    x2d = _post(x2d, [oa, ob], weights["w_att_out"], (gt1, sc2, sh2, gt2),
                (row(weights["g_mix_post"][0]), row(weights["g_ffn_pre"][0]), row(weights["g_ffn_post"][0])),
                weights["w_ffn_in"][0], weights["w_ffn_out"][0], tm, mod_map, ssm=False)

    sh1, sc1, gt1, sh2, sc2, gt2 = [as_mod(a) for a in jnp.split(ada[1], 6, axis=-1)]
    prev = caches["state_conv"] if decode else None
    z, xc, dt, conv_state = _ssm_proj(
        x2d, sc1, sh1, row(weights["g_mix_pre"][1]), weights["wt_ssm_z"], weights["wt_ssm_x"],
        weights["wt_ssm_dt"], weights["dt_bias"], weights["conv_w"], weights["conv_b"], tm, mod_map, t, prev=prev)
    if decode:
        y, h_t = _ssd_step(xc, dt, weights["a_log"], weights["d_skip"], caches["state_ssm"])
        y = y.reshape(m, D_INNER)
    else:
        y, h_t = _ssd_prompt(xc, dt, weights["a_log"], weights["d_skip"], b, t)
    x2d = _post(x2d, [y, z], weights["w_ssm_out"], (gt1, sc2, sh2, gt2),
                (row(weights["g_mix_post"][1]), row(weights["g_ffn_pre"][1]), row(weights["g_ffn_post"][1])),
                weights["w_ffn_in"][1], weights["w_ffn_out"][1], tm, mod_map, ssm=True,
                gn=row(weights["g_ssm_norm"]))

    return (x2d.reshape(b, t, D_MODEL), rows5(ka), rows5(va), rows5(kb), rows5(vb), lf_out,
            h_t.reshape(1, b, SSM_HEADS, SSM_HEAD_DIM, D_STATE),
            conv_state.transpose(1, 0, 2).reshape(1, b, CONV_W - 1, CONV_DIM))


def kernel(x_prompt, x_sample, cache_moba_k, cache_moba_v, cache_fox_k, cache_fox_v, cache_fox_logf, state_ssm, state_conv, page_table, c_prompt, c_sample, w_ada, b_ada, g_mix_pre, g_mix_post, g_ffn_pre, g_ffn_post, w_att_in, b_fox_f, w_att_out, w_ssm_in, conv_w, conv_b, dt_bias, a_log, d_skip, g_ssm_norm, w_ssm_out, w_ffn_in, w_ffn_out):
    bp = x_prompt.shape[0]
    bs = x_sample.shape[0]
    ada = _ada(jnp.concatenate([c_prompt, c_sample], axis=0), w_ada, b_ada)
    wt_att = w_att_in[0].T
    wt_ssm = w_ssm_in[0].T
    n_dt = wt_ssm.shape[0] - D_INNER - CONV_DIM
    weights = dict(
        g_mix_pre=g_mix_pre, g_mix_post=g_mix_post, g_ffn_pre=g_ffn_pre, g_ffn_post=g_ffn_post,
        wt_att6=wt_att[:6 * ATT_W].astype(BF16),
        wt_att_f=wt_att[6 * ATT_W:],
        b_fox_f=b_fox_f[0],
        w_att_out=w_att_out[0].astype(BF16),
        wt_ssm_z=wt_ssm[:D_INNER].astype(BF16),
        wt_ssm_x=wt_ssm[D_INNER:D_INNER + CONV_DIM].astype(BF16),
        wt_ssm_dt=jnp.pad(wt_ssm[D_INNER + CONV_DIM:], ((0, LANES - n_dt), (0, 0))).astype(BF16),
        dt_bias=_pad_lanes(dt_bias[0].reshape(1, -1)),
        conv_w=conv_w[0], conv_b=conv_b[0].reshape(1, -1),
        a_log=_pad_lanes(a_log[0].reshape(1, -1)),
        d_skip=jnp.repeat(d_skip[0], SSM_HEAD_DIM).reshape(1, -1),
        g_ssm_norm=g_ssm_norm[0],
        w_ssm_out=w_ssm_out[0].astype(BF16),
        w_ffn_in=w_ffn_in.astype(BF16), w_ffn_out=w_ffn_out.astype(BF16),
    )
    kv_t = lambda a: a.transpose(0, 1, 3, 4, 2)
    caches = dict(page_table=page_table, moba_k=kv_t(cache_moba_k), moba_v=kv_t(cache_moba_v),
                  fox_k=kv_t(cache_fox_k), fox_v=kv_t(cache_fox_v),
                  fox_lf=cache_fox_logf.transpose(0, 1, 3, 2),
                  state_ssm=state_ssm[0].reshape(bs, D_INNER, D_STATE),
                  state_conv=state_conv[0].transpose(1, 0, 2))
    prompt = _run_group(x_prompt, ada[:, :bp], weights, decode=False)
    sample = _run_group(x_sample, ada[:, bp:], weights, decode=True, caches=caches)
    return (prompt[0], sample[0]) + prompt[1:] + sample[1:]
```

```python
import functools
import math

import numpy as np
import jax
import jax.numpy as jnp
from jax import lax
from jax.experimental import pallas as pl
from jax.experimental.pallas import tpu as pltpu

F32 = jnp.float32
BF16 = jnp.bfloat16
HIGHEST = lax.Precision.HIGHEST

D_MODEL = 1024
HEAD_DIM = 64
N_HEADS = 8
ATT_W = N_HEADS * HEAD_DIM
ROT_DIM = HEAD_DIM // 4
ROPE_THETA = 500000.0
MOBA_BLOCK = 256
MOBA_TOPK = 3
PAGE_SIZE = 128
D_INNER = 2 * D_MODEL
SSM_HEAD_DIM = 64
SSM_HEADS = D_INNER // SSM_HEAD_DIM
SSM_GROUPS = 4
SSM_HPG = SSM_HEADS // SSM_GROUPS
D_STATE = 128
CONV_W = 4
CONV_DIM = D_INNER + 2 * SSM_GROUPS * D_STATE
SSD_CHUNK = 128
SSD_CHUNKS_PER_STEP = 2
D_FF = ((8 * D_MODEL + 3 * 256 - 1) // (3 * 256)) * 256
MXU_COLS = 256
FFN_SPLIT = (D_FF // MXU_COLS + 1) // 2 * MXU_COLS
EPS = 1e-6
LANES = 128
VMEM_LIMIT = 56 * 1024 * 1024
PAGES_PER_STEP = 16
MOBA_HEADS_PER_STEP = 2
LOG2E = math.log2(math.e)
NEG = -1e30

_NT = (((1,), (1,)), ((), ()))


def _cparams(*sem):
    return pltpu.CompilerParams(dimension_semantics=sem, vmem_limit_bytes=VMEM_LIMIT)


def _const_spec(shape):
    nd = len(shape)
    return pl.BlockSpec(shape, lambda *_: (0,) * nd, pipeline_mode=pl.Buffered(1))


def _rms(x, g):
    return x * lax.rsqrt(jnp.mean(x * x, axis=-1, keepdims=True) + EPS) * g


def _silu(x):
    return x * jax.nn.sigmoid(x)


def _softplus(x):
    return jnp.maximum(x, 0.0) + jnp.log1p(jnp.exp(-jnp.abs(x)))


def _log_sigmoid(x):
    return jnp.minimum(x, 0.0) - jnp.log1p(jnp.exp(-jnp.abs(x)))


def _dot(a, b, **kw):
    return jnp.dot(a, b, preferred_element_type=F32, **kw)


def _dot_nt(a, b, **kw):
    return lax.dot_general(a, b, _NT, preferred_element_type=F32, **kw)


def _split3(x):
    hi = x.astype(BF16).astype(F32)
    r = x - hi
    mid = r.astype(BF16).astype(F32)
    return hi, mid, r - mid


def _ada_kernel(c_ref, w_ref, b_ref, o_ref):
    a = _silu(c_ref[...]).astype(BF16)
    o_ref[0] = _dot(a, w_ref[0].astype(BF16)) + b_ref[0]


def _ada(c_all, w_ada, b_ada):
    depth, _, n = w_ada.shape
    rows = c_all.shape[0]
    tn = 1024
    return pl.pallas_call(
        _ada_kernel,
        grid=(depth, n // tn),
        in_specs=[pl.BlockSpec((rows, D_MODEL), lambda l, j: (0, 0)),
                  pl.BlockSpec((1, D_MODEL, tn), lambda l, j: (l, 0, j)),
                  pl.BlockSpec((1, 1, tn), lambda l, j: (l, 0, j))],
        out_specs=pl.BlockSpec((1, rows, tn), lambda l, j: (l, 0, j)),
        out_shape=jax.ShapeDtypeStruct((depth, rows, n), F32),
        compiler_params=_cparams("arbitrary", "arbitrary"),
        name="ada_terms",
    )(c_all, w_ada, b_ada.reshape(depth, 1, n))


def _rope_tables(pos):
    half = ROT_DIM // 2
    inv = ROPE_THETA ** (-2.0 * np.arange(half, dtype=np.float64) / ROT_DIM)
    ang = np.asarray(pos, np.float64)[:, None] * inv[None, :]
    cos, sin = np.cos(ang), np.sin(ang)
    n = len(pos)
    c64 = np.concatenate([cos, cos, np.ones((n, HEAD_DIM - ROT_DIM))], axis=1)
    s1 = np.concatenate([-sin, np.zeros((n, HEAD_DIM - half))], axis=1)
    s2 = np.concatenate([np.zeros((n, half)), sin, np.zeros((n, HEAD_DIM - ROT_DIM))], axis=1)
    return [np.tile(t, (1, N_HEADS)).astype(np.float32) for t in (c64, s1, s2)]


def _rope(z, cos, s1, s2, axis):
    half = ROT_DIM // 2
    return z * cos + pltpu.roll(z, ATT_W - half, axis) * s1 + pltpu.roll(z, half, axis) * s2


def _attn_proj_kernel(*refs, prompt):
    x_ref, sc_ref, sh_ref, g_ref, wt_ref, wtf_ref, bf_ref, cos_ref, s1_ref, s2_ref = refs[:10]
    if prompt:
        cost_ref, s1t_ref, s2t_ref = refs[10:13]
        qa_ref, qb_ref, ka_ref, va_ref, kb_ref, vb_ref, lf_ref = refs[13:]
    else:
        qa_ref, qb_ref, ka_ref, va_ref, kb_ref, vb_ref, lf_ref = refs[10:]
    h = (_rms(x_ref[...], g_ref[...]) * (1.0 + sc_ref[0]) + sh_ref[0]).astype(BF16)
    w = lambda n: wt_ref[n * ATT_W:(n + 1) * ATT_W, :]
    qa_ref[...] = _rope(_dot_nt(h, w(0)), cos_ref[0], s1_ref[0], s2_ref[0], 1)
    qb_ref[...] = _dot_nt(h, w(3))
    wtf = wtf_ref[...].astype(BF16)
    if prompt:
        ka_ref[0] = _rope(_dot_nt(w(1), h), cost_ref[...], s1t_ref[...], s2t_ref[...], 0)
        va_ref[0] = _dot_nt(w(2), h)
        kb_ref[0] = _dot_nt(w(4), h)
        vb_ref[0] = _dot_nt(w(5), h)
        lf_ref[0] = _log_sigmoid(_dot_nt(wtf, h) + bf_ref[...])
    else:
        ka_ref[...] = _rope(_dot_nt(h, w(1)), cos_ref[0], s1_ref[0], s2_ref[0], 1)
        va_ref[...] = _dot_nt(h, w(2))
        kb_ref[...] = _dot_nt(h, w(4))
        vb_ref[...] = _dot_nt(h, w(5))
        lf_ref[...] = _log_sigmoid(_dot_nt(h, wtf) + bf_ref[...])


def _attn_proj(x2d, sc, sh, g, wt6, wtf, bf, pos, tm, mod_map, seq_len, prompt):
    m = x2d.shape[0]
    row = lambda i: (i, 0)
    mod_spec = pl.BlockSpec((1,) + sc.shape[1:], mod_map)
    tabs = _rope_tables(pos)
    wide = jax.ShapeDtypeStruct((m, ATT_W), F32)
    in_specs = [pl.BlockSpec((tm, D_MODEL), row), mod_spec, mod_spec, _const_spec((1, D_MODEL)),
                _const_spec(wt6.shape), _const_spec(wtf.shape)]
    if prompt:
        tps = seq_len // tm
        nseq = m // seq_len
        tab_spec = pl.BlockSpec((1, tm, ATT_W), lambda i: (i % tps, 0, 0))
        tabt_spec = pl.BlockSpec((ATT_W, tm), lambda i: (0, i % tps))
        fm_spec = pl.BlockSpec((1, ATT_W, tm), lambda i: (i // tps, 0, i % tps))
        fm = jax.ShapeDtypeStruct((nseq, ATT_W, seq_len), F32)
        in_specs += [_const_spec((N_HEADS, 1))] + [tab_spec] * 3 + [tabt_spec] * 3
        args = ([jnp.asarray(tb.reshape(tps, tm, ATT_W)) for tb in tabs]
                + [jnp.asarray(np.ascontiguousarray(tb.T)) for tb in tabs])
        out_specs = [pl.BlockSpec((tm, ATT_W), row)] * 2 + [fm_spec] * 4 + [
            pl.BlockSpec((1, N_HEADS, tm), lambda i: (i // tps, 0, i % tps))]
        out_shape = [wide] * 2 + [fm] * 4 + [jax.ShapeDtypeStruct((nseq, N_HEADS, seq_len), F32)]
        bf = bf.reshape(N_HEADS, 1)
    else:
        in_specs += [_const_spec((1, N_HEADS))] + [_const_spec((1, 1, ATT_W))] * 3
        args = [jnp.asarray(tb.reshape(1, 1, ATT_W)) for tb in tabs]
        out_specs = [pl.BlockSpec((tm, ATT_W), row)] * 6 + [pl.BlockSpec((tm, N_HEADS), row)]
        out_shape = [wide] * 6 + [jax.ShapeDtypeStruct((m, N_HEADS), F32)]
        bf = bf.reshape(1, N_HEADS)
    return pl.pallas_call(
        functools.partial(_attn_proj_kernel, prompt=prompt),
        grid=(m // tm,),
        in_specs=in_specs,
        out_specs=out_specs,
        out_shape=out_shape,
        compiler_params=_cparams("arbitrary"),
        name="attn_in_proj" if prompt else "attn_in_proj_decode",
    )(x2d, sc, sh, g, wt6, wtf, bf, *args)


def _cumsum_kernel(lf_ref, ft_ref, fcol_ref):
    t = lf_ref.shape[2]
    r = lax.broadcasted_iota(jnp.int32, (LANES, LANES), 0)
    c = lax.broadcasted_iota(jnp.int32, (LANES, LANES), 1)
    triu = (r <= c).astype(F32)
    carry = jnp.zeros((N_HEADS, 1), F32)
    pad = jnp.zeros((LANES - N_HEADS, LANES), F32)
    for i in range(t // LANES):
        cols = slice(i * LANES, (i + 1) * LANES)
        blk = _dot(lf_ref[0, :, cols], triu, precision=HIGHEST) + carry
        carry = blk[:, LANES - 1:LANES]
        ft_ref[0, :, cols] = blk
        fcol_ref[cols, :] = jnp.concatenate([blk, pad], axis=0).T


def _fox_cumsum(lf_t, b, t):
    return pl.pallas_call(
        _cumsum_kernel,
        grid=(b,),
        in_specs=[pl.BlockSpec((1, N_HEADS, t), lambda i: (i, 0, 0))],
        out_specs=[pl.BlockSpec((1, N_HEADS, t), lambda i: (i, 0, 0)),
                   pl.BlockSpec((t, LANES), lambda i: (i, 0))],
        out_shape=[jax.ShapeDtypeStruct((b, N_HEADS, t), F32), jax.ShapeDtypeStruct((b * t, LANES), F32)],
        compiler_params=_cparams("arbitrary"),
        name="fox_cumsum",
    )(lf_t)


def _attn_prompt_kernel(*refs, fox, t):
    if fox:
        q_ref, k_ref, v_ref, fcol_ref, ft_ref, o_ref, kaug, vaug = refs
    else:
        q_ref, k_ref, v_ref, o_ref, kaug, vaug = refs
    blk = MOBA_BLOCK
    nb = t // blk
    pair = pl.program_id(1)
    c_exp = (HEAD_DIM ** -0.5) * LOG2E
    row_t = lax.broadcasted_iota(jnp.int32, (LANES, t), 0)
    low_t = row_t < HEAD_DIM
    r64 = row_t & (HEAD_DIM - 1)
    kf = k_ref[0]
    vf = v_ref[0]
    if fox:
        augs = []
        for hd in range(2):
            hi, mid, lo = _split3(ft_ref[0, pl.ds(2 * pair + hd, 1), :] * LOG2E)
            augs.append(jnp.where(r64 < 3, 1.0, jnp.where(r64 == 3, -hi, jnp.where(
                r64 == 4, -mid, jnp.where(r64 == 5, -lo, 0.0)))))
        kaug[0] = jnp.where(low_t, kf, augs[0]).astype(BF16)
        kaug[1] = jnp.where(low_t, augs[1], kf).astype(BF16)
    else:
        lane_t = lax.broadcasted_iota(jnp.int32, (LANES, t), 1)
        key_blk = lane_t >> (blk.bit_length() - 1)
        ind = jnp.where((key_blk == r64) & (r64 < nb), 1.0, 0.0)
        kaug[0] = jnp.where(low_t, kf, ind).astype(BF16)
        kaug[1] = jnp.where(low_t, ind, kf).astype(BF16)
        hs = 8 * pl.cdiv(nb, 8)
        l128 = lax.broadcasted_iota(jnp.int32, (LANES, LANES), 1)
        r128 = lax.broadcasted_iota(jnp.int32, (LANES, LANES), 0)
        kcols = jnp.zeros((LANES, LANES), F32)
        for n in range(nb):
            col = jnp.mean(kf[:, n * blk:(n + 1) * blk], axis=1, keepdims=True)
            kcols = jnp.where((l128 == n) | (l128 == hs + n), col, kcols)
        kmean = jnp.where(((r128 < hs) & (l128 < HEAD_DIM)) | ((r128 >= hs) & (l128 >= HEAD_DIM)), kcols.T, 0.0)
        blk_id = lax.broadcasted_iota(jnp.int32, (hs, blk), 0)
        fill = jnp.zeros((HEAD_DIM - hs, blk), F32)
    vaug[0] = jnp.where(low_t, vf, jnp.where(row_t == HEAD_DIM, 1.0, 0.0)).astype(BF16)
    vaug[1] = jnp.where(low_t, jnp.where(row_t == 0, 1.0, 0.0), vf).astype(BF16)

    lane = lax.broadcasted_iota(jnp.int32, (blk, LANES), 1)
    low = lane < HEAD_DIM
    l64 = lane & (HEAD_DIM - 1)
    causal = (lax.broadcasted_iota(jnp.int32, (blk, blk), 1) <= lax.broadcasted_iota(jnp.int32, (blk, blk), 0))

    def q_block(qi):
        q0 = qi * blk
        q = q_ref[q0:q0 + blk, :]
        if fox:
            fq = fcol_ref[q0:q0 + blk, :] * LOG2E
            parts = []
            for hd in range(2):
                col = jnp.sum(jnp.where(lane == 2 * pair + hd, fq, 0.0), axis=1, keepdims=True)
                hi, mid, lo = _split3(col)
                parts.append(jnp.where(l64 == 0, hi, jnp.where(l64 == 1, mid, jnp.where(
                    l64 == 2, lo, jnp.where(l64 < 6, 1.0, 0.0)))))
            aug = jnp.where(low, parts[1], parts[0])
        else:
            gate_t = _dot_nt(kmean, q, precision=HIGHEST)
            aug_t = []
            for hd in range(2):
                gate = gate_t[hd * hs:(hd + 1) * hs, :]
                beaten = jnp.zeros((hs, blk), F32)
                for n2 in range(qi):
                    g2 = gate[n2:n2 + 1, :]
                    better = (g2 > gate) | ((g2 == gate) & (n2 < blk_id))
                    beaten = beaten + jnp.where(better, 1.0, 0.0)
                chosen = ((beaten < MOBA_TOPK) & (blk_id < qi)) | (blk_id == qi)
                aug_t.append(jnp.where((blk_id < nb) & jnp.logical_not(chosen), NEG, 0.0))
            aug = jnp.concatenate([aug_t[1], fill, aug_t[0], fill], axis=0).T
        qc = q * c_exp
        qaug = (jnp.where(low, qc, aug).astype(BF16), jnp.where(low, aug, qc).astype(BF16))

        own = slice(q0, q0 + blk)
        outs = []
        for hd in range(2):
            s_own = jnp.where(causal, _dot(qaug[hd], kaug[hd, :, own]), NEG)
            m = jnp.max(s_own, axis=1, keepdims=True)
            if qi > 0:
                s_prev = _dot(qaug[hd], kaug[hd, :, 0:q0])
                m = jnp.maximum(m, jnp.max(s_prev, axis=1, keepdims=True))
                acc = _dot_nt(jnp.exp2(s_prev - m).astype(BF16), vaug[hd, :, 0:q0])
            else:
                acc = jnp.zeros((blk, LANES), F32)
            acc = acc + _dot_nt(jnp.exp2(s_own - m).astype(BF16), vaug[hd, :, own])
            denom = jnp.sum(jnp.where(lane == (HEAD_DIM if hd == 0 else 0), acc, 0.0), axis=1, keepdims=True)
            outs.append(acc / denom)
        o_ref[own, :] = jnp.where(low, outs[0], outs[1])

    for qi in range(nb):
        q_block(qi)


def _attn_prompt(q, k_t, v_t, b, t, fox, ft=None, fcol=None):
    pairs = ATT_W // LANES
    slab = pl.BlockSpec((t, LANES), lambda i, p: (i, p))
    slab_t = pl.BlockSpec((1, LANES, t), lambda i, p: (i, p, 0))
    in_specs = [slab, slab_t, slab_t]
    args = [q, k_t, v_t]
    if fox:
        in_specs += [pl.BlockSpec((t, LANES), lambda i, p: (i, 0)),
                     pl.BlockSpec((1, N_HEADS, t), lambda i, p: (i, 0, 0))]
        args += [fcol, ft]
    return pl.pallas_call(
        functools.partial(_attn_prompt_kernel, fox=fox, t=t),
        grid=(b, pairs),
        in_specs=in_specs,
        out_specs=slab,
        out_shape=jax.ShapeDtypeStruct((b * t, ATT_W), F32),
        scratch_shapes=[pltpu.VMEM((2, LANES, t), BF16), pltpu.VMEM((2, LANES, t), BF16)],
        compiler_params=_cparams("arbitrary", "arbitrary"),
        name="fox_prompt_attn" if fox else "moba_prompt_attn",
    )(*args)


def _post_kernel(*refs, ssm):
    if ssm:
        x_ref, y_ref, z_ref, gn_ref, wmix_ref = refs[:5]
        rest = refs[5:]
    else:
        x_ref, oa_ref, ob_ref, wmix_ref = refs[:4]
        rest = refs[4:]
    gt1_ref, gpost_ref, gpre_ref, sc2_ref, sh2_ref, gt2_ref, gfpost_ref, win_ref, wout_ref, o_ref = rest
    if ssm:
        y = y_ref[...] * _silu(z_ref[...])
        gs = D_INNER // SSM_GROUPS
        parts = []
        for g in range(SSM_GROUPS):
            yg = y[:, g * gs:(g + 1) * gs]
            parts.append(yg * lax.rsqrt(jnp.mean(yg * yg, axis=-1, keepdims=True) + EPS))
        yn = (jnp.concatenate(parts, axis=-1) * gn_ref[...]).astype(BF16)
        mix = _dot(yn, wmix_ref[...])
    else:
        mix = (_dot(oa_ref[...].astype(BF16), wmix_ref[0:ATT_W, :])
               + _dot(ob_ref[...].astype(BF16), wmix_ref[ATT_W:2 * ATT_W, :]))
    x1 = x_ref[...] + gt1_ref[0] * _rms(mix, gpost_ref[...])
    h2 = (_rms(x1, gpre_ref[...]) * (1.0 + sc2_ref[0]) + sh2_ref[0]).astype(BF16)
    acc = None
    for lo, hi in ((0, FFN_SPLIT), (FFN_SPLIT, D_FF)):
        gate = _dot(h2, win_ref[:, lo:hi])
        up = _dot(h2, win_ref[:, D_FF + lo:D_FF + hi])
        part = _dot((_silu(gate) * up).astype(BF16), wout_ref[lo:hi, :])
        acc = part if acc is None else acc + part
    o_ref[...] = x1 + gt2_ref[0] * _rms(acc, gfpost_ref[...])


def _post(x2d, mix_in, wmix, mods, gains, win, wout, tm, mod_map, ssm, gn=None):
    m = x2d.shape[0]
    row = lambda i: (i, 0)
    gt1, sc2, sh2, gt2 = mods
    gpost, gpre, gfpost = gains
    mod_spec = pl.BlockSpec((1,) + gt1.shape[1:], mod_map)
    vec = _const_spec((1, D_MODEL))
    in_specs = [pl.BlockSpec((tm, D_MODEL), row)]
    args = [x2d]
    for a in mix_in:
        in_specs.append(pl.BlockSpec((tm, a.shape[1]), row))
        args.append(a)
    if ssm:
        in_specs.append(_const_spec((1, D_INNER)))
        args.append(gn)
    in_specs += [_const_spec(wmix.shape), mod_spec, vec, vec, mod_spec, mod_spec, mod_spec, vec,
                 _const_spec(win.shape), _const_spec(wout.shape)]
    args += [wmix, gt1, gpost, gpre, sc2, sh2, gt2, gfpost, win, wout]
    return pl.pallas_call(
        functools.partial(_post_kernel, ssm=ssm),
        grid=(m // tm,),
        in_specs=in_specs,
        out_specs=pl.BlockSpec((tm, D_MODEL), row),
        out_shape=jax.ShapeDtypeStruct((m, D_MODEL), F32),
        compiler_params=_cparams("arbitrary"),
        name="ssm_out_ffn" if ssm else "attn_out_ffn",
    )(*args)


def _ssm_proj_kernel(*refs, decode, tiles_per_seq):
    if decode:
        (x_ref, sc_ref, sh_ref, g_ref, wz_ref, wx_ref, wdt_ref, dtb_ref, cw_ref, cb_ref, prev_ref,
         z_ref, xc_ref, dt_ref, cs_ref) = refs
    else:
        (x_ref, sc_ref, sh_ref, g_ref, wz_ref, wx_ref, wdt_ref, dtb_ref, cw_ref, cb_ref,
         z_ref, xc_ref, dt_ref, cs_ref, ubuf) = refs
    tm = x_ref.shape[0]
    h = (_rms(x_ref[...], g_ref[...]) * (1.0 + sc_ref[0]) + sh_ref[0]).astype(BF16)
    z_ref[...] = _dot_nt(h, wz_ref[...])
    dt_ref[...] = _softplus(_dot_nt(h, wdt_ref[...]) + dtb_ref[...])
    if decode:
        u = _dot_nt(h, wx_ref[...])
        w = [cw_ref[j:j + 1, :] for j in range(CONV_W)]
        p0, p1, p2 = prev_ref[0], prev_ref[1], prev_ref[2]
        y = cb_ref[...] + w[3] * u + w[2] * p2 + w[1] * p1 + w[0] * p0
        cs_ref[0] = p1
        cs_ref[1] = p2
        cs_ref[2] = u
        xc_ref[...] = _silu(y)
    else:
        pad = 8
        tail = CONV_W - 1
        @pl.when(pl.program_id(0) % tiles_per_seq == 0)
        def _():
            ubuf[0:pad, :] = jnp.zeros((pad, CONV_DIM), F32)
        cw = CONV_DIM // 6
        for c in range(CONV_DIM // cw):
            cols = slice(c * cw, (c + 1) * cw)
            u = _dot_nt(h, wx_ref[cols, :])
            ubuf[pad:pad + tm, cols] = u
            y = cb_ref[:, cols] + cw_ref[tail:CONV_W, cols] * u
            for j in range(tail):
                y = y + cw_ref[j:j + 1, cols] * ubuf[pad - tail + j:pad - tail + j + tm, cols]
            xc_ref[:, cols] = _silu(y)
        cs_ref[:, 0, 0, :] = ubuf[pad + tm - tail:pad + tm, :]
        ubuf[0:pad, :] = ubuf[tm:tm + pad, :]


def _ssm_proj(x2d, sc, sh, g, wz, wx, wdt, dtb, cw, cb, tm, mod_map, seq_len, prev=None):
    m = x2d.shape[0]
    decode = prev is not None
    row = lambda i: (i, 0)
    mod_spec = pl.BlockSpec((1,) + sc.shape[1:], mod_map)
    in_specs = [pl.BlockSpec((tm, D_MODEL), row), mod_spec, mod_spec, _const_spec((1, D_MODEL)),
                _const_spec(wz.shape), _const_spec(wx.shape), _const_spec(wdt.shape), _const_spec((1, LANES)),
                _const_spec(cw.shape), _const_spec((1, CONV_DIM))]
    args = [x2d, sc, sh, g, wz, wx, wdt, dtb, cw, cb]
    scratch = []
    if decode:
        in_specs.append(_const_spec(prev.shape))
        args.append(prev)
        cs_shape = prev.shape
        cs_spec = _const_spec(prev.shape)
        tiles_per_seq = 1
    else:
        tiles_per_seq = seq_len // tm
        nseq = m // seq_len
        cs_shape = (CONV_W - 1, nseq, 1, CONV_DIM)
        cs_spec = pl.BlockSpec((CONV_W - 1, 1, 1, CONV_DIM), lambda i: (0, i // tiles_per_seq, 0, 0))
        scratch = [pltpu.VMEM((tm + 8, CONV_DIM), F32)]
    z, xc, dt, cs = pl.pallas_call(
        functools.partial(_ssm_proj_kernel, decode=decode, tiles_per_seq=tiles_per_seq),
        grid=(m // tm,),
        in_specs=in_specs,
        out_specs=[pl.BlockSpec((tm, D_INNER), row), pl.BlockSpec((tm, CONV_DIM), row),
                   pl.BlockSpec((tm, LANES), row), cs_spec],
        out_shape=[jax.ShapeDtypeStruct((m, D_INNER), F32), jax.ShapeDtypeStruct((m, CONV_DIM), F32),
                   jax.ShapeDtypeStruct((m, LANES), F32), jax.ShapeDtypeStruct(cs_shape, F32)],
        scratch_shapes=scratch,
        compiler_params=_cparams("arbitrary"),
        name="ssm_in_proj_decode" if decode else "ssm_in_proj",
    )(*args)
    return z, xc, dt, cs.reshape(CONV_W - 1, -1, CONV_DIM)


def _head_expand():
    e = np.zeros((LANES, D_INNER), np.float32)
    for hh in range(SSM_HEADS):
        e[hh, hh * SSM_HEAD_DIM:(hh + 1) * SSM_HEAD_DIM] = 1.0
    return e


def _spread(x, e, terms=3):
    parts = _split3(x)[:terms]
    out = _dot(parts[0].astype(BF16), e)
    for t in parts[1:]:
        out = out + _dot(t.astype(BF16), e)
    return out


def _spread_rows(e, x):
    hi, mid, lo = _split3(x)
    return _dot(e, hi.astype(BF16)) + _dot(e, mid.astype(BF16)) + _dot(e, lo.astype(BF16))


def _ssd_kernel(xs_ref, b_ref, c_ref, dt_ref, alog_ref, dsk_ref, e_ref, et_ref, y_ref, ht_ref, hst):
    q = SSD_CHUNK
    gw = D_INNER // SSM_GROUPS
    ci = pl.program_id(1)

    @pl.when(ci == 0)
    def _():
        hst[...] = jnp.zeros_like(hst)

    r = lax.broadcasted_iota(jnp.int32, (q, q), 0)
    c = lax.broadcasted_iota(jnp.int32, (q, q), 1)
    causal = r >= c
    lane = lax.broadcasted_iota(jnp.int32, (q, LANES), 1)
    first_half = lane < SSM_HEAD_DIM
    a = -jnp.exp(alog_ref[...])
    expand = e_ref[...]
    dsk = dsk_ref[...]
    for sub in range(xs_ref.shape[0] // q):
        _ssd_chunk(slice(sub * q, (sub + 1) * q), xs_ref, b_ref, c_ref, dt_ref, y_ref, hst, et_ref,
                   a, expand, dsk, causal, first_half)

    @pl.when(ci == pl.num_programs(1) - 1)
    def _():
        ht_ref[0] = hst[...]


def _ssd_chunk(ts, xs_ref, b_ref, c_ref, dt_ref, y_ref, hst, et_ref, a, expand, dsk, causal, first_half):
    q = SSD_CHUNK
    gw = D_INNER // SSM_GROUPS
    dt = dt_ref[ts, :]
    acum = _dot(causal.astype(F32), dt * a, precision=HIGHEST)
    acum_t = acum.T
    dt_t = dt.T
    a_last = acum[q - 1:q, :]
    w_full = _spread(jnp.exp(a_last - acum) * dt, expand, terms=2)
    ea_full = _spread(jnp.exp(acum), expand, terms=2)
    dec_rows = _spread_rows(et_ref[...], jnp.exp(jnp.broadcast_to(acum_t[:, q - 1:q], (LANES, LANES))))
    xs = xs_ref[ts, :]
    xs_bf = xs.astype(BF16)
    xw = xs * w_full
    for g in range(SSM_GROUPS):
        bg = b_ref[ts, g * D_STATE:(g + 1) * D_STATE].astype(BF16)
        cg = c_ref[ts, g * D_STATE:(g + 1) * D_STATE].astype(BF16)
        cb = _dot_nt(cg, bg)
        rows = slice(g * gw, (g + 1) * gw)
        h_in = hst[rows, :]
        y_inter = _dot_nt(cg, h_in.astype(BF16))
        parts = []
        for pr in range(SSM_HPG // 2):
            h0 = g * SSM_HPG + 2 * pr
            x_pair = xs_bf[:, h0 * SSM_HEAD_DIM:(h0 + 2) * SSM_HEAD_DIM]
            outs = []
            for hh in (h0, h0 + 1):
                seg = acum[:, hh:hh + 1] - acum_t[hh:hh + 1, :]
                mh = cb * jnp.exp(jnp.where(causal, seg, -jnp.inf)) * dt_t[hh:hh + 1, :]
                outs.append(_dot(mh.astype(BF16), x_pair))
            parts.append(jnp.where(first_half, outs[0], outs[1]))
        y_intra = jnp.concatenate(parts, axis=-1)
        y_ref[ts, rows] = y_intra + y_inter * ea_full[:, rows] + dsk[:, rows] * xs[:, rows]
        s_inc = _dot(xw[:, rows].T.astype(BF16), bg)
        hst[rows, :] = h_in * dec_rows[rows, :] + s_inc


def _ssd_prompt(xc, dt, alog, dsk, b, t):
    step_rows = SSD_CHUNKS_PER_STEP * SSD_CHUNK
    nc = t // step_rows
    e = jnp.asarray(_head_expand(), BF16)
    et = jnp.asarray(_head_expand().T.copy(), BF16)
    rowmap = lambda i, c: (i * nc + c, 0)
    bc_w = SSM_GROUPS * D_STATE
    return pl.pallas_call(
        _ssd_kernel,
        grid=(b, nc),
        in_specs=[pl.BlockSpec((step_rows, D_INNER), rowmap),
                  pl.BlockSpec((step_rows, bc_w), lambda i, c: (i * nc + c, D_INNER // bc_w)),
                  pl.BlockSpec((step_rows, bc_w), lambda i, c: (i * nc + c, D_INNER // bc_w + 1)),
                  pl.BlockSpec((step_rows, LANES), rowmap),
                  _const_spec((1, LANES)), _const_spec((1, D_INNER)),
                  _const_spec((LANES, D_INNER)), _const_spec((D_INNER, LANES))],
        out_specs=[pl.BlockSpec((step_rows, D_INNER), rowmap),
                   pl.BlockSpec((1, D_INNER, D_STATE), lambda i, c: (i, 0, 0))],
        out_shape=[jax.ShapeDtypeStruct((b * t, D_INNER), F32),
                   jax.ShapeDtypeStruct((b, D_INNER, D_STATE), F32)],
        scratch_shapes=[pltpu.VMEM((D_INNER, D_STATE), F32)],
        compiler_params=_cparams("arbitrary", "arbitrary"),
        name="ssd_scan",
    )(xc, xc, xc, dt, alog, dsk, e, et)


def _ssd_step_kernel(xs_ref, b_ref, c_ref, dt_ref, alog_ref, dsk_ref, e_ref, et_ref, h0_ref, y_ref, ht_ref):
    gw = D_INNER // SSM_GROUPS
    dt = dt_ref[0]
    a = -jnp.exp(alog_ref[...])
    dec = jnp.exp(dt * a)
    expand = e_ref[...]
    eye = (lax.broadcasted_iota(jnp.int32, (LANES, LANES), 0)
           == lax.broadcasted_iota(jnp.int32, (LANES, LANES), 1))
    dec_col = jnp.sum(jnp.where(eye, jnp.broadcast_to(dec, (LANES, LANES)), 0.0), axis=1, keepdims=True)
    dec_rows = _spread_rows(et_ref[...], jnp.broadcast_to(dec_col, (LANES, LANES)))
    xs = xs_ref[0]
    dtx = xs * _spread(dt, expand)
    eye_g = (lax.broadcasted_iota(jnp.int32, (gw, gw), 0) == lax.broadcasted_iota(jnp.int32, (gw, gw), 1))
    bm = b_ref[0]
    cm = c_ref[0]
    ys = []
    for g in range(SSM_GROUPS):
        rows = slice(g * gw, (g + 1) * gw)
        bg = bm[:, g * D_STATE:(g + 1) * D_STATE]
        cg = cm[:, g * D_STATE:(g + 1) * D_STATE]
        diag = jnp.where(eye_g, jnp.broadcast_to(dtx[:, rows], (gw, gw)), 0.0).astype(BF16)
        outer = _dot(diag, jnp.broadcast_to(bg, (gw, D_STATE)).astype(BF16))
        h_new = h0_ref[0, rows, :] * dec_rows[rows, :] + outer
        ht_ref[0, rows, :] = h_new
        yg = _dot_nt(jnp.broadcast_to(cg, (8, D_STATE)).astype(BF16), h_new.astype(BF16))
        ys.append(yg[0:1, :])
    y_ref[0] = jnp.concatenate(ys, axis=-1) + dsk_ref[...] * xs


def _ssd_step(xc, dt, alog, dsk, h0):
    b = xc.shape[0]
    e = jnp.asarray(_head_expand(), BF16)
    et = jnp.asarray(_head_expand().T.copy(), BF16)
    bc_w = SSM_GROUPS * D_STATE
    xc3 = xc.reshape(b, 1, CONV_DIM)
    return pl.pallas_call(
        _ssd_step_kernel,
        grid=(b,),
        in_specs=[pl.BlockSpec((1, 1, D_INNER), lambda i: (i, 0, 0)),
                  pl.BlockSpec((1, 1, bc_w), lambda i: (i, 0, D_INNER // bc_w)),
                  pl.BlockSpec((1, 1, bc_w), lambda i: (i, 0, D_INNER // bc_w + 1)),
                  pl.BlockSpec((1, 1, LANES), lambda i: (i, 0, 0)),
                  _const_spec((1, LANES)), _const_spec((1, D_INNER)),
                  _const_spec((LANES, D_INNER)), _const_spec((D_INNER, LANES)),
                  pl.BlockSpec((1, D_INNER, D_STATE), lambda i: (i, 0, 0))],
        out_specs=[pl.BlockSpec((1, 1, D_INNER), lambda i: (i, 0, 0)),
                   pl.BlockSpec((1, D_INNER, D_STATE), lambda i: (i, 0, 0))],
        out_shape=[jax.ShapeDtypeStruct((b, 1, D_INNER), F32),
                   jax.ShapeDtypeStruct((b, D_INNER, D_STATE), F32)],
        compiler_params=_cparams("arbitrary"),
        name="ssd_step",
    )(xc3, xc3, xc3, dt.reshape(b, 1, LANES), alog, dsk, e, et, h0)


def _fox_decode_kernel(pt_ref, qrow_ref, kn_ref, vn_ref, lfn_ref, *refs):
    n = PAGES_PER_STEP
    k_refs, v_refs, lf_refs = refs[:n], refs[n:2 * n], refs[2 * n:3 * n]
    o_ref, m_s, l_s, r_s, acc_s, qbd_s = refs[3 * n:]
    j = pl.program_id(1)
    rows8 = lax.broadcasted_iota(jnp.int32, (N_HEADS, HEAD_DIM), 0)

    @pl.when(j == 0)
    def _():
        on_diag = (lax.broadcasted_iota(jnp.int32, (N_HEADS, ATT_W), 1) >> (HEAD_DIM.bit_length() - 1)
                   == lax.broadcasted_iota(jnp.int32, (N_HEADS, ATT_W), 0))
        qrow = qrow_ref[0] * (HEAD_DIM ** -0.5)
        qbd_s[...] = jnp.where(on_diag, jnp.broadcast_to(qrow, (N_HEADS, ATT_W)), 0.0).astype(BF16)
        m_s[...] = jnp.sum(jnp.where(on_diag, jnp.broadcast_to(qrow * kn_ref[0], (N_HEADS, ATT_W)), 0.0),
                           axis=1, keepdims=True)
        l_s[...] = jnp.ones_like(l_s)
        r_s[...] = lfn_ref[0]
        acc_s[...] = vn_ref[0]

    r = lax.broadcasted_iota(jnp.int32, (LANES, LANES), 0)
    c = lax.broadcasted_iota(jnp.int32, (LANES, LANES), 1)
    triu = (r <= c).astype(F32)
    run = r_s[...]
    bias = []
    for i in range(n):
        cs = _dot(lf_refs[i][...], triu, precision=HIGHEST)
        total = cs[:, PAGE_SIZE - 1:PAGE_SIZE]
        bias.append(run + total - cs)
        run = run + total
    r_s[...] = run
    k_all = jnp.concatenate([k_refs[i][...].reshape(ATT_W, PAGE_SIZE).astype(BF16) for i in range(n)], axis=1)
    s = _dot(qbd_s[...], k_all) + jnp.concatenate(bias, axis=1)
    m_old = m_s[...]
    m_new = jnp.maximum(m_old, jnp.max(s, axis=1, keepdims=True))
    alpha = jnp.exp(m_old - m_new)
    p = jnp.exp(s - m_new)
    l_s[...] = alpha * l_s[...] + jnp.sum(p, axis=1, keepdims=True)
    m_s[...] = m_new
    pb = p.astype(BF16)
    acc = acc_s[...] * alpha
    for h in range(N_HEADS):
        v_h = jnp.concatenate([v_refs[i][h].astype(BF16) for i in range(n)], axis=1)
        acc = acc + jnp.where(rows8 == h, _dot_nt(pb, v_h), 0.0)
    acc_s[...] = acc

    @pl.when(j == pl.num_programs(1) - 1)
    def _():
        o_ref[0] = acc_s[...] / l_s[...]


def _fox_decode(qrow, kn_row, v_new, lf_new, pool_k, pool_v, pool_lf, page_table):
    b, n_pages = page_table.shape
    n = PAGES_PER_STEP
    steps = n_pages // n
    row3 = lambda i, j, pt: (i, 0, 0)

    def page_spec(shape, off):
        nd = len(shape)
        return pl.BlockSpec((None, None) + shape,
                            lambda i, j, pt: (0, pt[i, n_pages - 1 - (j * n + off)]) + (0,) * nd)

    in_specs = ([pl.BlockSpec((1, 1, ATT_W), row3)] * 2 + [pl.BlockSpec((1, N_HEADS, HEAD_DIM), row3),
                                                          pl.BlockSpec((1, N_HEADS, 1), row3)]
                + [page_spec((N_HEADS, HEAD_DIM, PAGE_SIZE), i) for i in range(n)] * 2
                + [page_spec((N_HEADS, PAGE_SIZE), i) for i in range(n)])
    grid_spec = pltpu.PrefetchScalarGridSpec(
        num_scalar_prefetch=1, grid=(b, steps), in_specs=in_specs,
        out_specs=pl.BlockSpec((1, N_HEADS, HEAD_DIM), row3),
        scratch_shapes=[pltpu.VMEM((N_HEADS, 1), F32)] * 3 + [pltpu.VMEM((N_HEADS, HEAD_DIM), F32),
                                                              pltpu.VMEM((N_HEADS, ATT_W), BF16)])
    return pl.pallas_call(
        _fox_decode_kernel, grid_spec=grid_spec,
        out_shape=jax.ShapeDtypeStruct((b, N_HEADS, HEAD_DIM), F32),
        compiler_params=_cparams("arbitrary", "arbitrary"),
        name="fox_decode_attn",
    )(page_table, qrow, kn_row, v_new, lf_new, *([pool_k] * n), *([pool_v] * n), *([pool_lf] * n))


def _moba_gate_kernel(pt_ref, qcol_ref, *refs):
    n = PAGES_PER_STEP
    k_refs = refs[:n]
    idx_ref, gate_s, qb_s = refs[n:]
    j = pl.program_id(1)
    ppb = MOBA_BLOCK // PAGE_SIZE
    lane = lax.broadcasted_iota(jnp.int32, (N_HEADS, LANES), 1)

    @pl.when(j == 0)
    def _():
        gate_s[...] = jnp.full_like(gate_s, -jnp.inf)
        qb_s[...] = jnp.broadcast_to(qcol_ref[0], (ATT_W, PAGE_SIZE))

    qb = qb_s[...]
    sub = 8
    fold = (lax.broadcasted_iota(jnp.int32, (N_HEADS, N_HEADS * sub), 1) >> 3
            == lax.broadcasted_iota(jnp.int32, (N_HEADS, N_HEADS * sub), 0)).astype(F32)
    gates = gate_s[...]
    for blk in range(n // ppb):
        ksum = None
        for i in range(ppb):
            page = k_refs[blk * ppb + i][...].reshape(ATT_W, PAGE_SIZE)
            ksum = page if ksum is None else ksum + page
        part = jnp.sum((ksum * qb).reshape(N_HEADS, HEAD_DIM // sub, sub, PAGE_SIZE), axis=1)
        per_head = _dot(fold, part.reshape(N_HEADS * sub, PAGE_SIZE), precision=HIGHEST)
        g = jnp.sum(per_head, axis=1, keepdims=True) * (1.0 / MOBA_BLOCK)
        gates = jnp.where(lane == j * (n // ppb) + blk, g, gates)
    gate_s[...] = gates

    @pl.when(j == pl.num_programs(1) - 1)
    def _():
        gate = gate_s[...]
        lane_f = lane.astype(F32)
        picks = jnp.zeros((N_HEADS, LANES), F32)
        for k in range(MOBA_TOPK):
            best = jnp.max(gate, axis=1, keepdims=True)
            first = jnp.min(jnp.where(gate == best, lane_f, float(LANES)), axis=1, keepdims=True)
            picks = jnp.where(lane == k, first, picks)
            gate = jnp.where(lane_f == first, -jnp.inf, gate)
        idx_ref[0] = picks.astype(jnp.int32)


def _moba_gate(qcol, pool_k, page_table):
    b, n_pages = page_table.shape
    n = PAGES_PER_STEP
    steps = n_pages // n
    row3 = lambda i, j, pt: (i, 0, 0)
    in_specs = [pl.BlockSpec((1, ATT_W, 1), row3)] + [
        pl.BlockSpec((None, None, N_HEADS, HEAD_DIM, PAGE_SIZE),
                     functools.partial(lambda i, j, pt, off: (0, pt[i, j * n + off], 0, 0, 0), off=off))
        for off in range(n)]
    grid_spec = pltpu.PrefetchScalarGridSpec(
        num_scalar_prefetch=1, grid=(b, steps), in_specs=in_specs,
        out_specs=pl.BlockSpec((1, N_HEADS, LANES), row3),
        scratch_shapes=[pltpu.VMEM((N_HEADS, LANES), F32), pltpu.VMEM((ATT_W, PAGE_SIZE), F32)])
    return pl.pallas_call(
        _moba_gate_kernel, grid_spec=grid_spec,
        out_shape=jax.ShapeDtypeStruct((b, N_HEADS, LANES), jnp.int32),
        compiler_params=_cparams("arbitrary", "arbitrary"),
        name="moba_decode_gate",
    )(page_table, qcol, *([pool_k] * n))


def _moba_decode_kernel(pt_ref, idx_ref, q_ref, kn_ref, vn_ref, *refs):
    npg = MOBA_TOPK * (MOBA_BLOCK // PAGE_SIZE)
    nh = MOBA_HEADS_PER_STEP
    k_refs, v_refs = refs[:nh * npg], refs[nh * npg:2 * nh * npg]
    o_ref, acc_s = refs[2 * nh * npg:]
    grp = pl.program_id(1)
    q8 = q_ref[0] * (HEAD_DIM ** -0.5)
    q8_bf = q8.astype(BF16)
    rows8 = lax.broadcasted_iota(jnp.int32, (N_HEADS, HEAD_DIM), 0)
    s_new = jnp.sum(q8 * kn_ref[0], axis=1, keepdims=True)

    @pl.when(grp == 0)
    def _():
        acc_s[...] = jnp.zeros_like(acc_s)

    out = acc_s[...]
    for hl in range(nh):
        k_h = jnp.concatenate([k_refs[hl * npg + i][...].astype(BF16) for i in range(npg)], axis=1)
        v_h = jnp.concatenate([v_refs[hl * npg + i][...].astype(BF16) for i in range(npg)], axis=1)
        s = _dot(q8_bf, k_h)
        m = jnp.maximum(jnp.max(s, axis=1, keepdims=True), s_new)
        p = jnp.exp(s - m)
        p_new = jnp.exp(s_new - m)
        denom = jnp.sum(p, axis=1, keepdims=True) + p_new
        out_h = (_dot_nt(p.astype(BF16), v_h) + p_new * vn_ref[0]) / denom
        out = jnp.where(rows8 == grp * nh + hl, out_h, out)
    acc_s[...] = out

    @pl.when(grp == pl.num_programs(1) - 1)
    def _():
        o_ref[0] = out


def _moba_decode(q, k_new, v_new, idx, pool_k, pool_v, page_table):
    b, _ = page_table.shape
    ppb = MOBA_BLOCK // PAGE_SIZE
    npg = MOBA_TOPK * ppb
    nh = MOBA_HEADS_PER_STEP
    row3 = lambda i, g, pt, ix: (i, 0, 0)

    def page(hl, k, off):
        def index(i, g, pt, ix):
            h = g * nh + hl
            return (0, pt[i, ix[i, h * MOBA_TOPK + k] * ppb + off], h, 0, 0)
        return pl.BlockSpec((None, None, None, HEAD_DIM, PAGE_SIZE), index)

    slabs = [page(hl, k, off) for hl in range(nh) for k in range(MOBA_TOPK) for off in range(ppb)]
    in_specs = [pl.BlockSpec((1, N_HEADS, HEAD_DIM), row3)] * 3 + slabs * 2
    grid_spec = pltpu.PrefetchScalarGridSpec(
        num_scalar_prefetch=2, grid=(b, N_HEADS // nh), in_specs=in_specs,
        out_specs=pl.BlockSpec((1, N_HEADS, HEAD_DIM), row3),
        scratch_shapes=[pltpu.VMEM((N_HEADS, HEAD_DIM), F32)])
    return pl.pallas_call(
        _moba_decode_kernel, grid_spec=grid_spec,
        out_shape=jax.ShapeDtypeStruct((b, N_HEADS, HEAD_DIM), F32),
        compiler_params=_cparams("arbitrary", "arbitrary"),
        name="moba_decode_attn",
    )(page_table, idx, q, k_new, v_new, *([pool_k] * len(slabs)), *([pool_v] * len(slabs)))


def _pad_lanes(a, width=LANES):
    return jnp.pad(a, [(0, 0)] * (a.ndim - 1) + [(0, width - a.shape[-1])])


def _run_group(x, ada, weights, decode, caches=None):
    b, t, _ = x.shape
    m = b * t
    x2d = x.reshape(m, D_MODEL)
    if decode:
        tm = m
        mod_map = lambda i: (0, 0, 0)
        as_mod = lambda a: a.reshape(1, b, D_MODEL)
    else:
        tm = 512
        tps = t // tm
        mod_map = lambda i: (i // tps, 0, 0)
        as_mod = lambda a: a.reshape(b, 1, D_MODEL)
    row = lambda v: v.reshape(1, -1)

    sh1, sc1, gt1, sh2, sc2, gt2 = [as_mod(a) for a in jnp.split(ada[0], 6, axis=-1)]
    pos = (caches["page_table"].shape[1] * PAGE_SIZE if decode else 0) + np.arange(t)
    qa, qb, ka, va, kb, vb, lf = _attn_proj(
        x2d, sc1, sh1, row(weights["g_mix_pre"][0]), weights["wt_att6"], weights["wt_att_f"],
        weights["b_fox_f"], pos, tm, mod_map, t, prompt=not decode)
    if decode:
        hd = lambda a: a.reshape(b, N_HEADS, HEAD_DIM)
        pt = caches["page_table"]
        idx = _moba_gate(qa.reshape(b, ATT_W, 1), caches["moba_k"], pt)[:, :, :MOBA_TOPK]
        oa = _moba_decode(hd(qa), hd(ka), hd(va), idx.reshape(b, N_HEADS * MOBA_TOPK),
                          caches["moba_k"], caches["moba_v"], pt)
        ob = _fox_decode(qb.reshape(b, 1, ATT_W), kb.reshape(b, 1, ATT_W), hd(vb), lf.reshape(b, N_HEADS, 1),
                         caches["fox_k"], caches["fox_v"], caches["fox_lf"], pt)
        oa, ob = oa.reshape(m, ATT_W), ob.reshape(m, ATT_W)
        rows5 = lambda a: a.reshape(1, b, t, N_HEADS, HEAD_DIM)
        lf_out = lf.reshape(1, b, t, N_HEADS)
    else:
        ft, fcol = _fox_cumsum(lf, b, t)
        oa = _attn_prompt(qa, ka, va, b, t, fox=False)
        ob = _attn_prompt(qb, kb, vb, b, t, fox=True, ft=ft, fcol=fcol)
        rows5 = lambda a: a.reshape(1, b, N_HEADS, HEAD_DIM, t).transpose(0, 1, 4, 2, 3)
        lf_out = lf.reshape(1, b, N_HEADS, t).transpose(0, 1, 3, 2)
    x2d = _post(x2d, [oa, ob], weights["w_att_out"], (gt1, sc2, sh2, gt2),
                (row(weights["g_mix_post"][0]), row(weights["g_ffn_pre"][0]), row(weights["g_ffn_post"][0])),
                weights["w_ffn_in"][0], weights["w_ffn_out"][0], tm, mod_map, ssm=False)

    sh1, sc1, gt1, sh2, sc2, gt2 = [as_mod(a) for a in jnp.split(ada[1], 6, axis=-1)]
    prev = caches["state_conv"] if decode else None
    z, xc, dt, conv_state = _ssm_proj(
        x2d, sc1, sh1, row(weights["g_mix_pre"][1]), weights["wt_ssm_z"], weights["wt_ssm_x"],
        weights["wt_ssm_dt"], weights["dt_bias"], weights["conv_w"], weights["conv_b"], tm, mod_map, t, prev=prev)
    if decode:
        y, h_t = _ssd_step(xc, dt, weights["a_log"], weights["d_skip"], caches["state_ssm"])
        y = y.reshape(m, D_INNER)
    else:
        y, h_t = _ssd_prompt(xc, dt, weights["a_log"], weights["d_skip"], b, t)
    x2d = _post(x2d, [y, z], weights["w_ssm_out"], (gt1, sc2, sh2, gt2),
                (row(weights["g_mix_post"][1]), row(weights["g_ffn_pre"][1]), row(weights["g_ffn_post"][1])),
                weights["w_ffn_in"][1], weights["w_ffn_out"][1], tm, mod_map, ssm=True,
                gn=row(weights["g_ssm_norm"]))

    return (x2d.reshape(b, t, D_MODEL), rows5(ka), rows5(va), rows5(kb), rows5(vb), lf_out,
            h_t.reshape(1, b, SSM_HEADS, SSM_HEAD_DIM, D_STATE),
            conv_state.transpose(1, 0, 2).reshape(1, b, CONV_W - 1, CONV_DIM))


def kernel(x_prompt, x_sample, cache_moba_k, cache_moba_v, cache_fox_k, cache_fox_v, cache_fox_logf, state_ssm, state_conv, page_table, c_prompt, c_sample, w_ada, b_ada, g_mix_pre, g_mix_post, g_ffn_pre, g_ffn_post, w_att_in, b_fox_f, w_att_out, w_ssm_in, conv_w, conv_b, dt_bias, a_log, d_skip, g_ssm_norm, w_ssm_out, w_ffn_in, w_ffn_out):
    bp = x_prompt.shape[0]
    bs = x_sample.shape[0]
    ada = _ada(jnp.concatenate([c_prompt, c_sample], axis=0), w_ada, b_ada)
    wt_att = w_att_in[0].T
    wt_ssm = w_ssm_in[0].T
    n_dt = wt_ssm.shape[0] - D_INNER - CONV_DIM
    weights = dict(
        g_mix_pre=g_mix_pre, g_mix_post=g_mix_post, g_ffn_pre=g_ffn_pre, g_ffn_post=g_ffn_post,
        wt_att6=wt_att[:6 * ATT_W].astype(BF16),
        wt_att_f=wt_att[6 * ATT_W:],
        b_fox_f=b_fox_f[0],
        w_att_out=w_att_out[0].astype(BF16),
        wt_ssm_z=wt_ssm[:D_INNER].astype(BF16),
        wt_ssm_x=wt_ssm[D_INNER:D_INNER + CONV_DIM].astype(BF16),
        wt_ssm_dt=jnp.pad(wt_ssm[D_INNER + CONV_DIM:], ((0, LANES - n_dt), (0, 0))).astype(BF16),
        dt_bias=_pad_lanes(dt_bias[0].reshape(1, -1)),
        conv_w=conv_w[0], conv_b=conv_b[0].reshape(1, -1),
        a_log=_pad_lanes(a_log[0].reshape(1, -1)),
        d_skip=jnp.repeat(d_skip[0], SSM_HEAD_DIM).reshape(1, -1),
        g_ssm_norm=g_ssm_norm[0],
        w_ssm_out=w_ssm_out[0].astype(BF16),
        w_ffn_in=w_ffn_in.astype(BF16), w_ffn_out=w_ffn_out.astype(BF16),
    )
    kv_t = lambda a: a.transpose(0, 1, 3, 4, 2)
    caches = dict(page_table=page_table, moba_k=kv_t(cache_moba_k), moba_v=kv_t(cache_moba_v),
                  fox_k=kv_t(cache_fox_k), fox_v=kv_t(cache_fox_v),
                  fox_lf=cache_fox_logf.transpose(0, 1, 3, 2),
                  state_ssm=state_ssm[0].reshape(bs, D_INNER, D_STATE),
                  state_conv=state_conv[0].transpose(1, 0, 2))
    prompt = _run_group(x_prompt, ada[:, :bp], weights, decode=False)
    sample = _run_group(x_sample, ada[:, bp:], weights, decode=True, caches=caches)
    return (prompt[0], sample[0]) + prompt[1:] + sample[1:]
```

```python
import functools
import math

import numpy as np
import jax
import jax.numpy as jnp
from jax import lax
from jax.experimental import pallas as pl
from jax.experimental.pallas import tpu as pltpu

F32 = jnp.float32
BF16 = jnp.bfloat16
HIGHEST = lax.Precision.HIGHEST

D_MODEL = 1024
HEAD_DIM = 64
N_HEADS = 8
ATT_W = N_HEADS * HEAD_DIM
ROT_DIM = HEAD_DIM // 4
ROPE_THETA = 500000.0
MOBA_BLOCK = 256
MOBA_TOPK = 3
PAGE_SIZE = 128
D_INNER = 2 * D_MODEL
SSM_HEAD_DIM = 64
SSM_HEADS = D_INNER // SSM_HEAD_DIM
SSM_GROUPS = 4
SSM_HPG = SSM_HEADS // SSM_GROUPS
D_STATE = 128
CONV_W = 4
CONV_DIM = D_INNER + 2 * SSM_GROUPS * D_STATE
SSD_CHUNK = 128
SSD_CHUNKS_PER_STEP = 2
D_FF = ((8 * D_MODEL + 3 * 256 - 1) // (3 * 256)) * 256
MXU_COLS = 256
FFN_SPLIT = (D_FF // MXU_COLS + 1) // 2 * MXU_COLS
EPS = 1e-6
LANES = 128
VMEM_LIMIT = 56 * 1024 * 1024
PAGES_PER_STEP = 16
MOBA_HEADS_PER_STEP = 2
LOG2E = math.log2(math.e)
NEG = -1e30

_NT = (((1,), (1,)), ((), ()))


def _cparams(*sem):
    return pltpu.CompilerParams(dimension_semantics=sem, vmem_limit_bytes=VMEM_LIMIT)


def _const_spec(shape):
    nd = len(shape)
    return pl.BlockSpec(shape, lambda *_: (0,) * nd, pipeline_mode=pl.Buffered(1))


def _rms(x, g):
    return x * lax.rsqrt(jnp.mean(x * x, axis=-1, keepdims=True) + EPS) * g


def _silu(x):
    return x * jax.nn.sigmoid(x)


def _softplus(x):
    return jnp.maximum(x, 0.0) + jnp.log1p(jnp.exp(-jnp.abs(x)))


def _log_sigmoid(x):
    return jnp.minimum(x, 0.0) - jnp.log1p(jnp.exp(-jnp.abs(x)))


def _dot(a, b, **kw):
    return jnp.dot(a, b, preferred_element_type=F32, **kw)


def _dot_nt(a, b, **kw):
    return lax.dot_general(a, b, _NT, preferred_element_type=F32, **kw)


def _split3(x):
    hi = x.astype(BF16).astype(F32)
    r = x - hi
    mid = r.astype(BF16).astype(F32)
    return hi, mid, r - mid


def _ada_kernel(c_ref, w_ref, b_ref, o_ref):
    a = _silu(c_ref[...]).astype(BF16)
    o_ref[0] = _dot(a, w_ref[0].astype(BF16)) + b_ref[0]


def _ada(c_all, w_ada, b_ada):
    depth, _, n = w_ada.shape
    rows = c_all.shape[0]
    tn = 1024
    return pl.pallas_call(
        _ada_kernel,
        grid=(depth, n // tn),
        in_specs=[pl.BlockSpec((rows, D_MODEL), lambda l, j: (0, 0)),
                  pl.BlockSpec((1, D_MODEL, tn), lambda l, j: (l, 0, j)),
                  pl.BlockSpec((1, 1, tn), lambda l, j: (l, 0, j))],
        out_specs=pl.BlockSpec((1, rows, tn), lambda l, j: (l, 0, j)),
        out_shape=jax.ShapeDtypeStruct((depth, rows, n), F32),
        compiler_params=_cparams("arbitrary", "arbitrary"),
        name="ada_terms",
    )(c_all, w_ada, b_ada.reshape(depth, 1, n))


def _rope_tables(pos):
    half = ROT_DIM // 2
    inv = ROPE_THETA ** (-2.0 * np.arange(half, dtype=np.float64) / ROT_DIM)
    ang = np.asarray(pos, np.float64)[:, None] * inv[None, :]
    cos, sin = np.cos(ang), np.sin(ang)
    n = len(pos)
    c64 = np.concatenate([cos, cos, np.ones((n, HEAD_DIM - ROT_DIM))], axis=1)
    s1 = np.concatenate([-sin, np.zeros((n, HEAD_DIM - half))], axis=1)
    s2 = np.concatenate([np.zeros((n, half)), sin, np.zeros((n, HEAD_DIM - ROT_DIM))], axis=1)
    return [t.astype(np.float32) for t in (c64, s1, s2)]


def _rope(z, cos, s1, s2, axis):
    half = ROT_DIM // 2
    period = cos.shape[axis]
    outs = []
    for j in range(z.shape[axis] // period):
        zj = lax.slice_in_dim(z, j * period, (j + 1) * period, axis=axis)
        outs.append(zj * cos + pltpu.roll(zj, period - half, axis) * s1 + pltpu.roll(zj, half, axis) * s2)
    return jnp.concatenate(outs, axis=axis)


def _attn_proj_kernel(*refs, prompt):
    x_ref, sc_ref, sh_ref, g_ref, wt_ref, wtf_ref, bf_ref, cos_ref, s1_ref, s2_ref = refs[:10]
    if prompt:
        cost_ref, s1t_ref, s2t_ref = refs[10:13]
        qa_ref, qb_ref, ka_ref, va_ref, kb_ref, vb_ref, lf_ref = refs[13:]
    else:
        qa_ref, qb_ref, ka_ref, va_ref, kb_ref, vb_ref, lf_ref = refs[10:]
    h = (_rms(x_ref[...], g_ref[...]) * (1.0 + sc_ref[0]) + sh_ref[0]).astype(BF16)
    w = lambda n: wt_ref[n * ATT_W:(n + 1) * ATT_W, :]
    qa_ref[...] = _rope(_dot_nt(h, w(0)), cos_ref[0], s1_ref[0], s2_ref[0], 1)
    qb_ref[...] = _dot_nt(h, w(3))
    wtf = wtf_ref[...].astype(BF16)
    if prompt:
        ka_ref[0] = _rope(_dot_nt(w(1), h), cost_ref[...], s1t_ref[...], s2t_ref[...], 0)
        va_ref[0] = _dot_nt(w(2), h)
        kb_ref[0] = _dot_nt(w(4), h)
        vb_ref[0] = _dot_nt(w(5), h)
        lf_ref[0] = _log_sigmoid(_dot_nt(wtf, h) + bf_ref[...])
    else:
        ka_ref[...] = _rope(_dot_nt(h, w(1)), cos_ref[0], s1_ref[0], s2_ref[0], 1)
        va_ref[...] = _dot_nt(h, w(2))
        kb_ref[...] = _dot_nt(h, w(4))
        vb_ref[...] = _dot_nt(h, w(5))
        lf_ref[...] = _log_sigmoid(_dot_nt(h, wtf) + bf_ref[...])


def _attn_proj(x2d, sc, sh, g, wt6, wtf, bf, pos, tm, mod_map, seq_len, prompt):
    m = x2d.shape[0]
    row = lambda i: (i, 0)
    mod_spec = pl.BlockSpec((1,) + sc.shape[1:], mod_map)
    tabs = _rope_tables(pos)
    wide = jax.ShapeDtypeStruct((m, ATT_W), F32)
    in_specs = [pl.BlockSpec((tm, D_MODEL), row), mod_spec, mod_spec, _const_spec((1, D_MODEL)),
                _const_spec(wt6.shape), _const_spec(wtf.shape)]
    if prompt:
        tps = seq_len // tm
        nseq = m // seq_len
        tab_spec = pl.BlockSpec((1, tm, LANES), lambda i: (i % tps, 0, 0))
        tabt_spec = pl.BlockSpec((HEAD_DIM, tm), lambda i: (0, i % tps))
        fm_spec = pl.BlockSpec((1, ATT_W, tm), lambda i: (i // tps, 0, i % tps))
        fm = jax.ShapeDtypeStruct((nseq, ATT_W, seq_len), F32)
        in_specs += [_const_spec((N_HEADS, 1))] + [tab_spec] * 3 + [tabt_spec] * 3
        args = ([jnp.asarray(np.tile(tb, (1, LANES // HEAD_DIM)).reshape(tps, tm, LANES)) for tb in tabs]
                + [jnp.asarray(np.ascontiguousarray(tb.T)) for tb in tabs])
        out_specs = [pl.BlockSpec((tm, ATT_W), row)] * 2 + [fm_spec] * 4 + [
            pl.BlockSpec((1, N_HEADS, tm), lambda i: (i // tps, 0, i % tps))]
        out_shape = [wide] * 2 + [fm] * 4 + [jax.ShapeDtypeStruct((nseq, N_HEADS, seq_len), F32)]
        bf = bf.reshape(N_HEADS, 1)
    else:
        in_specs += [_const_spec((1, N_HEADS))] + [_const_spec((1, 1, LANES))] * 3
        args = [jnp.asarray(np.tile(tb, (1, LANES // HEAD_DIM)).reshape(1, 1, LANES)) for tb in tabs]
        out_specs = [pl.BlockSpec((tm, ATT_W), row)] * 6 + [pl.BlockSpec((tm, N_HEADS), row)]
        out_shape = [wide] * 6 + [jax.ShapeDtypeStruct((m, N_HEADS), F32)]
        bf = bf.reshape(1, N_HEADS)
    return pl.pallas_call(
        functools.partial(_attn_proj_kernel, prompt=prompt),
        grid=(m // tm,),
        in_specs=in_specs,
        out_specs=out_specs,
        out_shape=out_shape,
        compiler_params=_cparams("arbitrary"),
        name="attn_in_proj" if prompt else "attn_in_proj_decode",
    )(x2d, sc, sh, g, wt6, wtf, bf, *args)


def _cumsum_kernel(lf_ref, ft_ref, fcol_ref):
    t = lf_ref.shape[2]
    r = lax.broadcasted_iota(jnp.int32, (LANES, LANES), 0)
    c = lax.broadcasted_iota(jnp.int32, (LANES, LANES), 1)
    triu = (r <= c).astype(F32)
    carry = jnp.zeros((N_HEADS, 1), F32)
    pad = jnp.zeros((LANES - N_HEADS, LANES), F32)
    for i in range(t // LANES):
        cols = slice(i * LANES, (i + 1) * LANES)
        blk = _dot(lf_ref[0, :, cols], triu, precision=HIGHEST) + carry
        carry = blk[:, LANES - 1:LANES]
        ft_ref[0, :, cols] = blk
        fcol_ref[cols, :] = jnp.concatenate([blk, pad], axis=0).T


def _fox_cumsum(lf_t, b, t):
    return pl.pallas_call(
        _cumsum_kernel,
        grid=(b,),
        in_specs=[pl.BlockSpec((1, N_HEADS, t), lambda i: (i, 0, 0))],
        out_specs=[pl.BlockSpec((1, N_HEADS, t), lambda i: (i, 0, 0)),
                   pl.BlockSpec((t, LANES), lambda i: (i, 0))],
        out_shape=[jax.ShapeDtypeStruct((b, N_HEADS, t), F32), jax.ShapeDtypeStruct((b * t, LANES), F32)],
        compiler_params=_cparams("arbitrary"),
        name="fox_cumsum",
    )(lf_t)


def _attn_prompt_kernel(*refs, fox, t):
    if fox:
        q_ref, k_ref, v_ref, fcol_ref, ft_ref, o_ref, kaug, vaug = refs
    else:
        q_ref, k_ref, v_ref, o_ref, kaug, vaug = refs
    blk = MOBA_BLOCK
    nb = t // blk
    pair = pl.program_id(1)
    c_exp = (HEAD_DIM ** -0.5) * LOG2E
    row_t = lax.broadcasted_iota(jnp.int32, (LANES, t), 0)
    low_t = row_t < HEAD_DIM
    r64 = row_t & (HEAD_DIM - 1)
    kf = k_ref[0]
    vf = v_ref[0]
    if fox:
        augs = []
        for hd in range(2):
            hi, mid, lo = _split3(ft_ref[0, pl.ds(2 * pair + hd, 1), :] * LOG2E)
            augs.append(jnp.where(r64 < 3, 1.0, jnp.where(r64 == 3, -hi, jnp.where(
                r64 == 4, -mid, jnp.where(r64 == 5, -lo, 0.0)))))
        kaug[0] = jnp.where(low_t, kf, augs[0]).astype(BF16)
        kaug[1] = jnp.where(low_t, augs[1], kf).astype(BF16)
    else:
        lane_t = lax.broadcasted_iota(jnp.int32, (LANES, t), 1)
        key_blk = lane_t >> (blk.bit_length() - 1)
        ind = jnp.where((key_blk == r64) & (r64 < nb), 1.0, 0.0)
        kaug[0] = jnp.where(low_t, kf, ind).astype(BF16)
        kaug[1] = jnp.where(low_t, ind, kf).astype(BF16)
        hs = 8 * pl.cdiv(nb, 8)
        l128 = lax.broadcasted_iota(jnp.int32, (LANES, LANES), 1)
        r128 = lax.broadcasted_iota(jnp.int32, (LANES, LANES), 0)
        kcols = jnp.zeros((LANES, LANES), F32)
        for n in range(nb):
            col = jnp.mean(kf[:, n * blk:(n + 1) * blk], axis=1, keepdims=True)
            kcols = jnp.where((l128 == n) | (l128 == hs + n), col, kcols)
        kmean = jnp.where(((r128 < hs) & (l128 < HEAD_DIM)) | ((r128 >= hs) & (l128 >= HEAD_DIM)), kcols.T, 0.0)
        blk_id = lax.broadcasted_iota(jnp.int32, (hs, blk), 0)
        fill = jnp.zeros((HEAD_DIM - hs, blk), F32)
    vaug[0] = jnp.where(low_t, vf, jnp.where(row_t == HEAD_DIM, 1.0, 0.0)).astype(BF16)
    vaug[1] = jnp.where(low_t, jnp.where(row_t == 0, 1.0, 0.0), vf).astype(BF16)

    lane = lax.broadcasted_iota(jnp.int32, (blk, LANES), 1)
    low = lane < HEAD_DIM
    l64 = lane & (HEAD_DIM - 1)
    causal = (lax.broadcasted_iota(jnp.int32, (blk, blk), 1) <= lax.broadcasted_iota(jnp.int32, (blk, blk), 0))

    def q_block(qi):
        q0 = qi * blk
        q = q_ref[q0:q0 + blk, :]
        if fox:
            fq = fcol_ref[q0:q0 + blk, :] * LOG2E
            parts = []
            for hd in range(2):
                col = jnp.sum(jnp.where(lane == 2 * pair + hd, fq, 0.0), axis=1, keepdims=True)
                hi, mid, lo = _split3(col)
                parts.append(jnp.where(l64 == 0, hi, jnp.where(l64 == 1, mid, jnp.where(
                    l64 == 2, lo, jnp.where(l64 < 6, 1.0, 0.0)))))
            aug = jnp.where(low, parts[1], parts[0])
        else:
            gate_t = _dot_nt(kmean, q, precision=HIGHEST)
            aug_t = []
            for hd in range(2):
                gate = gate_t[hd * hs:(hd + 1) * hs, :]
                beaten = jnp.zeros((hs, blk), F32)
                for n2 in range(qi):
                    g2 = gate[n2:n2 + 1, :]
                    better = (g2 > gate) | ((g2 == gate) & (n2 < blk_id))
                    beaten = beaten + jnp.where(better, 1.0, 0.0)
                chosen = ((beaten < MOBA_TOPK) & (blk_id < qi)) | (blk_id == qi)
                aug_t.append(jnp.where((blk_id < nb) & jnp.logical_not(chosen), NEG, 0.0))
            aug = jnp.concatenate([aug_t[1], fill, aug_t[0], fill], axis=0).T
        qc = q * c_exp
        qaug = (jnp.where(low, qc, aug).astype(BF16), jnp.where(low, aug, qc).astype(BF16))

        own = slice(q0, q0 + blk)
        outs = []
        for hd in range(2):
            s_own = jnp.where(causal, _dot(qaug[hd], kaug[hd, :, own]), NEG)
            m = jnp.max(s_own, axis=1, keepdims=True)
            if qi > 0:
                s_prev = _dot(qaug[hd], kaug[hd, :, 0:q0])
                m = jnp.maximum(m, jnp.max(s_prev, axis=1, keepdims=True))
                acc = _dot_nt(jnp.exp2(s_prev - m).astype(BF16), vaug[hd, :, 0:q0])
            else:
                acc = jnp.zeros((blk, LANES), F32)
            acc = acc + _dot_nt(jnp.exp2(s_own - m).astype(BF16), vaug[hd, :, own])
            denom = jnp.sum(jnp.where(lane == (HEAD_DIM if hd == 0 else 0), acc, 0.0), axis=1, keepdims=True)
            outs.append(acc / denom)
        o_ref[own, :] = jnp.where(low, outs[0], outs[1])

    for qi in range(nb):
        q_block(qi)


def _attn_prompt(q, k_t, v_t, b, t, fox, ft=None, fcol=None):
    pairs = ATT_W // LANES
    slab = pl.BlockSpec((t, LANES), lambda i, p: (i, p))
    slab_t = pl.BlockSpec((1, LANES, t), lambda i, p: (i, p, 0))
    in_specs = [slab, slab_t, slab_t]
    args = [q, k_t, v_t]
    if fox:
        in_specs += [pl.BlockSpec((t, LANES), lambda i, p: (i, 0)),
                     pl.BlockSpec((1, N_HEADS, t), lambda i, p: (i, 0, 0))]
        args += [fcol, ft]
    return pl.pallas_call(
        functools.partial(_attn_prompt_kernel, fox=fox, t=t),
        grid=(b, pairs),
        in_specs=in_specs,
        out_specs=slab,
        out_shape=jax.ShapeDtypeStruct((b * t, ATT_W), F32),
        scratch_shapes=[pltpu.VMEM((2, LANES, t), BF16), pltpu.VMEM((2, LANES, t), BF16)],
        compiler_params=_cparams("arbitrary", "arbitrary"),
        name="fox_prompt_attn" if fox else "moba_prompt_attn",
    )(*args)


def _post_kernel(*refs, ssm):
    if ssm:
        x_ref, y_ref, z_ref, gn_ref, wmix_ref = refs[:5]
        rest = refs[5:]
    else:
        x_ref, oa_ref, ob_ref, wmix_ref = refs[:4]
        rest = refs[4:]
    gt1_ref, gpost_ref, gpre_ref, sc2_ref, sh2_ref, gt2_ref, gfpost_ref, win_ref, wout_ref, o_ref = rest
    if ssm:
        y = y_ref[...] * _silu(z_ref[...])
        gs = D_INNER // SSM_GROUPS
        parts = []
        for g in range(SSM_GROUPS):
            yg = y[:, g * gs:(g + 1) * gs]
            parts.append(yg * lax.rsqrt(jnp.mean(yg * yg, axis=-1, keepdims=True) + EPS))
        yn = (jnp.concatenate(parts, axis=-1) * gn_ref[...]).astype(BF16)
        mix = _dot(yn, wmix_ref[...])
    else:
        mix = (_dot(oa_ref[...].astype(BF16), wmix_ref[0:ATT_W, :])
               + _dot(ob_ref[...].astype(BF16), wmix_ref[ATT_W:2 * ATT_W, :]))
    x1 = x_ref[...] + gt1_ref[0] * _rms(mix, gpost_ref[...])
    h2 = (_rms(x1, gpre_ref[...]) * (1.0 + sc2_ref[0]) + sh2_ref[0]).astype(BF16)
    acc = None
    for lo, hi in ((0, FFN_SPLIT), (FFN_SPLIT, D_FF)):
        gate = _dot(h2, win_ref[:, lo:hi])
        up = _dot(h2, win_ref[:, D_FF + lo:D_FF + hi])
        part = _dot((_silu(gate) * up).astype(BF16), wout_ref[lo:hi, :])
        acc = part if acc is None else acc + part
    o_ref[...] = x1 + gt2_ref[0] * _rms(acc, gfpost_ref[...])


def _post(x2d, mix_in, wmix, mods, gains, win, wout, tm, mod_map, ssm, gn=None):
    m = x2d.shape[0]
    row = lambda i: (i, 0)
    gt1, sc2, sh2, gt2 = mods
    gpost, gpre, gfpost = gains
    mod_spec = pl.BlockSpec((1,) + gt1.shape[1:], mod_map)
    vec = _const_spec((1, D_MODEL))
    in_specs = [pl.BlockSpec((tm, D_MODEL), row)]
    args = [x2d]
    for a in mix_in:
        in_specs.append(pl.BlockSpec((tm, a.shape[1]), row))
        args.append(a)
    if ssm:
        in_specs.append(_const_spec((1, D_INNER)))
        args.append(gn)
    in_specs += [_const_spec(wmix.shape), mod_spec, vec, vec, mod_spec, mod_spec, mod_spec, vec,
                 _const_spec(win.shape), _const_spec(wout.shape)]
    args += [wmix, gt1, gpost, gpre, sc2, sh2, gt2, gfpost, win, wout]
    return pl.pallas_call(
        functools.partial(_post_kernel, ssm=ssm),
        grid=(m // tm,),
        in_specs=in_specs,
        out_specs=pl.BlockSpec((tm, D_MODEL), row),
        out_shape=jax.ShapeDtypeStruct((m, D_MODEL), F32),
        compiler_params=_cparams("arbitrary"),
        name="ssm_out_ffn" if ssm else "attn_out_ffn",
    )(*args)


def _ssm_proj_kernel(*refs, decode, tiles_per_seq):
    if decode:
        (x_ref, sc_ref, sh_ref, g_ref, wz_ref, wx_ref, wdt_ref, dtb_ref, cw_ref, cb_ref, prev_ref,
         z_ref, xc_ref, dt_ref, cs_ref) = refs
    else:
        (x_ref, sc_ref, sh_ref, g_ref, wz_ref, wx_ref, wdt_ref, dtb_ref, cw_ref, cb_ref,
         z_ref, xc_ref, dt_ref, cs_ref, ubuf) = refs
    tm = x_ref.shape[0]
    h = (_rms(x_ref[...], g_ref[...]) * (1.0 + sc_ref[0]) + sh_ref[0]).astype(BF16)
    z_ref[...] = _dot_nt(h, wz_ref[...])
    dt_ref[...] = _softplus(_dot_nt(h, wdt_ref[...]) + dtb_ref[...])
    if decode:
        u = _dot_nt(h, wx_ref[...])
        w = [cw_ref[j:j + 1, :] for j in range(CONV_W)]
        p0, p1, p2 = prev_ref[0], prev_ref[1], prev_ref[2]
        y = cb_ref[...] + w[3] * u + w[2] * p2 + w[1] * p1 + w[0] * p0
        cs_ref[0] = p1
        cs_ref[1] = p2
        cs_ref[2] = u
        xc_ref[...] = _silu(y)
    else:
        pad = 8
        tail = CONV_W - 1
        @pl.when(pl.program_id(0) % tiles_per_seq == 0)
        def _():
            ubuf[0:pad, :] = jnp.zeros((pad, CONV_DIM), F32)
        cw = CONV_DIM // 6
        for c in range(CONV_DIM // cw):
            cols = slice(c * cw, (c + 1) * cw)
            u = _dot_nt(h, wx_ref[cols, :])
            ubuf[pad:pad + tm, cols] = u
            y = cb_ref[:, cols] + cw_ref[tail:CONV_W, cols] * u
            for j in range(tail):
                y = y + cw_ref[j:j + 1, cols] * ubuf[pad - tail + j:pad - tail + j + tm, cols]
            xc_ref[:, cols] = _silu(y)
        cs_ref[:, 0, 0, :] = ubuf[pad + tm - tail:pad + tm, :]
        ubuf[0:pad, :] = ubuf[tm:tm + pad, :]


def _ssm_proj(x2d, sc, sh, g, wz, wx, wdt, dtb, cw, cb, tm, mod_map, seq_len, prev=None):
    m = x2d.shape[0]
    decode = prev is not None
    row = lambda i: (i, 0)
    mod_spec = pl.BlockSpec((1,) + sc.shape[1:], mod_map)
    in_specs = [pl.BlockSpec((tm, D_MODEL), row), mod_spec, mod_spec, _const_spec((1, D_MODEL)),
                _const_spec(wz.shape), _const_spec(wx.shape), _const_spec(wdt.shape), _const_spec((1, LANES)),
                _const_spec(cw.shape), _const_spec((1, CONV_DIM))]
    args = [x2d, sc, sh, g, wz, wx, wdt, dtb, cw, cb]
    scratch = []
    if decode:
        in_specs.append(_const_spec(prev.shape))
        args.append(prev)
        cs_shape = prev.shape
        cs_spec = _const_spec(prev.shape)
        tiles_per_seq = 1
    else:
        tiles_per_seq = seq_len // tm
        nseq = m // seq_len
        cs_shape = (CONV_W - 1, nseq, 1, CONV_DIM)
        cs_spec = pl.BlockSpec((CONV_W - 1, 1, 1, CONV_DIM), lambda i: (0, i // tiles_per_seq, 0, 0))
        scratch = [pltpu.VMEM((tm + 8, CONV_DIM), F32)]
    z, xc, dt, cs = pl.pallas_call(
        functools.partial(_ssm_proj_kernel, decode=decode, tiles_per_seq=tiles_per_seq),
        grid=(m // tm,),
        in_specs=in_specs,
        out_specs=[pl.BlockSpec((tm, D_INNER), row), pl.BlockSpec((tm, CONV_DIM), row),
                   pl.BlockSpec((tm, LANES), row), cs_spec],
        out_shape=[jax.ShapeDtypeStruct((m, D_INNER), F32), jax.ShapeDtypeStruct((m, CONV_DIM), F32),
                   jax.ShapeDtypeStruct((m, LANES), F32), jax.ShapeDtypeStruct(cs_shape, F32)],
        scratch_shapes=scratch,
        compiler_params=_cparams("arbitrary"),
        name="ssm_in_proj_decode" if decode else "ssm_in_proj",
    )(*args)
    return z, xc, dt, cs.reshape(CONV_W - 1, -1, CONV_DIM)


def _head_expand():
    e = np.zeros((LANES, D_INNER), np.float32)
    for hh in range(SSM_HEADS):
        e[hh, hh * SSM_HEAD_DIM:(hh + 1) * SSM_HEAD_DIM] = 1.0
    return e


def _spread(x, e, terms=3):
    parts = _split3(x)[:terms]
    out = _dot(parts[0].astype(BF16), e)
    for t in parts[1:]:
        out = out + _dot(t.astype(BF16), e)
    return out


def _spread_rows(e, x):
    hi, mid, lo = _split3(x)
    return _dot(e, hi.astype(BF16)) + _dot(e, mid.astype(BF16)) + _dot(e, lo.astype(BF16))


def _ssd_kernel(xs_ref, b_ref, c_ref, dt_ref, alog_ref, dsk_ref, e_ref, et_ref, y_ref, ht_ref, hst):
    q = SSD_CHUNK
    gw = D_INNER // SSM_GROUPS
    ci = pl.program_id(1)

    @pl.when(ci == 0)
    def _():
        hst[...] = jnp.zeros_like(hst)

    r = lax.broadcasted_iota(jnp.int32, (q, q), 0)
    c = lax.broadcasted_iota(jnp.int32, (q, q), 1)
    causal = r >= c
    lane = lax.broadcasted_iota(jnp.int32, (q, LANES), 1)
    first_half = lane < SSM_HEAD_DIM
    a = -jnp.exp(alog_ref[...])
    expand = e_ref[...]
    dsk = dsk_ref[...]
    for sub in range(xs_ref.shape[0] // q):
        _ssd_chunk(slice(sub * q, (sub + 1) * q), xs_ref, b_ref, c_ref, dt_ref, y_ref, hst, et_ref,
                   a, expand, dsk, causal, first_half)

    @pl.when(ci == pl.num_programs(1) - 1)
    def _():
        ht_ref[0] = hst[...]


def _ssd_chunk(ts, xs_ref, b_ref, c_ref, dt_ref, y_ref, hst, et_ref, a, expand, dsk, causal, first_half):
    q = SSD_CHUNK
    gw = D_INNER // SSM_GROUPS
    dt = dt_ref[ts, :]
    acum = _dot(causal.astype(F32), dt * a, precision=HIGHEST)
    acum_t = acum.T
    dt_t = dt.T
    a_last = acum[q - 1:q, :]
    w_full = _spread(jnp.exp(a_last - acum) * dt, expand, terms=2)
    ea_full = _spread(jnp.exp(acum), expand, terms=2)
    dec_rows = _spread_rows(et_ref[...], jnp.exp(jnp.broadcast_to(acum_t[:, q - 1:q], (LANES, LANES))))
    xs = xs_ref[ts, :]
    xs_bf = xs.astype(BF16)
    xw = xs * w_full
    for g in range(SSM_GROUPS):
        bg = b_ref[ts, g * D_STATE:(g + 1) * D_STATE].astype(BF16)
        cg = c_ref[ts, g * D_STATE:(g + 1) * D_STATE].astype(BF16)
        cb = _dot_nt(cg, bg)
        rows = slice(g * gw, (g + 1) * gw)
        h_in = hst[rows, :]
        y_inter = _dot_nt(cg, h_in.astype(BF16))
        parts = []
        for pr in range(SSM_HPG // 2):
            h0 = g * SSM_HPG + 2 * pr
            x_pair = xs_bf[:, h0 * SSM_HEAD_DIM:(h0 + 2) * SSM_HEAD_DIM]
            outs = []
            for hh in (h0, h0 + 1):
                seg = acum[:, hh:hh + 1] - acum_t[hh:hh + 1, :]
                mh = cb * jnp.exp(jnp.where(causal, seg, -jnp.inf)) * dt_t[hh:hh + 1, :]
                outs.append(_dot(mh.astype(BF16), x_pair))
            parts.append(jnp.where(first_half, outs[0], outs[1]))
        y_intra = jnp.concatenate(parts, axis=-1)
        y_ref[ts, rows] = y_intra + y_inter * ea_full[:, rows] + dsk[:, rows] * xs[:, rows]
        s_inc = _dot(xw[:, rows].T.astype(BF16), bg)
        hst[rows, :] = h_in * dec_rows[rows, :] + s_inc


def _ssd_prompt(xc, dt, alog, dsk, b, t):
    step_rows = SSD_CHUNKS_PER_STEP * SSD_CHUNK
    nc = t // step_rows
    e = jnp.asarray(_head_expand(), BF16)
    et = jnp.asarray(_head_expand().T.copy(), BF16)
    rowmap = lambda i, c: (i * nc + c, 0)
    bc_w = SSM_GROUPS * D_STATE
    return pl.pallas_call(
        _ssd_kernel,
        grid=(b, nc),
        in_specs=[pl.BlockSpec((step_rows, D_INNER), rowmap),
                  pl.BlockSpec((step_rows, bc_w), lambda i, c: (i * nc + c, D_INNER // bc_w)),
                  pl.BlockSpec((step_rows, bc_w), lambda i, c: (i * nc + c, D_INNER // bc_w + 1)),
                  pl.BlockSpec((step_rows, LANES), rowmap),
                  _const_spec((1, LANES)), _const_spec((1, D_INNER)),
                  _const_spec((LANES, D_INNER)), _const_spec((D_INNER, LANES))],
        out_specs=[pl.BlockSpec((step_rows, D_INNER), rowmap),
                   pl.BlockSpec((1, D_INNER, D_STATE), lambda i, c: (i, 0, 0))],
        out_shape=[jax.ShapeDtypeStruct((b * t, D_INNER), F32),
                   jax.ShapeDtypeStruct((b, D_INNER, D_STATE), F32)],
        scratch_shapes=[pltpu.VMEM((D_INNER, D_STATE), F32)],
        compiler_params=_cparams("arbitrary", "arbitrary"),
        name="ssd_scan",
    )(xc, xc, xc, dt, alog, dsk, e, et)


def _ssd_step_kernel(xs_ref, b_ref, c_ref, dt_ref, alog_ref, dsk_ref, e_ref, et_ref, h0_ref, y_ref, ht_ref):
    gw = D_INNER // SSM_GROUPS
    dt = dt_ref[0]
    a = -jnp.exp(alog_ref[...])
    dec = jnp.exp(dt * a)
    expand = e_ref[...]
    eye = (lax.broadcasted_iota(jnp.int32, (LANES, LANES), 0)
           == lax.broadcasted_iota(jnp.int32, (LANES, LANES), 1))
    dec_col = jnp.sum(jnp.where(eye, jnp.broadcast_to(dec, (LANES, LANES)), 0.0), axis=1, keepdims=True)
    dec_rows = _spread_rows(et_ref[...], jnp.broadcast_to(dec_col, (LANES, LANES)))
    xs = xs_ref[0]
    dtx = xs * _spread(dt, expand)
    eye_g = (lax.broadcasted_iota(jnp.int32, (gw, gw), 0) == lax.broadcasted_iota(jnp.int32, (gw, gw), 1))
    bm = b_ref[0]
    cm = c_ref[0]
    ys = []
    for g in range(SSM_GROUPS):
        rows = slice(g * gw, (g + 1) * gw)
        bg = bm[:, g * D_STATE:(g + 1) * D_STATE]
        cg = cm[:, g * D_STATE:(g + 1) * D_STATE]
        diag = jnp.where(eye_g, jnp.broadcast_to(dtx[:, rows], (gw, gw)), 0.0).astype(BF16)
        outer = _dot(diag, jnp.broadcast_to(bg, (gw, D_STATE)).astype(BF16))
        h_new = h0_ref[0, rows, :] * dec_rows[rows, :] + outer
        ht_ref[0, rows, :] = h_new
        yg = _dot_nt(jnp.broadcast_to(cg, (8, D_STATE)).astype(BF16), h_new.astype(BF16))
        ys.append(yg[0:1, :])
    y_ref[0] = jnp.concatenate(ys, axis=-1) + dsk_ref[...] * xs


def _ssd_step(xc, dt, alog, dsk, h0):
    b = xc.shape[0]
    e = jnp.asarray(_head_expand(), BF16)
    et = jnp.asarray(_head_expand().T.copy(), BF16)
    bc_w = SSM_GROUPS * D_STATE
    xc3 = xc.reshape(b, 1, CONV_DIM)
    return pl.pallas_call(
        _ssd_step_kernel,
        grid=(b,),
        in_specs=[pl.BlockSpec((1, 1, D_INNER), lambda i: (i, 0, 0)),
                  pl.BlockSpec((1, 1, bc_w), lambda i: (i, 0, D_INNER // bc_w)),
                  pl.BlockSpec((1, 1, bc_w), lambda i: (i, 0, D_INNER // bc_w + 1)),
                  pl.BlockSpec((1, 1, LANES), lambda i: (i, 0, 0)),
                  _const_spec((1, LANES)), _const_spec((1, D_INNER)),
                  _const_spec((LANES, D_INNER)), _const_spec((D_INNER, LANES)),
                  pl.BlockSpec((1, D_INNER, D_STATE), lambda i: (i, 0, 0))],
        out_specs=[pl.BlockSpec((1, 1, D_INNER), lambda i: (i, 0, 0)),
                   pl.BlockSpec((1, D_INNER, D_STATE), lambda i: (i, 0, 0))],
        out_shape=[jax.ShapeDtypeStruct((b, 1, D_INNER), F32),
                   jax.ShapeDtypeStruct((b, D_INNER, D_STATE), F32)],
        compiler_params=_cparams("arbitrary"),
        name="ssd_step",
    )(xc3, xc3, xc3, dt.reshape(b, 1, LANES), alog, dsk, e, et, h0)


def _fox_decode_kernel(pt_ref, qrow_ref, kn_ref, vn_ref, lfn_ref, *refs):
    n = PAGES_PER_STEP
    k_refs, v_refs, lf_refs = refs[:n], refs[n:2 * n], refs[2 * n:3 * n]
    o_ref, m_s, l_s, r_s, acc_s, qbd_s = refs[3 * n:]
    j = pl.program_id(1)
    rows8 = lax.broadcasted_iota(jnp.int32, (N_HEADS, HEAD_DIM), 0)

    @pl.when(j == 0)
    def _():
        on_diag = (lax.broadcasted_iota(jnp.int32, (N_HEADS, ATT_W), 1) >> (HEAD_DIM.bit_length() - 1)
                   == lax.broadcasted_iota(jnp.int32, (N_HEADS, ATT_W), 0))
        qrow = qrow_ref[0] * (HEAD_DIM ** -0.5)
        qbd_s[...] = jnp.where(on_diag, jnp.broadcast_to(qrow, (N_HEADS, ATT_W)), 0.0).astype(BF16)
        m_s[...] = jnp.sum(jnp.where(on_diag, jnp.broadcast_to(qrow * kn_ref[0], (N_HEADS, ATT_W)), 0.0),
                           axis=1, keepdims=True)
        l_s[...] = jnp.ones_like(l_s)
        r_s[...] = lfn_ref[0]
        acc_s[...] = vn_ref[0]

    r = lax.broadcasted_iota(jnp.int32, (LANES, LANES), 0)
    c = lax.broadcasted_iota(jnp.int32, (LANES, LANES), 1)
    triu = (r <= c).astype(F32)
    run = r_s[...]
    bias = []
    for i in range(n):
        cs = _dot(lf_refs[i][...], triu, precision=HIGHEST)
        total = cs[:, PAGE_SIZE - 1:PAGE_SIZE]
        bias.append(run + total - cs)
        run = run + total
    r_s[...] = run
    k_all = jnp.concatenate([k_refs[i][...].reshape(ATT_W, PAGE_SIZE).astype(BF16) for i in range(n)], axis=1)
    s = _dot(qbd_s[...], k_all) + jnp.concatenate(bias, axis=1)
    m_old = m_s[...]
    m_new = jnp.maximum(m_old, jnp.max(s, axis=1, keepdims=True))
    alpha = jnp.exp(m_old - m_new)
    p = jnp.exp(s - m_new)
    l_s[...] = alpha * l_s[...] + jnp.sum(p, axis=1, keepdims=True)
    m_s[...] = m_new
    pb = p.astype(BF16)
    acc = acc_s[...] * alpha
    for h in range(N_HEADS):
        v_h = jnp.concatenate([v_refs[i][h].astype(BF16) for i in range(n)], axis=1)
        acc = acc + jnp.where(rows8 == h, _dot_nt(pb, v_h), 0.0)
    acc_s[...] = acc

    @pl.when(j == pl.num_programs(1) - 1)
    def _():
        o_ref[0] = acc_s[...] / l_s[...]


def _fox_decode(qrow, kn_row, v_new, lf_new, pool_k, pool_v, pool_lf, page_table):
    b, n_pages = page_table.shape
    n = PAGES_PER_STEP
    steps = n_pages // n
    row3 = lambda i, j, pt: (i, 0, 0)

    def page_spec(shape, off):
        nd = len(shape)
        return pl.BlockSpec((None, None) + shape,
                            lambda i, j, pt: (0, pt[i, n_pages - 1 - (j * n + off)]) + (0,) * nd)

    in_specs = ([pl.BlockSpec((1, 1, ATT_W), row3)] * 2 + [pl.BlockSpec((1, N_HEADS, HEAD_DIM), row3),
                                                          pl.BlockSpec((1, N_HEADS, 1), row3)]
                + [page_spec((N_HEADS, HEAD_DIM, PAGE_SIZE), i) for i in range(n)] * 2
                + [page_spec((N_HEADS, PAGE_SIZE), i) for i in range(n)])
    grid_spec = pltpu.PrefetchScalarGridSpec(
        num_scalar_prefetch=1, grid=(b, steps), in_specs=in_specs,
        out_specs=pl.BlockSpec((1, N_HEADS, HEAD_DIM), row3),
        scratch_shapes=[pltpu.VMEM((N_HEADS, 1), F32)] * 3 + [pltpu.VMEM((N_HEADS, HEAD_DIM), F32),
                                                              pltpu.VMEM((N_HEADS, ATT_W), BF16)])
    return pl.pallas_call(
        _fox_decode_kernel, grid_spec=grid_spec,
        out_shape=jax.ShapeDtypeStruct((b, N_HEADS, HEAD_DIM), F32),
        compiler_params=_cparams("arbitrary", "arbitrary"),
        name="fox_decode_attn",
    )(page_table, qrow, kn_row, v_new, lf_new, *([pool_k] * n), *([pool_v] * n), *([pool_lf] * n))


def _moba_gate_kernel(pt_ref, qcol_ref, *refs):
    n = PAGES_PER_STEP
    k_refs = refs[:n]
    idx_ref, gate_s, qb_s = refs[n:]
    j = pl.program_id(1)
    ppb = MOBA_BLOCK // PAGE_SIZE
    lane = lax.broadcasted_iota(jnp.int32, (N_HEADS, LANES), 1)

    @pl.when(j == 0)
    def _():
        gate_s[...] = jnp.full_like(gate_s, -jnp.inf)
        qb_s[...] = jnp.broadcast_to(qcol_ref[0], (ATT_W, PAGE_SIZE))

    qb = qb_s[...]
    sub = 8
    fold = (lax.broadcasted_iota(jnp.int32, (N_HEADS, N_HEADS * sub), 1) >> 3
            == lax.broadcasted_iota(jnp.int32, (N_HEADS, N_HEADS * sub), 0)).astype(F32)
    gates = gate_s[...]
    for blk in range(n // ppb):
        ksum = None
        for i in range(ppb):
            page = k_refs[blk * ppb + i][...].reshape(ATT_W, PAGE_SIZE)
            ksum = page if ksum is None else ksum + page
        part = jnp.sum((ksum * qb).reshape(N_HEADS, HEAD_DIM // sub, sub, PAGE_SIZE), axis=1)
        per_head = _dot(fold, part.reshape(N_HEADS * sub, PAGE_SIZE), precision=HIGHEST)
        g = jnp.sum(per_head, axis=1, keepdims=True) * (1.0 / MOBA_BLOCK)
        gates = jnp.where(lane == j * (n // ppb) + blk, g, gates)
    gate_s[...] = gates

    @pl.when(j == pl.num_programs(1) - 1)
    def _():
        gate = gate_s[...]
        lane_f = lane.astype(F32)
        picks = jnp.zeros((N_HEADS, LANES), F32)
        for k in range(MOBA_TOPK):
            best = jnp.max(gate, axis=1, keepdims=True)
            first = jnp.min(jnp.where(gate == best, lane_f, float(LANES)), axis=1, keepdims=True)
            picks = jnp.where(lane == k, first, picks)
            gate = jnp.where(lane_f == first, -jnp.inf, gate)
        idx_ref[0] = picks.astype(jnp.int32)


def _moba_gate(qcol, pool_k, page_table):
    b, n_pages = page_table.shape
    n = PAGES_PER_STEP
    steps = n_pages // n
    row3 = lambda i, j, pt: (i, 0, 0)
    in_specs = [pl.BlockSpec((1, ATT_W, 1), row3)] + [
        pl.BlockSpec((None, None, N_HEADS, HEAD_DIM, PAGE_SIZE),
                     functools.partial(lambda i, j, pt, off: (0, pt[i, j * n + off], 0, 0, 0), off=off))
        for off in range(n)]
    grid_spec = pltpu.PrefetchScalarGridSpec(
        num_scalar_prefetch=1, grid=(b, steps), in_specs=in_specs,
        out_specs=pl.BlockSpec((1, N_HEADS, LANES), row3),
        scratch_shapes=[pltpu.VMEM((N_HEADS, LANES), F32), pltpu.VMEM((ATT_W, PAGE_SIZE), F32)])
    return pl.pallas_call(
        _moba_gate_kernel, grid_spec=grid_spec,
        out_shape=jax.ShapeDtypeStruct((b, N_HEADS, LANES), jnp.int32),
        compiler_params=_cparams("arbitrary", "arbitrary"),
        name="moba_decode_gate",
    )(page_table, qcol, *([pool_k] * n))


def _moba_decode_kernel(pt_ref, idx_ref, q_ref, kn_ref, vn_ref, *refs):
    npg = MOBA_TOPK * (MOBA_BLOCK // PAGE_SIZE)
    nh = MOBA_HEADS_PER_STEP
    k_refs, v_refs = refs[:nh * npg], refs[nh * npg:2 * nh * npg]
    o_ref, acc_s = refs[2 * nh * npg:]
    grp = pl.program_id(1)
    q8 = q_ref[0] * (HEAD_DIM ** -0.5)
    q8_bf = q8.astype(BF16)
    rows8 = lax.broadcasted_iota(jnp.int32, (N_HEADS, HEAD_DIM), 0)
    s_new = jnp.sum(q8 * kn_ref[0], axis=1, keepdims=True)

    @pl.when(grp == 0)
    def _():
        acc_s[...] = jnp.zeros_like(acc_s)

    out = acc_s[...]
    for hl in range(nh):
        k_h = jnp.concatenate([k_refs[hl * npg + i][...].astype(BF16) for i in range(npg)], axis=1)
        v_h = jnp.concatenate([v_refs[hl * npg + i][...].astype(BF16) for i in range(npg)], axis=1)
        s = _dot(q8_bf, k_h)
        m = jnp.maximum(jnp.max(s, axis=1, keepdims=True), s_new)
        p = jnp.exp(s - m)
        p_new = jnp.exp(s_new - m)
        denom = jnp.sum(p, axis=1, keepdims=True) + p_new
        out_h = (_dot_nt(p.astype(BF16), v_h) + p_new * vn_ref[0]) / denom
        out = jnp.where(rows8 == grp * nh + hl, out_h, out)
    acc_s[...] = out

    @pl.when(grp == pl.num_programs(1) - 1)
    def _():
        o_ref[0] = out


def _moba_decode(q, k_new, v_new, idx, pool_k, pool_v, page_table):
    b, _ = page_table.shape
    ppb = MOBA_BLOCK // PAGE_SIZE
    npg = MOBA_TOPK * ppb
    nh = MOBA_HEADS_PER_STEP
    row3 = lambda i, g, pt, ix: (i, 0, 0)

    def page(hl, k, off):
        def index(i, g, pt, ix):
            h = g * nh + hl
            return (0, pt[i, ix[i, h * MOBA_TOPK + k] * ppb + off], h, 0, 0)
        return pl.BlockSpec((None, None, None, HEAD_DIM, PAGE_SIZE), index)

    slabs = [page(hl, k, off) for hl in range(nh) for k in range(MOBA_TOPK) for off in range(ppb)]
    in_specs = [pl.BlockSpec((1, N_HEADS, HEAD_DIM), row3)] * 3 + slabs * 2
    grid_spec = pltpu.PrefetchScalarGridSpec(
        num_scalar_prefetch=2, grid=(b, N_HEADS // nh), in_specs=in_specs,
        out_specs=pl.BlockSpec((1, N_HEADS, HEAD_DIM), row3),
        scratch_shapes=[pltpu.VMEM((N_HEADS, HEAD_DIM), F32)])
    return pl.pallas_call(
        _moba_decode_kernel, grid_spec=grid_spec,
        out_shape=jax.ShapeDtypeStruct((b, N_HEADS, HEAD_DIM), F32),
        compiler_params=_cparams("arbitrary", "arbitrary"),
        name="moba_decode_attn",
    )(page_table, idx, q, k_new, v_new, *([pool_k] * len(slabs)), *([pool_v] * len(slabs)))


def _pad_lanes(a, width=LANES):
    return jnp.pad(a, [(0, 0)] * (a.ndim - 1) + [(0, width - a.shape[-1])])


def _run_group(x, ada, weights, decode, caches=None):
    b, t, _ = x.shape
    m = b * t
    x2d = x.reshape(m, D_MODEL)
    if decode:
        tm = m
        mod_map = lambda i: (0, 0, 0)
        as_mod = lambda a: a.reshape(1, b, D_MODEL)
    else:
        tm = 512
        tps = t // tm
        mod_map = lambda i: (i // tps, 0, 0)
        as_mod = lambda a: a.reshape(b, 1, D_MODEL)
    row = lambda v: v.reshape(1, -1)

    sh1, sc1, gt1, sh2, sc2, gt2 = [as_mod(a) for a in jnp.split(ada[0], 6, axis=-1)]
    pos = (caches["page_table"].shape[1] * PAGE_SIZE if decode else 0) + np.arange(t)
    qa, qb, ka, va, kb, vb, lf = _attn_proj(
        x2d, sc1, sh1, row(weights["g_mix_pre"][0]), weights["wt_att6"], weights["wt_att_f"],
        weights["b_fox_f"], pos, tm, mod_map, t, prompt=not decode)
    if decode:
        hd = lambda a: a.reshape(b, N_HEADS, HEAD_DIM)
        pt = caches["page_table"]
        idx = _moba_gate(qa.reshape(b, ATT_W, 1), caches["moba_k"], pt)[:, :, :MOBA_TOPK]
        oa = _moba_decode(hd(qa), hd(ka), hd(va), idx.reshape(b, N_HEADS * MOBA_TOPK),
                          caches["moba_k"], caches["moba_v"], pt)
        ob = _fox_decode(qb.reshape(b, 1, ATT_W), kb.reshape(b, 1, ATT_W), hd(vb), lf.reshape(b, N_HEADS, 1),
                         caches["fox_k"], caches["fox_v"], caches["fox_lf"], pt)
        oa, ob = oa.reshape(m, ATT_W), ob.reshape(m, ATT_W)
        rows5 = lambda a: a.reshape(1, b, t, N_HEADS, HEAD_DIM)
        lf_out = lf.reshape(1, b, t, N_HEADS)
    else:
        ft, fcol = _fox_cumsum(lf, b, t)
        oa = _attn_prompt(qa, ka, va, b, t, fox=False)
        ob = _attn_prompt(qb, kb, vb, b, t, fox=True, ft=ft, fcol=fcol)
        rows5 = lambda a: a.reshape(1, b, N_HEADS, HEAD_DIM, t).transpose(0, 1, 4, 2, 3)
        lf_out = lf.reshape(1, b, N_HEADS, t).transpose(0, 1, 3, 2)
    x2d = _post(x2d, [oa, ob], weights["w_att_out"], (gt1, sc2, sh2, gt2),
                (row(weights["g_mix_post"][0]), row(weights["g_ffn_pre"][0]), row(weights["g_ffn_post"][0])),
                weights["w_ffn_in"][0], weights["w_ffn_out"][0], tm, mod_map, ssm=False)

    sh1, sc1, gt1, sh2, sc2, gt2 = [as_mod(a) for a in jnp.split(ada[1], 6, axis=-1)]
    prev = caches["state_conv"] if decode else None
    z, xc, dt, conv_state = _ssm_proj(
        x2d, sc1, sh1, row(weights["g_mix_pre"][1]), weights["wt_ssm_z"], weights["wt_ssm_x"],
        weights["wt_ssm_dt"], weights["dt_bias"], weights["conv_w"], weights["conv_b"], tm, mod_map, t, prev=prev)
    if decode:
        y, h_t = _ssd_step(xc, dt, weights["a_log"], weights["d_skip"], caches["state_ssm"])
        y = y.reshape(m, D_INNER)
    else:
        y, h_t = _ssd_prompt(xc, dt, weights["a_log"], weights["d_skip"], b, t)
    x2d = _post(x2d, [y, z], weights["w_ssm_out"], (gt1, sc2, sh2, gt2),
                (row(weights["g_mix_post"][1]), row(weights["g_ffn_pre"][1]), row(weights["g_ffn_post"][1])),
                weights["w_ffn_in"][1], weights["w_ffn_out"][1], tm, mod_map, ssm=True,
                gn=row(weights["g_ssm_norm"]))

    return (x2d.reshape(b, t, D_MODEL), rows5(ka), rows5(va), rows5(kb), rows5(vb), lf_out,
            h_t.reshape(1, b, SSM_HEADS, SSM_HEAD_DIM, D_STATE),
            conv_state.transpose(1, 0, 2).reshape(1, b, CONV_W - 1, CONV_DIM))


def kernel(x_prompt, x_sample, cache_moba_k, cache_moba_v, cache_fox_k, cache_fox_v, cache_fox_logf, state_ssm, state_conv, page_table, c_prompt, c_sample, w_ada, b_ada, g_mix_pre, g_mix_post, g_ffn_pre, g_ffn_post, w_att_in, b_fox_f, w_att_out, w_ssm_in, conv_w, conv_b, dt_bias, a_log, d_skip, g_ssm_norm, w_ssm_out, w_ffn_in, w_ffn_out):
    bp = x_prompt.shape[0]
    bs = x_sample.shape[0]
    ada = _ada(jnp.concatenate([c_prompt, c_sample], axis=0), w_ada, b_ada)
    wt_att = w_att_in[0].T
    wt_ssm = w_ssm_in[0].T
    n_dt = wt_ssm.shape[0] - D_INNER - CONV_DIM
    weights = dict(
        g_mix_pre=g_mix_pre, g_mix_post=g_mix_post, g_ffn_pre=g_ffn_pre, g_ffn_post=g_ffn_post,
        wt_att6=wt_att[:6 * ATT_W].astype(BF16),
        wt_att_f=wt_att[6 * ATT_W:],
        b_fox_f=b_fox_f[0],
        w_att_out=w_att_out[0].astype(BF16),
        wt_ssm_z=wt_ssm[:D_INNER].astype(BF16),
        wt_ssm_x=wt_ssm[D_INNER:D_INNER + CONV_DIM].astype(BF16),
        wt_ssm_dt=jnp.pad(wt_ssm[D_INNER + CONV_DIM:], ((0, LANES - n_dt), (0, 0))).astype(BF16),
        dt_bias=_pad_lanes(dt_bias[0].reshape(1, -1)),
        conv_w=conv_w[0], conv_b=conv_b[0].reshape(1, -1),
        a_log=_pad_lanes(a_log[0].reshape(1, -1)),
        d_skip=jnp.repeat(d_skip[0], SSM_HEAD_DIM).reshape(1, -1),
        g_ssm_norm=g_ssm_norm[0],
        w_ssm_out=w_ssm_out[0].astype(BF16),
        w_ffn_in=w_ffn_in.astype(BF16), w_ffn_out=w_ffn_out.astype(BF16),
    )
    kv_t = lambda a: a.transpose(0, 1, 3, 4, 2)
    caches = dict(page_table=page_table, moba_k=kv_t(cache_moba_k), moba_v=kv_t(cache_moba_v),
                  fox_k=kv_t(cache_fox_k), fox_v=kv_t(cache_fox_v),
                  fox_lf=cache_fox_logf.transpose(0, 1, 3, 2),
                  state_ssm=state_ssm[0].reshape(bs, D_INNER, D_STATE),
                  state_conv=state_conv[0].transpose(1, 0, 2))
    prompt = _run_group(x_prompt, ada[:, :bp], weights, decode=False)
    sample = _run_group(x_sample, ada[:, bp:], weights, decode=True, caches=caches)
    return (prompt[0], sample[0]) + prompt[1:] + sample[1:]
```

```python
import functools
import math

import numpy as np
import jax
import jax.numpy as jnp
from jax import lax
from jax.experimental import pallas as pl
from jax.experimental.pallas import tpu as pltpu

F32 = jnp.float32
BF16 = jnp.bfloat16
HIGHEST = lax.Precision.HIGHEST

D_MODEL = 1024
HEAD_DIM = 64
N_HEADS = 8
ATT_W = N_HEADS * HEAD_DIM
ROT_DIM = HEAD_DIM // 4
ROPE_THETA = 500000.0
MOBA_BLOCK = 256
MOBA_TOPK = 3
PAGE_SIZE = 128
D_INNER = 2 * D_MODEL
SSM_HEAD_DIM = 64
SSM_HEADS = D_INNER // SSM_HEAD_DIM
SSM_GROUPS = 4
SSM_HPG = SSM_HEADS // SSM_GROUPS
D_STATE = 128
CONV_W = 4
CONV_DIM = D_INNER + 2 * SSM_GROUPS * D_STATE
SSD_CHUNK = 128
SSD_CHUNKS_PER_STEP = 4
D_FF = ((8 * D_MODEL + 3 * 256 - 1) // (3 * 256)) * 256
MXU_COLS = 256
FFN_SPLIT = (D_FF // MXU_COLS + 1) // 2 * MXU_COLS
EPS = 1e-6
LANES = 128
VMEM_LIMIT = 56 * 1024 * 1024
PAGES_PER_STEP = 16
GATE_PAGES_PER_STEP = 32
MOBA_HEADS_PER_STEP = 2
LOG2E = math.log2(math.e)
NEG = -1e30

_NT = (((1,), (1,)), ((), ()))


def _cparams(*sem):
    return pltpu.CompilerParams(dimension_semantics=sem, vmem_limit_bytes=VMEM_LIMIT)


def _const_spec(shape):
    nd = len(shape)
    return pl.BlockSpec(shape, lambda *_: (0,) * nd, pipeline_mode=pl.Buffered(1))


def _rms(x, g):
    return x * lax.rsqrt(jnp.mean(x * x, axis=-1, keepdims=True) + EPS) * g


def _silu(x):
    return x * jax.nn.sigmoid(x)


def _softplus(x):
    return jnp.maximum(x, 0.0) + jnp.log1p(jnp.exp(-jnp.abs(x)))


def _log_sigmoid(x):
    return jnp.minimum(x, 0.0) - jnp.log1p(jnp.exp(-jnp.abs(x)))


def _dot(a, b, **kw):
    return jnp.dot(a, b, preferred_element_type=F32, **kw)


def _dot_nt(a, b, **kw):
    return lax.dot_general(a, b, _NT, preferred_element_type=F32, **kw)


def _split3(x):
    hi = x.astype(BF16).astype(F32)
    r = x - hi
    mid = r.astype(BF16).astype(F32)
    return hi, mid, r - mid


def _ada_kernel(c_ref, w_ref, b_ref, o_ref):
    a = _silu(c_ref[...]).astype(BF16)
    o_ref[0] = _dot(a, w_ref[0].astype(BF16)) + b_ref[0]


def _ada(c_all, w_ada, b_ada):
    depth, _, n = w_ada.shape
    rows = c_all.shape[0]
    tn = 1024
    return pl.pallas_call(
        _ada_kernel,
        grid=(depth, n // tn),
        in_specs=[pl.BlockSpec((rows, D_MODEL), lambda l, j: (0, 0)),
                  pl.BlockSpec((1, D_MODEL, tn), lambda l, j: (l, 0, j)),
                  pl.BlockSpec((1, 1, tn), lambda l, j: (l, 0, j))],
        out_specs=pl.BlockSpec((1, rows, tn), lambda l, j: (l, 0, j)),
        out_shape=jax.ShapeDtypeStruct((depth, rows, n), F32),
        compiler_params=_cparams("arbitrary", "arbitrary"),
        name="ada_terms",
    )(c_all, w_ada, b_ada.reshape(depth, 1, n))


def _rope_tables(pos):
    half = ROT_DIM // 2
    inv = ROPE_THETA ** (-2.0 * np.arange(half, dtype=np.float64) / ROT_DIM)
    ang = np.asarray(pos, np.float64)[:, None] * inv[None, :]
    cos, sin = np.cos(ang), np.sin(ang)
    n = len(pos)
    c64 = np.concatenate([cos, cos, np.ones((n, HEAD_DIM - ROT_DIM))], axis=1)
    s1 = np.concatenate([-sin, np.zeros((n, HEAD_DIM - half))], axis=1)
    s2 = np.concatenate([np.zeros((n, half)), sin, np.zeros((n, HEAD_DIM - ROT_DIM))], axis=1)
    return [t.astype(np.float32) for t in (c64, s1, s2)]


def _rope(z, cos, s1, s2, axis):
    half = ROT_DIM // 2
    period = cos.shape[axis]
    outs = []
    for j in range(z.shape[axis] // period):
        zj = lax.slice_in_dim(z, j * period, (j + 1) * period, axis=axis)
        outs.append(zj * cos + pltpu.roll(zj, period - half, axis) * s1 + pltpu.roll(zj, half, axis) * s2)
    return jnp.concatenate(outs, axis=axis)


def _attn_proj_kernel(*refs, prompt):
    x_ref, sc_ref, sh_ref, g_ref, wt_ref, wtf_ref, bf_ref, cos_ref, s1_ref, s2_ref = refs[:10]
    if prompt:
        cost_ref, s1t_ref, s2t_ref = refs[10:13]
        qa_ref, qb_ref, ka_ref, va_ref, kb_ref, vb_ref, lf_ref = refs[13:]
    else:
        qa_ref, qb_ref, ka_ref, va_ref, kb_ref, vb_ref, lf_ref = refs[10:]
    h = (_rms(x_ref[...], g_ref[...]) * (1.0 + sc_ref[0]) + sh_ref[0]).astype(BF16)
    w = lambda n: wt_ref[n * ATT_W:(n + 1) * ATT_W, :]
    qa_ref[...] = _rope(_dot_nt(h, w(0)), cos_ref[0], s1_ref[0], s2_ref[0], 1)
    qb_ref[...] = _dot_nt(h, w(3))
    wtf = wtf_ref[...].astype(BF16)
    if prompt:
        ka_ref[0] = _rope(_dot_nt(w(1), h), cost_ref[...], s1t_ref[...], s2t_ref[...], 0)
        va_ref[0] = _dot_nt(w(2), h)
        kb_ref[0] = _dot_nt(w(4), h)
        vb_ref[0] = _dot_nt(w(5), h)
        lf_ref[0] = _log_sigmoid(_dot_nt(wtf, h) + bf_ref[...])
    else:
        ka_ref[...] = _rope(_dot_nt(h, w(1)), cos_ref[0], s1_ref[0], s2_ref[0], 1)
        va_ref[...] = _dot_nt(h, w(2))
        kb_ref[...] = _dot_nt(h, w(4))
        vb_ref[...] = _dot_nt(h, w(5))
        lf_ref[...] = _log_sigmoid(_dot_nt(h, wtf) + bf_ref[...])


def _attn_proj(x2d, sc, sh, g, wt6, wtf, bf, pos, tm, mod_map, seq_len, prompt):
    m = x2d.shape[0]
    row = lambda i: (i, 0)
    mod_spec = pl.BlockSpec((1,) + sc.shape[1:], mod_map)
    tabs = _rope_tables(pos)
    wide = jax.ShapeDtypeStruct((m, ATT_W), F32)
    in_specs = [pl.BlockSpec((tm, D_MODEL), row), mod_spec, mod_spec, _const_spec((1, D_MODEL)),
                _const_spec(wt6.shape), _const_spec(wtf.shape)]
    if prompt:
        tps = seq_len // tm
        nseq = m // seq_len
        tab_spec = pl.BlockSpec((1, tm, LANES), lambda i: (i % tps, 0, 0))
        tabt_spec = pl.BlockSpec((HEAD_DIM, tm), lambda i: (0, i % tps))
        fm_spec = pl.BlockSpec((1, ATT_W, tm), lambda i: (i // tps, 0, i % tps))
        fm = jax.ShapeDtypeStruct((nseq, ATT_W, seq_len), F32)
        in_specs += [_const_spec((N_HEADS, 1))] + [tab_spec] * 3 + [tabt_spec] * 3
        args = ([jnp.asarray(np.tile(tb, (1, LANES // HEAD_DIM)).reshape(tps, tm, LANES)) for tb in tabs]
                + [jnp.asarray(np.ascontiguousarray(tb.T)) for tb in tabs])
        out_specs = [pl.BlockSpec((tm, ATT_W), row)] * 2 + [fm_spec] * 4 + [
            pl.BlockSpec((1, N_HEADS, tm), lambda i: (i // tps, 0, i % tps))]
        out_shape = [wide] * 2 + [fm] * 4 + [jax.ShapeDtypeStruct((nseq, N_HEADS, seq_len), F32)]
        bf = bf.reshape(N_HEADS, 1)
    else:
        in_specs += [_const_spec((1, N_HEADS))] + [_const_spec((1, 1, LANES))] * 3
        args = [jnp.asarray(np.tile(tb, (1, LANES // HEAD_DIM)).reshape(1, 1, LANES)) for tb in tabs]
        out_specs = [pl.BlockSpec((tm, ATT_W), row)] * 6 + [pl.BlockSpec((tm, N_HEADS), row)]
        out_shape = [wide] * 6 + [jax.ShapeDtypeStruct((m, N_HEADS), F32)]
        bf = bf.reshape(1, N_HEADS)
    return pl.pallas_call(
        functools.partial(_attn_proj_kernel, prompt=prompt),
        grid=(m // tm,),
        in_specs=in_specs,
        out_specs=out_specs,
        out_shape=out_shape,
        compiler_params=_cparams("arbitrary"),
        name="attn_in_proj" if prompt else "attn_in_proj_decode",
    )(x2d, sc, sh, g, wt6, wtf, bf, *args)


def _cumsum_kernel(lf_ref, ft_ref, fcol_ref):
    t = lf_ref.shape[2]
    r = lax.broadcasted_iota(jnp.int32, (LANES, LANES), 0)
    c = lax.broadcasted_iota(jnp.int32, (LANES, LANES), 1)
    triu = (r <= c).astype(F32)
    carry = jnp.zeros((N_HEADS, 1), F32)
    pad = jnp.zeros((LANES - N_HEADS, LANES), F32)
    for i in range(t // LANES):
        cols = slice(i * LANES, (i + 1) * LANES)
        blk = _dot(lf_ref[0, :, cols], triu, precision=HIGHEST) + carry
        carry = blk[:, LANES - 1:LANES]
        ft_ref[0, :, cols] = blk
        fcol_ref[cols, :] = jnp.concatenate([blk, pad], axis=0).T


def _fox_cumsum(lf_t, b, t):
    return pl.pallas_call(
        _cumsum_kernel,
        grid=(b,),
        in_specs=[pl.BlockSpec((1, N_HEADS, t), lambda i: (i, 0, 0))],
        out_specs=[pl.BlockSpec((1, N_HEADS, t), lambda i: (i, 0, 0)),
                   pl.BlockSpec((t, LANES), lambda i: (i, 0))],
        out_shape=[jax.ShapeDtypeStruct((b, N_HEADS, t), F32), jax.ShapeDtypeStruct((b * t, LANES), F32)],
        compiler_params=_cparams("arbitrary"),
        name="fox_cumsum",
    )(lf_t)


def _attn_prompt_kernel(*refs, fox, t):
    if fox:
        q_ref, k_ref, v_ref, fcol_ref, ft_ref, o_ref, kaug, vaug = refs
    else:
        q_ref, k_ref, v_ref, o_ref, kaug, vaug = refs
    blk = MOBA_BLOCK
    nb = t // blk
    pair = pl.program_id(1)
    c_exp = (HEAD_DIM ** -0.5) * LOG2E
    row_t = lax.broadcasted_iota(jnp.int32, (LANES, t), 0)
    low_t = row_t < HEAD_DIM
    r64 = row_t & (HEAD_DIM - 1)
    kf = k_ref[0]
    vf = v_ref[0]
    if fox:
        augs = []
        for hd in range(2):
            hi, mid, lo = _split3(ft_ref[0, pl.ds(2 * pair + hd, 1), :] * LOG2E)
            augs.append(jnp.where(r64 < 3, 1.0, jnp.where(r64 == 3, -hi, jnp.where(
                r64 == 4, -mid, jnp.where(r64 == 5, -lo, 0.0)))))
        kaug[0] = jnp.where(low_t, kf, augs[0]).astype(BF16)
        kaug[1] = jnp.where(low_t, augs[1], kf).astype(BF16)
    else:
        lane_t = lax.broadcasted_iota(jnp.int32, (LANES, t), 1)
        key_blk = lane_t >> (blk.bit_length() - 1)
        ind = jnp.where((key_blk == r64) & (r64 < nb), 1.0, 0.0)
        kaug[0] = jnp.where(low_t, kf, ind).astype(BF16)
        kaug[1] = jnp.where(low_t, ind, kf).astype(BF16)
        hs = 8 * pl.cdiv(nb, 8)
        l128 = lax.broadcasted_iota(jnp.int32, (LANES, LANES), 1)
        r128 = lax.broadcasted_iota(jnp.int32, (LANES, LANES), 0)
        kcols = jnp.zeros((LANES, LANES), F32)
        for n in range(nb):
            col = jnp.mean(kf[:, n * blk:(n + 1) * blk], axis=1, keepdims=True)
            kcols = jnp.where((l128 == n) | (l128 == hs + n), col, kcols)
        kmean = jnp.where(((r128 < hs) & (l128 < HEAD_DIM)) | ((r128 >= hs) & (l128 >= HEAD_DIM)), kcols.T, 0.0)
        blk_id = lax.broadcasted_iota(jnp.int32, (hs, blk), 0)
        fill = jnp.zeros((HEAD_DIM - hs, blk), F32)
    vaug[0] = jnp.where(low_t, vf, jnp.where(row_t == HEAD_DIM, 1.0, 0.0)).astype(BF16)
    vaug[1] = jnp.where(low_t, jnp.where(row_t == 0, 1.0, 0.0), vf).astype(BF16)

    lane = lax.broadcasted_iota(jnp.int32, (blk, LANES), 1)
    low = lane < HEAD_DIM
    l64 = lane & (HEAD_DIM - 1)
    causal = (lax.broadcasted_iota(jnp.int32, (blk, blk), 1) <= lax.broadcasted_iota(jnp.int32, (blk, blk), 0))

    def q_block(qi):
        q0 = qi * blk
        q = q_ref[q0:q0 + blk, :]
        if fox:
            fq = fcol_ref[q0:q0 + blk, :] * LOG2E
            parts = []
            for hd in range(2):
                col = jnp.sum(jnp.where(lane == 2 * pair + hd, fq, 0.0), axis=1, keepdims=True)
                hi, mid, lo = _split3(col)
                parts.append(jnp.where(l64 == 0, hi, jnp.where(l64 == 1, mid, jnp.where(
                    l64 == 2, lo, jnp.where(l64 < 6, 1.0, 0.0)))))
            aug = jnp.where(low, parts[1], parts[0])
        else:
            gate_t = _dot_nt(kmean, q, precision=HIGHEST)
            aug_t = []
            for hd in range(2):
                gate = gate_t[hd * hs:(hd + 1) * hs, :]
                beaten = jnp.zeros((hs, blk), F32)
                for n2 in range(qi):
                    g2 = gate[n2:n2 + 1, :]
                    better = (g2 > gate) | ((g2 == gate) & (n2 < blk_id))
                    beaten = beaten + jnp.where(better, 1.0, 0.0)
                chosen = ((beaten < MOBA_TOPK) & (blk_id < qi)) | (blk_id == qi)
                aug_t.append(jnp.where((blk_id < nb) & jnp.logical_not(chosen), NEG, 0.0))
            aug = jnp.concatenate([aug_t[1], fill, aug_t[0], fill], axis=0).T
        qc = q * c_exp
        qaug = (jnp.where(low, qc, aug).astype(BF16), jnp.where(low, aug, qc).astype(BF16))

        own = slice(q0, q0 + blk)
        outs = []
        for hd in range(2):
            s_own = jnp.where(causal, _dot(qaug[hd], kaug[hd, :, own]), NEG)
            m = jnp.max(s_own, axis=1, keepdims=True)
            if qi > 0:
                s_prev = _dot(qaug[hd], kaug[hd, :, 0:q0])
                m = jnp.maximum(m, jnp.max(s_prev, axis=1, keepdims=True))
                acc = _dot_nt(jnp.exp2(s_prev - m).astype(BF16), vaug[hd, :, 0:q0])
            else:
                acc = jnp.zeros((blk, LANES), F32)
            acc = acc + _dot_nt(jnp.exp2(s_own - m).astype(BF16), vaug[hd, :, own])
            denom = jnp.sum(jnp.where(lane == (HEAD_DIM if hd == 0 else 0), acc, 0.0), axis=1, keepdims=True)
            outs.append(acc / denom)
        o_ref[own, :] = jnp.where(low, outs[0], outs[1])

    for qi in range(nb):
        q_block(qi)


def _attn_prompt(q, k_t, v_t, b, t, fox, ft=None, fcol=None):
    pairs = ATT_W // LANES
    slab = pl.BlockSpec((t, LANES), lambda i, p: (i, p))
    slab_t = pl.BlockSpec((1, LANES, t), lambda i, p: (i, p, 0))
    in_specs = [slab, slab_t, slab_t]
    args = [q, k_t, v_t]
    if fox:
        in_specs += [pl.BlockSpec((t, LANES), lambda i, p: (i, 0)),
                     pl.BlockSpec((1, N_HEADS, t), lambda i, p: (i, 0, 0))]
        args += [fcol, ft]
    return pl.pallas_call(
        functools.partial(_attn_prompt_kernel, fox=fox, t=t),
        grid=(b, pairs),
        in_specs=in_specs,
        out_specs=slab,
        out_shape=jax.ShapeDtypeStruct((b * t, ATT_W), F32),
        scratch_shapes=[pltpu.VMEM((2, LANES, t), BF16), pltpu.VMEM((2, LANES, t), BF16)],
        compiler_params=_cparams("arbitrary", "arbitrary"),
        name="fox_prompt_attn" if fox else "moba_prompt_attn",
    )(*args)


def _post_kernel(*refs, ssm):
    if ssm:
        x_ref, y_ref, z_ref, gn_ref, wmix_ref = refs[:5]
        rest = refs[5:]
    else:
        x_ref, oa_ref, ob_ref, wmix_ref = refs[:4]
        rest = refs[4:]
    gt1_ref, gpost_ref, gpre_ref, sc2_ref, sh2_ref, gt2_ref, gfpost_ref, win_ref, wout_ref, o_ref = rest
    if ssm:
        y = y_ref[...] * _silu(z_ref[...])
        gs = D_INNER // SSM_GROUPS
        parts = []
        for g in range(SSM_GROUPS):
            yg = y[:, g * gs:(g + 1) * gs]
            parts.append(yg * lax.rsqrt(jnp.mean(yg * yg, axis=-1, keepdims=True) + EPS))
        yn = (jnp.concatenate(parts, axis=-1) * gn_ref[...]).astype(BF16)
        mix = _dot(yn, wmix_ref[...])
    else:
        mix = (_dot(oa_ref[...].astype(BF16), wmix_ref[0:ATT_W, :])
               + _dot(ob_ref[...].astype(BF16), wmix_ref[ATT_W:2 * ATT_W, :]))
    x1 = x_ref[...] + gt1_ref[0] * _rms(mix, gpost_ref[...])
    h2 = (_rms(x1, gpre_ref[...]) * (1.0 + sc2_ref[0]) + sh2_ref[0]).astype(BF16)
    acc = None
    for lo, hi in ((0, FFN_SPLIT), (FFN_SPLIT, D_FF)):
        gate = _dot(h2, win_ref[:, lo:hi])
        up = _dot(h2, win_ref[:, D_FF + lo:D_FF + hi])
        part = _dot((_silu(gate) * up).astype(BF16), wout_ref[lo:hi, :])
        acc = part if acc is None else acc + part
    o_ref[...] = x1 + gt2_ref[0] * _rms(acc, gfpost_ref[...])


def _post(x2d, mix_in, wmix, mods, gains, win, wout, tm, mod_map, ssm, gn=None):
    m = x2d.shape[0]
    row = lambda i: (i, 0)
    gt1, sc2, sh2, gt2 = mods
    gpost, gpre, gfpost = gains
    mod_spec = pl.BlockSpec((1,) + gt1.shape[1:], mod_map)
    vec = _const_spec((1, D_MODEL))
    in_specs = [pl.BlockSpec((tm, D_MODEL), row)]
    args = [x2d]
    for a in mix_in:
        in_specs.append(pl.BlockSpec((tm, a.shape[1]), row))
        args.append(a)
    if ssm:
        in_specs.append(_const_spec((1, D_INNER)))
        args.append(gn)
    in_specs += [_const_spec(wmix.shape), mod_spec, vec, vec, mod_spec, mod_spec, mod_spec, vec,
                 _const_spec(win.shape), _const_spec(wout.shape)]
    args += [wmix, gt1, gpost, gpre, sc2, sh2, gt2, gfpost, win, wout]
    return pl.pallas_call(
        functools.partial(_post_kernel, ssm=ssm),
        grid=(m // tm,),
        in_specs=in_specs,
        out_specs=pl.BlockSpec((tm, D_MODEL), row),
        out_shape=jax.ShapeDtypeStruct((m, D_MODEL), F32),
        compiler_params=_cparams("arbitrary"),
        name="ssm_out_ffn" if ssm else "attn_out_ffn",
    )(*args)


def _ssm_proj_kernel(*refs, decode, tiles_per_seq):
    if decode:
        (x_ref, sc_ref, sh_ref, g_ref, wz_ref, wx_ref, wdt_ref, dtb_ref, cw_ref, cb_ref, prev_ref,
         z_ref, xc_ref, dt_ref, cs_ref) = refs
    else:
        (x_ref, sc_ref, sh_ref, g_ref, wz_ref, wx_ref, wdt_ref, dtb_ref, cw_ref, cb_ref,
         z_ref, xc_ref, dt_ref, cs_ref, ubuf) = refs
    tm = x_ref.shape[0]
    h = (_rms(x_ref[...], g_ref[...]) * (1.0 + sc_ref[0]) + sh_ref[0]).astype(BF16)
    z_ref[...] = _dot_nt(h, wz_ref[...])
    dt_ref[...] = _softplus(_dot_nt(h, wdt_ref[...]) + dtb_ref[...])
    if decode:
        u = _dot_nt(h, wx_ref[...])
        w = [cw_ref[j:j + 1, :] for j in range(CONV_W)]
        p0, p1, p2 = prev_ref[0], prev_ref[1], prev_ref[2]
        y = cb_ref[...] + w[3] * u + w[2] * p2 + w[1] * p1 + w[0] * p0
        cs_ref[0] = p1
        cs_ref[1] = p2
        cs_ref[2] = u
        xc_ref[...] = _silu(y)
    else:
        pad = 8
        tail = CONV_W - 1
        @pl.when(pl.program_id(0) % tiles_per_seq == 0)
        def _():
            ubuf[0:pad, :] = jnp.zeros((pad, CONV_DIM), F32)
        cw = CONV_DIM // 6
        for c in range(CONV_DIM // cw):
            cols = slice(c * cw, (c + 1) * cw)
            u = _dot_nt(h, wx_ref[cols, :])
            ubuf[pad:pad + tm, cols] = u
            y = cb_ref[:, cols] + cw_ref[tail:CONV_W, cols] * u
            for j in range(tail):
                y = y + cw_ref[j:j + 1, cols] * ubuf[pad - tail + j:pad - tail + j + tm, cols]
            xc_ref[:, cols] = _silu(y)
        cs_ref[:, 0, 0, :] = ubuf[pad + tm - tail:pad + tm, :]
        ubuf[0:pad, :] = ubuf[tm:tm + pad, :]


def _ssm_proj(x2d, sc, sh, g, wz, wx, wdt, dtb, cw, cb, tm, mod_map, seq_len, prev=None):
    m = x2d.shape[0]
    decode = prev is not None
    row = lambda i: (i, 0)
    mod_spec = pl.BlockSpec((1,) + sc.shape[1:], mod_map)
    in_specs = [pl.BlockSpec((tm, D_MODEL), row), mod_spec, mod_spec, _const_spec((1, D_MODEL)),
                _const_spec(wz.shape), _const_spec(wx.shape), _const_spec(wdt.shape), _const_spec((1, LANES)),
                _const_spec(cw.shape), _const_spec((1, CONV_DIM))]
    args = [x2d, sc, sh, g, wz, wx, wdt, dtb, cw, cb]
    scratch = []
    if decode:
        in_specs.append(_const_spec(prev.shape))
        args.append(prev)
        cs_shape = prev.shape
        cs_spec = _const_spec(prev.shape)
        tiles_per_seq = 1
    else:
        tiles_per_seq = seq_len // tm
        nseq = m // seq_len
        cs_shape = (CONV_W - 1, nseq, 1, CONV_DIM)
        cs_spec = pl.BlockSpec((CONV_W - 1, 1, 1, CONV_DIM), lambda i: (0, i // tiles_per_seq, 0, 0))
        scratch = [pltpu.VMEM((tm + 8, CONV_DIM), F32)]
    z, xc, dt, cs = pl.pallas_call(
        functools.partial(_ssm_proj_kernel, decode=decode, tiles_per_seq=tiles_per_seq),
        grid=(m // tm,),
        in_specs=in_specs,
        out_specs=[pl.BlockSpec((tm, D_INNER), row), pl.BlockSpec((tm, CONV_DIM), row),
                   pl.BlockSpec((tm, LANES), row), cs_spec],
        out_shape=[jax.ShapeDtypeStruct((m, D_INNER), F32), jax.ShapeDtypeStruct((m, CONV_DIM), F32),
                   jax.ShapeDtypeStruct((m, LANES), F32), jax.ShapeDtypeStruct(cs_shape, F32)],
        scratch_shapes=scratch,
        compiler_params=_cparams("arbitrary"),
        name="ssm_in_proj_decode" if decode else "ssm_in_proj",
    )(*args)
    return z, xc, dt, cs.reshape(CONV_W - 1, -1, CONV_DIM)


def _head_expand():
    e = np.zeros((LANES, D_INNER), np.float32)
    for hh in range(SSM_HEADS):
        e[hh, hh * SSM_HEAD_DIM:(hh + 1) * SSM_HEAD_DIM] = 1.0
    return e


def _spread(x, e, terms=3):
    parts = _split3(x)[:terms]
    out = _dot(parts[0].astype(BF16), e)
    for t in parts[1:]:
        out = out + _dot(t.astype(BF16), e)
    return out


def _spread_rows(e, x):
    hi, mid, lo = _split3(x)
    return _dot(e, hi.astype(BF16)) + _dot(e, mid.astype(BF16)) + _dot(e, lo.astype(BF16))


def _ssd_kernel(xs_ref, b_ref, c_ref, dt_ref, alog_ref, dsk_ref, e_ref, et_ref, y_ref, ht_ref, hst):
    q = SSD_CHUNK
    gw = D_INNER // SSM_GROUPS
    ci = pl.program_id(1)

    @pl.when(ci == 0)
    def _():
        hst[...] = jnp.zeros_like(hst)

    r = lax.broadcasted_iota(jnp.int32, (q, q), 0)
    c = lax.broadcasted_iota(jnp.int32, (q, q), 1)
    causal = r >= c
    lane = lax.broadcasted_iota(jnp.int32, (q, LANES), 1)
    first_half = lane < SSM_HEAD_DIM
    a = -jnp.exp(alog_ref[...])
    expand = e_ref[...]
    dsk = dsk_ref[...]
    for sub in range(xs_ref.shape[0] // q):
        _ssd_chunk(slice(sub * q, (sub + 1) * q), xs_ref, b_ref, c_ref, dt_ref, y_ref, hst, et_ref,
                   a, expand, dsk, causal, first_half)

    @pl.when(ci == pl.num_programs(1) - 1)
    def _():
        ht_ref[0] = hst[...]


def _ssd_chunk(ts, xs_ref, b_ref, c_ref, dt_ref, y_ref, hst, et_ref, a, expand, dsk, causal, first_half):
    q = SSD_CHUNK
    gw = D_INNER // SSM_GROUPS
    dt = dt_ref[ts, :]
    acum = _dot(causal.astype(F32), dt * a, precision=HIGHEST)
    acum_t = acum.T
    dt_t = dt.T
    a_last = acum[q - 1:q, :]
    w_full = _spread(jnp.exp(a_last - acum) * dt, expand, terms=2)
    ea_full = _spread(jnp.exp(acum), expand, terms=2)
    dec_rows = _spread_rows(et_ref[...], jnp.exp(jnp.broadcast_to(acum_t[:, q - 1:q], (LANES, LANES))))
    xs = xs_ref[ts, :]
    xs_bf = xs.astype(BF16)
    xw = xs * w_full
    for g in range(SSM_GROUPS):
        bg = b_ref[ts, g * D_STATE:(g + 1) * D_STATE].astype(BF16)
        cg = c_ref[ts, g * D_STATE:(g + 1) * D_STATE].astype(BF16)
        cb = _dot_nt(cg, bg)
        rows = slice(g * gw, (g + 1) * gw)
        h_in = hst[rows, :]
        y_inter = _dot_nt(cg, h_in.astype(BF16))
        parts = []
        for pr in range(SSM_HPG // 2):
            h0 = g * SSM_HPG + 2 * pr
            x_pair = xs_bf[:, h0 * SSM_HEAD_DIM:(h0 + 2) * SSM_HEAD_DIM]
            outs = []
            for hh in (h0, h0 + 1):
                seg = acum[:, hh:hh + 1] - acum_t[hh:hh + 1, :]
                mh = cb * jnp.exp(jnp.where(causal, seg, -jnp.inf)) * dt_t[hh:hh + 1, :]
                outs.append(_dot(mh.astype(BF16), x_pair))
            parts.append(jnp.where(first_half, outs[0], outs[1]))
        y_intra = jnp.concatenate(parts, axis=-1)
        y_ref[ts, rows] = y_intra + y_inter * ea_full[:, rows] + dsk[:, rows] * xs[:, rows]
        s_inc = _dot(xw[:, rows].T.astype(BF16), bg)
        hst[rows, :] = h_in * dec_rows[rows, :] + s_inc


def _ssd_prompt(xc, dt, alog, dsk, b, t):
    step_rows = SSD_CHUNKS_PER_STEP * SSD_CHUNK
    nc = t // step_rows
    e = jnp.asarray(_head_expand(), BF16)
    et = jnp.asarray(_head_expand().T.copy(), BF16)
    rowmap = lambda i, c: (i * nc + c, 0)
    bc_w = SSM_GROUPS * D_STATE
    return pl.pallas_call(
        _ssd_kernel,
        grid=(b, nc),
        in_specs=[pl.BlockSpec((step_rows, D_INNER), rowmap),
                  pl.BlockSpec((step_rows, bc_w), lambda i, c: (i * nc + c, D_INNER // bc_w)),
                  pl.BlockSpec((step_rows, bc_w), lambda i, c: (i * nc + c, D_INNER // bc_w + 1)),
                  pl.BlockSpec((step_rows, LANES), rowmap),
                  _const_spec((1, LANES)), _const_spec((1, D_INNER)),
                  _const_spec((LANES, D_INNER)), _const_spec((D_INNER, LANES))],
        out_specs=[pl.BlockSpec((step_rows, D_INNER), rowmap),
                   pl.BlockSpec((1, D_INNER, D_STATE), lambda i, c: (i, 0, 0))],
        out_shape=[jax.ShapeDtypeStruct((b * t, D_INNER), F32),
                   jax.ShapeDtypeStruct((b, D_INNER, D_STATE), F32)],
        scratch_shapes=[pltpu.VMEM((D_INNER, D_STATE), F32)],
        compiler_params=_cparams("arbitrary", "arbitrary"),
        name="ssd_scan",
    )(xc, xc, xc, dt, alog, dsk, e, et)


def _ssd_step_kernel(xs_ref, b_ref, c_ref, dt_ref, alog_ref, dsk_ref, e_ref, et_ref, h0_ref, y_ref, ht_ref):
    gw = D_INNER // SSM_GROUPS
    dt = dt_ref[0]
    a = -jnp.exp(alog_ref[...])
    dec = jnp.exp(dt * a)
    expand = e_ref[...]
    eye = (lax.broadcasted_iota(jnp.int32, (LANES, LANES), 0)
           == lax.broadcasted_iota(jnp.int32, (LANES, LANES), 1))
    dec_col = jnp.sum(jnp.where(eye, jnp.broadcast_to(dec, (LANES, LANES)), 0.0), axis=1, keepdims=True)
    dec_rows = _spread_rows(et_ref[...], jnp.broadcast_to(dec_col, (LANES, LANES)))
    xs = xs_ref[0]
    dtx = xs * _spread(dt, expand)
    eye_g = (lax.broadcasted_iota(jnp.int32, (gw, gw), 0) == lax.broadcasted_iota(jnp.int32, (gw, gw), 1))
    bm = b_ref[0]
    cm = c_ref[0]
    ys = []
    for g in range(SSM_GROUPS):
        rows = slice(g * gw, (g + 1) * gw)
        bg = bm[:, g * D_STATE:(g + 1) * D_STATE]
        cg = cm[:, g * D_STATE:(g + 1) * D_STATE]
        diag = jnp.where(eye_g, jnp.broadcast_to(dtx[:, rows], (gw, gw)), 0.0).astype(BF16)
        outer = _dot(diag, jnp.broadcast_to(bg, (gw, D_STATE)).astype(BF16))
        h_new = h0_ref[0, rows, :] * dec_rows[rows, :] + outer
        ht_ref[0, rows, :] = h_new
        yg = _dot_nt(jnp.broadcast_to(cg, (8, D_STATE)).astype(BF16), h_new.astype(BF16))
        ys.append(yg[0:1, :])
    y_ref[0] = jnp.concatenate(ys, axis=-1) + dsk_ref[...] * xs


def _ssd_step(xc, dt, alog, dsk, h0):
    b = xc.shape[0]
    e = jnp.asarray(_head_expand(), BF16)
    et = jnp.asarray(_head_expand().T.copy(), BF16)
    bc_w = SSM_GROUPS * D_STATE
    xc3 = xc.reshape(b, 1, CONV_DIM)
    return pl.pallas_call(
        _ssd_step_kernel,
        grid=(b,),
        in_specs=[pl.BlockSpec((1, 1, D_INNER), lambda i: (i, 0, 0)),
                  pl.BlockSpec((1, 1, bc_w), lambda i: (i, 0, D_INNER // bc_w)),
                  pl.BlockSpec((1, 1, bc_w), lambda i: (i, 0, D_INNER // bc_w + 1)),
                  pl.BlockSpec((1, 1, LANES), lambda i: (i, 0, 0)),
                  _const_spec((1, LANES)), _const_spec((1, D_INNER)),
                  _const_spec((LANES, D_INNER)), _const_spec((D_INNER, LANES)),
                  pl.BlockSpec((1, D_INNER, D_STATE), lambda i: (i, 0, 0))],
        out_specs=[pl.BlockSpec((1, 1, D_INNER), lambda i: (i, 0, 0)),
                   pl.BlockSpec((1, D_INNER, D_STATE), lambda i: (i, 0, 0))],
        out_shape=[jax.ShapeDtypeStruct((b, 1, D_INNER), F32),
                   jax.ShapeDtypeStruct((b, D_INNER, D_STATE), F32)],
        compiler_params=_cparams("arbitrary"),
        name="ssd_step",
    )(xc3, xc3, xc3, dt.reshape(b, 1, LANES), alog, dsk, e, et, h0)


def _fox_decode_kernel(pt_ref, qrow_ref, kn_ref, vn_ref, lfn_ref, *refs):
    n = PAGES_PER_STEP
    k_refs, v_refs, lf_refs = refs[:n], refs[n:2 * n], refs[2 * n:3 * n]
    o_ref, m_s, l_s, r_s, acc_s, qbd_s = refs[3 * n:]
    j = pl.program_id(1)
    rows8 = lax.broadcasted_iota(jnp.int32, (N_HEADS, HEAD_DIM), 0)

    @pl.when(j == 0)
    def _():
        on_diag = (lax.broadcasted_iota(jnp.int32, (N_HEADS, ATT_W), 1) >> (HEAD_DIM.bit_length() - 1)
                   == lax.broadcasted_iota(jnp.int32, (N_HEADS, ATT_W), 0))
        qrow = qrow_ref[0] * (HEAD_DIM ** -0.5)
        qbd_s[...] = jnp.where(on_diag, jnp.broadcast_to(qrow, (N_HEADS, ATT_W)), 0.0).astype(BF16)
        m_s[...] = jnp.sum(jnp.where(on_diag, jnp.broadcast_to(qrow * kn_ref[0], (N_HEADS, ATT_W)), 0.0),
                           axis=1, keepdims=True)
        l_s[...] = jnp.ones_like(l_s)
        r_s[...] = lfn_ref[0]
        acc_s[...] = vn_ref[0]

    r = lax.broadcasted_iota(jnp.int32, (LANES, LANES), 0)
    c = lax.broadcasted_iota(jnp.int32, (LANES, LANES), 1)
    triu = (r <= c).astype(F32)
    run = r_s[...]
    bias = []
    for i in range(n):
        cs = _dot(lf_refs[i][...], triu, precision=HIGHEST)
        total = cs[:, PAGE_SIZE - 1:PAGE_SIZE]
        bias.append(run + total - cs)
        run = run + total
    r_s[...] = run
    k_all = jnp.concatenate([k_refs[i][...].reshape(ATT_W, PAGE_SIZE).astype(BF16) for i in range(n)], axis=1)
    s = _dot(qbd_s[...], k_all) + jnp.concatenate(bias, axis=1)
    m_old = m_s[...]
    m_new = jnp.maximum(m_old, jnp.max(s, axis=1, keepdims=True))
    alpha = jnp.exp(m_old - m_new)
    p = jnp.exp(s - m_new)
    l_s[...] = alpha * l_s[...] + jnp.sum(p, axis=1, keepdims=True)
    m_s[...] = m_new
    pb = p.astype(BF16)
    acc = acc_s[...] * alpha
    for h in range(N_HEADS):
        v_h = jnp.concatenate([v_refs[i][h].astype(BF16) for i in range(n)], axis=1)
        acc = acc + jnp.where(rows8 == h, _dot_nt(pb, v_h), 0.0)
    acc_s[...] = acc

    @pl.when(j == pl.num_programs(1) - 1)
    def _():
        o_ref[0] = acc_s[...] / l_s[...]


def _fox_decode(qrow, kn_row, v_new, lf_new, pool_k, pool_v, pool_lf, page_table):
    b, n_pages = page_table.shape
    n = PAGES_PER_STEP
    steps = n_pages // n
    row3 = lambda i, j, pt: (i, 0, 0)

    def page_spec(shape, off):
        nd = len(shape)
        return pl.BlockSpec((None, None) + shape,
                            lambda i, j, pt: (0, pt[i, n_pages - 1 - (j * n + off)]) + (0,) * nd)

    in_specs = ([pl.BlockSpec((1, 1, ATT_W), row3)] * 2 + [pl.BlockSpec((1, N_HEADS, HEAD_DIM), row3),
                                                          pl.BlockSpec((1, N_HEADS, 1), row3)]
                + [page_spec((N_HEADS, HEAD_DIM, PAGE_SIZE), i) for i in range(n)] * 2
                + [page_spec((N_HEADS, PAGE_SIZE), i) for i in range(n)])
    grid_spec = pltpu.PrefetchScalarGridSpec(
        num_scalar_prefetch=1, grid=(b, steps), in_specs=in_specs,
        out_specs=pl.BlockSpec((1, N_HEADS, HEAD_DIM), row3),
        scratch_shapes=[pltpu.VMEM((N_HEADS, 1), F32)] * 3 + [pltpu.VMEM((N_HEADS, HEAD_DIM), F32),
                                                              pltpu.VMEM((N_HEADS, ATT_W), BF16)])
    return pl.pallas_call(
        _fox_decode_kernel, grid_spec=grid_spec,
        out_shape=jax.ShapeDtypeStruct((b, N_HEADS, HEAD_DIM), F32),
        compiler_params=_cparams("arbitrary", "arbitrary"),
        name="fox_decode_attn",
    )(page_table, qrow, kn_row, v_new, lf_new, *([pool_k] * n), *([pool_v] * n), *([pool_lf] * n))


def _moba_gate_kernel(pt_ref, qcol_ref, *refs):
    n = GATE_PAGES_PER_STEP
    k_refs = refs[:n]
    idx_ref, gate_s, qb_s = refs[n:]
    j = pl.program_id(1)
    ppb = MOBA_BLOCK // PAGE_SIZE
    lane = lax.broadcasted_iota(jnp.int32, (N_HEADS, LANES), 1)

    @pl.when(j == 0)
    def _():
        gate_s[...] = jnp.full_like(gate_s, -jnp.inf)
        qb_s[...] = jnp.broadcast_to(qcol_ref[0], (ATT_W, PAGE_SIZE))

    qb = qb_s[...]
    sub = 8
    fold = (lax.broadcasted_iota(jnp.int32, (N_HEADS, N_HEADS * sub), 1) >> 3
            == lax.broadcasted_iota(jnp.int32, (N_HEADS, N_HEADS * sub), 0)).astype(F32)
    gates = gate_s[...]
    for blk in range(n // ppb):
        ksum = None
        for i in range(ppb):
            page = k_refs[blk * ppb + i][...].reshape(ATT_W, PAGE_SIZE)
            ksum = page if ksum is None else ksum + page
        part = jnp.sum((ksum * qb).reshape(N_HEADS, HEAD_DIM // sub, sub, PAGE_SIZE), axis=1)
        per_head = _dot(fold, part.reshape(N_HEADS * sub, PAGE_SIZE), precision=HIGHEST)
        g = jnp.sum(per_head, axis=1, keepdims=True) * (1.0 / MOBA_BLOCK)
        gates = jnp.where(lane == j * (n // ppb) + blk, g, gates)
    gate_s[...] = gates

    @pl.when(j == pl.num_programs(1) - 1)
    def _():
        gate = gate_s[...]
        lane_f = lane.astype(F32)
        picks = jnp.zeros((N_HEADS, LANES), F32)
        for k in range(MOBA_TOPK):
            best = jnp.max(gate, axis=1, keepdims=True)
            first = jnp.min(jnp.where(gate == best, lane_f, float(LANES)), axis=1, keepdims=True)
            picks = jnp.where(lane == k, first, picks)
            gate = jnp.where(lane_f == first, -jnp.inf, gate)
        idx_ref[0] = picks.astype(jnp.int32)


def _moba_gate(qcol, pool_k, page_table):
    b, n_pages = page_table.shape
    n = GATE_PAGES_PER_STEP
    steps = n_pages // n
    row3 = lambda i, j, pt: (i, 0, 0)
    in_specs = [pl.BlockSpec((1, ATT_W, 1), row3)] + [
        pl.BlockSpec((None, None, N_HEADS, HEAD_DIM, PAGE_SIZE),
                     functools.partial(lambda i, j, pt, off: (0, pt[i, j * n + off], 0, 0, 0), off=off))
        for off in range(n)]
    grid_spec = pltpu.PrefetchScalarGridSpec(
        num_scalar_prefetch=1, grid=(b, steps), in_specs=in_specs,
        out_specs=pl.BlockSpec((1, N_HEADS, LANES), row3),
        scratch_shapes=[pltpu.VMEM((N_HEADS, LANES), F32), pltpu.VMEM((ATT_W, PAGE_SIZE), F32)])
    return pl.pallas_call(
        _moba_gate_kernel, grid_spec=grid_spec,
        out_shape=jax.ShapeDtypeStruct((b, N_HEADS, LANES), jnp.int32),
        compiler_params=_cparams("arbitrary", "arbitrary"),
        name="moba_decode_gate",
    )(page_table, qcol, *([pool_k] * n))


def _moba_decode_kernel(pt_ref, idx_ref, q_ref, kn_ref, vn_ref, *refs):
    npg = MOBA_TOPK * (MOBA_BLOCK // PAGE_SIZE)
    nh = MOBA_HEADS_PER_STEP
    k_refs, v_refs = refs[:nh * npg], refs[nh * npg:2 * nh * npg]
    o_ref, acc_s = refs[2 * nh * npg:]
    grp = pl.program_id(1)
    q8 = q_ref[0] * (HEAD_DIM ** -0.5)
    q8_bf = q8.astype(BF16)
    rows8 = lax.broadcasted_iota(jnp.int32, (N_HEADS, HEAD_DIM), 0)
    s_new = jnp.sum(q8 * kn_ref[0], axis=1, keepdims=True)

    @pl.when(grp == 0)
    def _():
        acc_s[...] = jnp.zeros_like(acc_s)

    out = acc_s[...]
    for hl in range(nh):
        k_h = jnp.concatenate([k_refs[hl * npg + i][...].astype(BF16) for i in range(npg)], axis=1)
        v_h = jnp.concatenate([v_refs[hl * npg + i][...].astype(BF16) for i in range(npg)], axis=1)
        s = _dot(q8_bf, k_h)
        m = jnp.maximum(jnp.max(s, axis=1, keepdims=True), s_new)
        p = jnp.exp(s - m)
        p_new = jnp.exp(s_new - m)
        denom = jnp.sum(p, axis=1, keepdims=True) + p_new
        out_h = (_dot_nt(p.astype(BF16), v_h) + p_new * vn_ref[0]) / denom
        out = jnp.where(rows8 == grp * nh + hl, out_h, out)
    acc_s[...] = out

    @pl.when(grp == pl.num_programs(1) - 1)
    def _():
        o_ref[0] = out


def _moba_decode(q, k_new, v_new, idx, pool_k, pool_v, page_table):
    b, _ = page_table.shape
    ppb = MOBA_BLOCK // PAGE_SIZE
    npg = MOBA_TOPK * ppb
    nh = MOBA_HEADS_PER_STEP
    row3 = lambda i, g, pt, ix: (i, 0, 0)

    def page(hl, k, off):
        def index(i, g, pt, ix):
            h = g * nh + hl
            return (0, pt[i, ix[i, h * MOBA_TOPK + k] * ppb + off], h, 0, 0)
        return pl.BlockSpec((None, None, None, HEAD_DIM, PAGE_SIZE), index)

    slabs = [page(hl, k, off) for hl in range(nh) for k in range(MOBA_TOPK) for off in range(ppb)]
    in_specs = [pl.BlockSpec((1, N_HEADS, HEAD_DIM), row3)] * 3 + slabs * 2
    grid_spec = pltpu.PrefetchScalarGridSpec(
        num_scalar_prefetch=2, grid=(b, N_HEADS // nh), in_specs=in_specs,
        out_specs=pl.BlockSpec((1, N_HEADS, HEAD_DIM), row3),
        scratch_shapes=[pltpu.VMEM((N_HEADS, HEAD_DIM), F32)])
    return pl.pallas_call(
        _moba_decode_kernel, grid_spec=grid_spec,
        out_shape=jax.ShapeDtypeStruct((b, N_HEADS, HEAD_DIM), F32),
        compiler_params=_cparams("arbitrary", "arbitrary"),
        name="moba_decode_attn",
    )(page_table, idx, q, k_new, v_new, *([pool_k] * len(slabs)), *([pool_v] * len(slabs)))


def _pad_lanes(a, width=LANES):
    return jnp.pad(a, [(0, 0)] * (a.ndim - 1) + [(0, width - a.shape[-1])])


def _run_group(x, ada, weights, decode, caches=None):
    b, t, _ = x.shape
    m = b * t
    x2d = x.reshape(m, D_MODEL)
    if decode:
        tm = m
        mod_map = lambda i: (0, 0, 0)
        as_mod = lambda a: a.reshape(1, b, D_MODEL)
    else:
        tm = 512
        tps = t // tm
        mod_map = lambda i: (i // tps, 0, 0)
        as_mod = lambda a: a.reshape(b, 1, D_MODEL)
    row = lambda v: v.reshape(1, -1)

    sh1, sc1, gt1, sh2, sc2, gt2 = [as_mod(a) for a in jnp.split(ada[0], 6, axis=-1)]
    pos = (caches["page_table"].shape[1] * PAGE_SIZE if decode else 0) + np.arange(t)
    qa, qb, ka, va, kb, vb, lf = _attn_proj(
        x2d, sc1, sh1, row(weights["g_mix_pre"][0]), weights["wt_att6"], weights["wt_att_f"],
        weights["b_fox_f"], pos, tm, mod_map, t, prompt=not decode)
    if decode:
        hd = lambda a: a.reshape(b, N_HEADS, HEAD_DIM)
        pt = caches["page_table"]
        idx = _moba_gate(qa.reshape(b, ATT_W, 1), caches["moba_k"], pt)[:, :, :MOBA_TOPK]
        oa = _moba_decode(hd(qa), hd(ka), hd(va), idx.reshape(b, N_HEADS * MOBA_TOPK),
                          caches["moba_k"], caches["moba_v"], pt)
        ob = _fox_decode(qb.reshape(b, 1, ATT_W), kb.reshape(b, 1, ATT_W), hd(vb), lf.reshape(b, N_HEADS, 1),
                         caches["fox_k"], caches["fox_v"], caches["fox_lf"], pt)
        oa, ob = oa.reshape(m, ATT_W), ob.reshape(m, ATT_W)
        rows5 = lambda a: a.reshape(1, b, t, N_HEADS, HEAD_DIM)
        lf_out = lf.reshape(1, b, t, N_HEADS)
    else:
        ft, fcol = _fox_cumsum(lf, b, t)
        oa = _attn_prompt(qa, ka, va, b, t, fox=False)
        ob = _attn_prompt(qb, kb, vb, b, t, fox=True, ft=ft, fcol=fcol)
        rows5 = lambda a: a.reshape(1, b, N_HEADS, HEAD_DIM, t).transpose(0, 1, 4, 2, 3)
        lf_out = lf.reshape(1, b, N_HEADS, t).transpose(0, 1, 3, 2)
    x2d = _post(x2d, [oa, ob], weights["w_att_out"], (gt1, sc2, sh2, gt2),
                (row(weights["g_mix_post"][0]), row(weights["g_ffn_pre"][0]), row(weights["g_ffn_post"][0])),
                weights["w_ffn_in"][0], weights["w_ffn_out"][0], tm, mod_map, ssm=False)

    sh1, sc1, gt1, sh2, sc2, gt2 = [as_mod(a) for a in jnp.split(ada[1], 6, axis=-1)]
    prev = caches["state_conv"] if decode else None
    z, xc, dt, conv_state = _ssm_proj(
        x2d, sc1, sh1, row(weights["g_mix_pre"][1]), weights["wt_ssm_z"], weights["wt_ssm_x"],
        weights["wt_ssm_dt"], weights["dt_bias"], weights["conv_w"], weights["conv_b"], tm, mod_map, t, prev=prev)
    if decode:
        y, h_t = _ssd_step(xc, dt, weights["a_log"], weights["d_skip"], caches["state_ssm"])
        y = y.reshape(m, D_INNER)
    else:
        y, h_t = _ssd_prompt(xc, dt, weights["a_log"], weights["d_skip"], b, t)
    x2d = _post(x2d, [y, z], weights["w_ssm_out"], (gt1, sc2, sh2, gt2),
                (row(weights["g_mix_post"][1]), row(weights["g_ffn_pre"][1]), row(weights["g_ffn_post"][1])),
                weights["w_ffn_in"][1], weights["w_ffn_out"][1], tm, mod_map, ssm=True,
                gn=row(weights["g_ssm_norm"]))

    return (x2d.reshape(b, t, D_MODEL), rows5(ka), rows5(va), rows5(kb), rows5(vb), lf_out,
            h_t.reshape(1, b, SSM_HEADS, SSM_HEAD_DIM, D_STATE),
            conv_state.transpose(1, 0, 2).reshape(1, b, CONV_W - 1, CONV_DIM))


def kernel(x_prompt, x_sample, cache_moba_k, cache_moba_v, cache_fox_k, cache_fox_v, cache_fox_logf, state_ssm, state_conv, page_table, c_prompt, c_sample, w_ada, b_ada, g_mix_pre, g_mix_post, g_ffn_pre, g_ffn_post, w_att_in, b_fox_f, w_att_out, w_ssm_in, conv_w, conv_b, dt_bias, a_log, d_skip, g_ssm_norm, w_ssm_out, w_ffn_in, w_ffn_out):
    bp = x_prompt.shape[0]
    bs = x_sample.shape[0]
    ada = _ada(jnp.concatenate([c_prompt, c_sample], axis=0), w_ada, b_ada)
    wt_att = w_att_in[0].T
    wt_ssm = w_ssm_in[0].T
    n_dt = wt_ssm.shape[0] - D_INNER - CONV_DIM
    weights = dict(
        g_mix_pre=g_mix_pre, g_mix_post=g_mix_post, g_ffn_pre=g_ffn_pre, g_ffn_post=g_ffn_post,
        wt_att6=wt_att[:6 * ATT_W].astype(BF16),
        wt_att_f=wt_att[6 * ATT_W:],
        b_fox_f=b_fox_f[0],
        w_att_out=w_att_out[0].astype(BF16),
        wt_ssm_z=wt_ssm[:D_INNER].astype(BF16),
        wt_ssm_x=wt_ssm[D_INNER:D_INNER + CONV_DIM].astype(BF16),
        wt_ssm_dt=jnp.pad(wt_ssm[D_INNER + CONV_DIM:], ((0, LANES - n_dt), (0, 0))).astype(BF16),
        dt_bias=_pad_lanes(dt_bias[0].reshape(1, -1)),
        conv_w=conv_w[0], conv_b=conv_b[0].reshape(1, -1),
        a_log=_pad_lanes(a_log[0].reshape(1, -1)),
        d_skip=jnp.repeat(d_skip[0], SSM_HEAD_DIM).reshape(1, -1),
        g_ssm_norm=g_ssm_norm[0],
        w_ssm_out=w_ssm_out[0].astype(BF16),
        w_ffn_in=w_ffn_in.astype(BF16), w_ffn_out=w_ffn_out.astype(BF16),
    )
    kv_t = lambda a: a.transpose(0, 1, 3, 4, 2)
    caches = dict(page_table=page_table, moba_k=kv_t(cache_moba_k), moba_v=kv_t(cache_moba_v),
                  fox_k=kv_t(cache_fox_k), fox_v=kv_t(cache_fox_v),
                  fox_lf=cache_fox_logf.transpose(0, 1, 3, 2),
                  state_ssm=state_ssm[0].reshape(bs, D_INNER, D_STATE),
                  state_conv=state_conv[0].transpose(1, 0, 2))
    prompt = _run_group(x_prompt, ada[:, :bp], weights, decode=False)
    sample = _run_group(x_sample, ada[:, bp:], weights, decode=True, caches=caches)
    return (prompt[0], sample[0]) + prompt[1:] + sample[1:]
```
